```python
import math
import jax, jax.numpy as jnp
from jax import lax
import numpy as np

D_MODEL = 1024
BATCH = 4
SEQ = 4096
DEPTH = 2

CHUNK = 64
DN_HEADS = 8
DN_HEAD_DIM = 128
DN_WIDTH = DN_HEADS * DN_HEAD_DIM
CONV_WIDTH = 4
AT_HEADS = 8
AT_HEAD_DIM = 64
AT_WIDTH = AT_HEADS * AT_HEAD_DIM
LEFT_CHUNKS = 8
BAND = LEFT_CHUNKS + 1
REL_CLIP = 256
N_EXPERTS = 64
TOP_K = 6
EXPERT_DIM = 256
SHARED_DIM = 256
ROUTED_SCALE = 2.5
MOE_BLOCK = 128
ALPHA = (2 * DEPTH) ** 0.25
BETA_INIT = (8 * DEPTH) ** -0.25
LN_EPS = 1e-5
RMS_EPS = 1e-6
L2_EPS = 1e-6
NEG_INF = -1e30

_IN_SPLITS = (DN_WIDTH, DN_WIDTH, DN_WIDTH, DN_WIDTH, DN_HEADS, DN_HEADS,
              AT_WIDTH, AT_WIDTH, AT_WIDTH, D_MODEL, D_MODEL)
IN_COLS = sum(_IN_SPLITS)
SPLIT_POINTS = tuple(int(v) for v in np.cumsum(_IN_SPLITS)[:-1])

kernel_name = "hybrid_deltanet_chunkattn_moe_deepnorm"


def layer_norm(x, g, b):
    xf = x.astype(jnp.float32)
    mu = jnp.mean(xf, axis=-1, keepdims=True)
    var = jnp.mean(jnp.square(xf - mu), axis=-1, keepdims=True)
    y = (xf - mu) * lax.rsqrt(var + LN_EPS) * g.astype(jnp.float32) + b.astype(jnp.float32)
    return y.astype(x.dtype)


def causal_depthwise_conv(x, w):
    T = x.shape[1]
    xp = jnp.pad(x, ((0, 0), (CONV_WIDTH - 1, 0), (0, 0)))
    return sum(xp[:, i:i + T] * w[i] for i in range(CONV_WIDTH))


def l2_normalize(x):
    return x * lax.rsqrt(jnp.sum(jnp.square(x), axis=-1, keepdims=True) + L2_EPS)


def delta_rule_chunked(q, k, v, g, beta):
    B, T, H, dk = q.shape
    dv = v.shape[-1]
    nc = T // CHUNK

    def chunks(a):
        return jnp.moveaxis(a.reshape(B, nc, CHUNK, H, -1), 3, 1)

    q, k, v = chunks(q), chunks(k), chunks(v)
    g = jnp.moveaxis(g.reshape(B, nc, CHUNK, H), 3, 1)
    beta = jnp.moveaxis(beta.reshape(B, nc, CHUNK, H), 3, 1)
    gc = jnp.cumsum(g, axis=-1)
    pos = jnp.arange(CHUNK)
    incl = pos[:, None] >= pos[None, :]
    strict = pos[:, None] > pos[None, :]
    decay = jnp.exp(jnp.where(incl, gc[..., :, None] - gc[..., None, :], -jnp.inf))

    kb = k * beta[..., None]
    lower = jnp.where(strict, jnp.einsum('bhnid,bhnjd->bhnij', kb, k) * decay, 0.0)
    eye = jnp.broadcast_to(jnp.eye(CHUNK, dtype=lower.dtype), lower.shape)
    t_inv = lax.linalg.triangular_solve(eye + lower, eye, left_side=True, lower=True,
                                        unit_diagonal=True)
    w = t_inv @ (kb * jnp.exp(gc)[..., None])
    u = t_inv @ (v * beta[..., None])
    qk = jnp.einsum('bhnid,bhnjd->bhnij', q, k) * decay
    q_dec = q * jnp.exp(gc)[..., None]
    k_dec = k * jnp.exp(gc[..., -1:] - gc)[..., None]
    chunk_decay = jnp.exp(gc[..., -1])

    xs = tuple(jnp.moveaxis(a, 2, 0) for a in (u, w, q_dec, qk, k_dec, chunk_decay))

    def step(S, inp):
        u_n, w_n, qd_n, qk_n, kd_n, dec_n = inp
        v_new = u_n - w_n @ S
        o_n = qd_n @ S + qk_n @ v_new
        S = S * dec_n[..., None, None] + jnp.swapaxes(kd_n, -1, -2) @ v_new
        return S, o_n

    S0 = jnp.zeros((B, H, dk, dv), jnp.float32)
    _, o = lax.scan(step, S0, xs)
    return o.transpose(1, 0, 3, 2, 4).reshape(B, T, H, dv)


def gated_deltanet(q_in, k_in, v_in, z, b_in, a_in, conv_w, a_log, dt_bias, norm_w):
    B, T, _ = q_in.shape
    qkv = jax.nn.silu(causal_depthwise_conv(jnp.concatenate([q_in, k_in, v_in], axis=-1), conv_w))
    qkv = qkv.astype(jnp.float32).reshape(B, T, 3, DN_HEADS, DN_HEAD_DIM)
    q = l2_normalize(qkv[:, :, 0]) * (DN_HEAD_DIM ** -0.5)
    k = l2_normalize(qkv[:, :, 1])
    v = qkv[:, :, 2]
    beta = jax.nn.sigmoid(b_in.astype(jnp.float32))
    g = -jnp.exp(a_log.astype(jnp.float32)) * jax.nn.softplus(a_in.astype(jnp.float32)
                                                              + dt_bias.astype(jnp.float32))
    o = delta_rule_chunked(q, k, v, g, beta)
    o = o * lax.rsqrt(jnp.mean(jnp.square(o), axis=-1, keepdims=True) + RMS_EPS)
    o = o * norm_w.astype(jnp.float32) * jax.nn.silu(z.astype(jnp.float32).reshape(o.shape))
    return o.reshape(B, T, DN_WIDTH).astype(q_in.dtype)


def chunk_band_attention(q_in, k_in, v_in, rel_table):
    B, T, _ = q_in.shape
    nc = T // CHUNK
    q = q_in.reshape(B, nc, CHUNK, AT_HEADS, AT_HEAD_DIM)
    pad = ((0, 0), (LEFT_CHUNKS, 0), (0, 0), (0, 0), (0, 0))
    k = jnp.pad(k_in.reshape(B, nc, CHUNK, AT_HEADS, AT_HEAD_DIM), pad)
    v = jnp.pad(v_in.reshape(B, nc, CHUNK, AT_HEADS, AT_HEAD_DIM), pad)
    band_idx = jnp.arange(nc)[:, None] + jnp.arange(BAND)[None, :]
    kb = k[:, band_idx].reshape(B, nc, BAND * CHUNK, AT_HEADS, AT_HEAD_DIM)
    vb = v[:, band_idx].reshape(B, nc, BAND * CHUNK, AT_HEADS, AT_HEAD_DIM)
    scores = jnp.einsum('bnqhd,bnkhd->bhnqk', q, kb).astype(jnp.float32) * (AT_HEAD_DIM ** -0.5)
    q_pos = jnp.arange(CHUNK)
    k_pos = jnp.arange(BAND * CHUNK) - LEFT_CHUNKS * CHUNK
    rel = jnp.clip(k_pos[None, :] - q_pos[:, None], -REL_CLIP, REL_CLIP) + REL_CLIP
    bias = rel_table.astype(jnp.float32)[:, rel]
    key_valid = jnp.repeat(band_idx >= LEFT_CHUNKS, CHUNK, axis=1)
    scores = jnp.where(key_valid[None, None, :, None, :], scores + bias[None, :, None], NEG_INF)
    p = jax.nn.softmax(scores, axis=-1).astype(vb.dtype)
    o = jnp.einsum('bhnqk,bnkhd->bnqhd', p, vb)
    return o.reshape(B, T, AT_WIDTH)


def moe_ffn(x, router_w, router_bias, w_gate_up, w_down, ws_gate_up, ws_down):
    B, T, D = x.shape
    N = B * T
    xt = x.reshape(N, D)
    scores = jax.nn.sigmoid((xt @ router_w).astype(jnp.float32))
    _, sel = lax.top_k(scores + router_bias.astype(jnp.float32), TOP_K)
    wsel = jnp.take_along_axis(scores, sel, axis=-1)
    wsel = wsel / jnp.sum(wsel, axis=-1, keepdims=True) * ROUTED_SCALE

    nk = N * TOP_K
    flat_e = sel.reshape(nk)
    flat_t = jnp.repeat(jnp.arange(N, dtype=jnp.int32), TOP_K)
    flat_w = wsel.reshape(nk)
    order = jnp.argsort(flat_e)
    e_s, t_s, w_s = flat_e[order], flat_t[order], flat_w[order]
    counts = jnp.bincount(flat_e, length=N_EXPERTS)
    start = jnp.cumsum(counts) - counts
    padded = (counts + MOE_BLOCK - 1) // MOE_BLOCK * MOE_BLOCK
    pend = jnp.cumsum(padded)
    pstart = pend - padded
    dest = pstart[e_s] + (jnp.arange(nk) - start[e_s])
    n_blocks = -(-(nk + N_EXPERTS * (MOE_BLOCK - 1)) // MOE_BLOCK)
    n_rows = n_blocks * MOE_BLOCK
    row_tok = jnp.full((n_rows,), N, jnp.int32).at[dest].set(t_s)
    row_w = jnp.zeros((n_rows,), jnp.float32).at[dest].set(w_s)
    blk_e = jnp.minimum(jnp.searchsorted(pend, jnp.arange(n_blocks) * MOE_BLOCK, side='right'),
                        N_EXPERTS - 1)
    x_pad = jnp.concatenate([xt, jnp.zeros((1, D), xt.dtype)], axis=0)

    def block_fn(args):
        tok, wt, e = args
        gu = x_pad[tok] @ w_gate_up[e]
        h = jax.nn.silu(gu[:, :EXPERT_DIM]) * gu[:, EXPERT_DIM:]
        return (h @ w_down[e]) * wt[:, None]

    y_rows = lax.map(block_fn, (row_tok.reshape(n_blocks, MOE_BLOCK),
                                row_w.reshape(n_blocks, MOE_BLOCK), blk_e))
    routed = jax.ops.segment_sum(y_rows.reshape(n_rows, D), row_tok, num_segments=N + 1)[:N]
    gus = xt @ ws_gate_up
    shared = (jax.nn.silu(gus[:, :SHARED_DIM]) * gus[:, SHARED_DIM:]) @ ws_down
    return (routed.astype(x.dtype) + shared).reshape(B, T, D)


def setup_inputs(seed: int = 0) -> dict:
    key = jax.random.key(seed)
    ks = jax.random.split(key, 24)
    L, D = DEPTH, D_MODEL
    nrm = jax.random.normal
    dt = jnp.exp(jax.random.uniform(ks[4], (L, DN_HEADS)) * (math.log(0.1) - math.log(0.001))
                 + math.log(0.001))
    return {
        'x': nrm(ks[0], (BATCH, SEQ, D), jnp.float32),
        'w_in': nrm(ks[1], (L, D, IN_COLS), jnp.float32) * D ** -0.5,
        'conv_w': nrm(ks[2], (L, CONV_WIDTH, 3 * DN_WIDTH), jnp.float32) * CONV_WIDTH ** -0.5,
        'a_log': jnp.log(jax.random.uniform(ks[3], (L, DN_HEADS), jnp.float32, 1.0, 16.0)),
        'dt_bias': dt + jnp.log(-jnp.expm1(-dt)),
        'dn_norm_w': 1.0 + 0.02 * nrm(ks[5], (L, DN_HEAD_DIM), jnp.float32),
        'rel_table': 0.5 * nrm(ks[6], (L, AT_HEADS, 2 * REL_CLIP + 1), jnp.float32),
        'w_branch_a': nrm(ks[7], (L, DN_WIDTH, D), jnp.float32) * DN_WIDTH ** -0.5,
        'w_branch_b': nrm(ks[8], (L, AT_WIDTH, D), jnp.float32) * AT_WIDTH ** -0.5,
        'w_out': nrm(ks[9], (L, D, D), jnp.float32) * D ** -0.5 * BETA_INIT,
        'ln1_g': 1.0 + 0.02 * nrm(ks[10], (L, D), jnp.float32),
        'ln1_b': 0.02 * nrm(ks[11], (L, D), jnp.float32),
        'router_w': nrm(ks[12], (L, D, N_EXPERTS), jnp.float32) * D ** -0.5,
        'router_bias': 0.01 * nrm(ks[13], (L, N_EXPERTS), jnp.float32),
        'expert_w_gate_up': nrm(ks[14], (L, N_EXPERTS, D, 2 * EXPERT_DIM), jnp.float32) * D ** -0.5,
        'expert_w_down': nrm(ks[15], (L, N_EXPERTS, EXPERT_DIM, D), jnp.float32)
                         * EXPERT_DIM ** -0.5 * BETA_INIT,
        'shared_w_gate_up': nrm(ks[16], (L, D, 2 * SHARED_DIM), jnp.float32) * D ** -0.5,
        'shared_w_down': nrm(ks[17], (L, SHARED_DIM, D), jnp.float32) * SHARED_DIM ** -0.5 * BETA_INIT,
        'ln2_g': 1.0 + 0.02 * nrm(ks[18], (L, D), jnp.float32),
        'ln2_b': 0.02 * nrm(ks[19], (L, D), jnp.float32),
    }


def reference(x, w_in, conv_w, a_log, dt_bias, dn_norm_w, rel_table, w_branch_a, w_branch_b,
              w_out, ln1_g, ln1_b, router_w, router_bias, expert_w_gate_up, expert_w_down,
              shared_w_gate_up, shared_w_down, ln2_g, ln2_b):
    for l in range(DEPTH):
        proj = x @ w_in[l]
        dq, dk, dv, dz, db, da, aq, ak, av, ga, gb = jnp.split(proj, SPLIT_POINTS, axis=-1)
        ya = gated_deltanet(dq, dk, dv, dz, db, da, conv_w[l], a_log[l], dt_bias[l],
                            dn_norm_w[l]) @ w_branch_a[l]
        yb = chunk_band_attention(aq, ak, av, rel_table[l]) @ w_branch_b[l]
        mix = (jax.nn.sigmoid(ga) * ya + jax.nn.sigmoid(gb) * yb) @ w_out[l]
        x = layer_norm(ALPHA * x + mix, ln1_g[l], ln1_b[l])
        ffn = moe_ffn(x, router_w[l], router_bias[l], expert_w_gate_up[l], expert_w_down[l],
                      shared_w_gate_up[l], shared_w_down[l])
        x = layer_norm(ALPHA * x + ffn, ln2_g[l], ln2_b[l])
    return x
```

```python
import functools

import jax
import jax.numpy as jnp
from jax import lax
from jax.experimental import pallas as pl
from jax.experimental.pallas import tpu as pltpu

F32 = jnp.float32
BF16 = jnp.bfloat16
I32 = jnp.int32

D_MODEL = 1024
DEPTH = 2
CHUNK = 64
DN_HEADS = 8
DN_HEAD_DIM = 128
DN_WIDTH = DN_HEADS * DN_HEAD_DIM
CONV_WIDTH = 4
AT_HEADS = 8
AT_HEAD_DIM = 64
AT_WIDTH = AT_HEADS * AT_HEAD_DIM
LEFT_CHUNKS = 8
REL_CLIP = 256
N_EXPERTS = 64
TOP_K = 6
EXPERT_DIM = 256
SHARED_DIM = 256
ROUTED_SCALE = 2.5
ALPHA = (2 * DEPTH) ** 0.25
LN_EPS = 1e-5
RMS_EPS = 1e-6
L2_EPS = 1e-6
NEG_INF = -1e30

LANES = 128
SUBLANES = 8
ROW_TILES = D_MODEL // LANES

DN_TB = 256
AT_QB = 256
AT_KB = 3 * AT_QB
GROUP = 1024
FFN_BM = 128
GROUP_ROWS = GROUP * TOP_K + N_EXPERTS * (SUBLANES - 1) + FFN_BM
GROUP_ROWS = -(-GROUP_ROWS // SUBLANES) * SUBLANES
VMEM_LIMIT = 56 * 1024 * 1024

COL_Q, COL_K, COL_V, COL_Z, COL_GA, COL_GB = 0, 1, 2, 3, 4, 5
COL_AQ, COL_AK, COL_AV = 12, 13, 14
BIG_COLS = 4 * DN_WIDTH + 2 * D_MODEL + 3 * AT_WIDTH


def _sigmoid(x):
    return 1.0 / (1.0 + jnp.exp(-x))


def _silu(x):
    return x * _sigmoid(x)


def _softplus(x):
    return jnp.maximum(x, 0.0) + jnp.log1p(jnp.exp(-jnp.abs(x)))


def _bdot(a, b):
    return jnp.dot(a.astype(BF16), b.astype(BF16), preferred_element_type=F32)


def _bdot_nt(a, b):
    return lax.dot_general(a.astype(BF16), b.astype(BF16), (((1,), (1,)), ((), ())),
                           preferred_element_type=F32)


def _cparams(sem):
    return pltpu.CompilerParams(dimension_semantics=sem, vmem_limit_bytes=VMEM_LIMIT)


def _mm_kernel(x_ref, w_ref, o_ref):
    o_ref[...] = jnp.dot(x_ref[...], w_ref[...], preferred_element_type=F32).astype(o_ref.dtype)


def _matmul(x, w, out_dtype, tm, tn):
    n, k = x.shape
    m = w.shape[1]
    return pl.pallas_call(
        _mm_kernel,
        grid=(n // tm, m // tn),
        in_specs=[pl.BlockSpec((tm, k), lambda i, j: (i, 0)),
                  pl.BlockSpec((k, tn), lambda i, j: (0, j))],
        out_specs=pl.BlockSpec((tm, tn), lambda i, j: (i, j)),
        out_shape=jax.ShapeDtypeStruct((n, m), out_dtype),
        compiler_params=_cparams(("parallel", "parallel")),
        name="inproj",
    )(x, w)


def _gates_kernel(x_ref, wc_ref, wr_ref, pc_ref, pr_ref, cg_ref, rg_ref, *, tm):
    x = x_ref[...]
    r = lax.broadcasted_iota(I32, (tm, tm), 0)
    c = lax.broadcasted_iota(I32, (tm, tm), 1)
    same = (r // CHUNK) == (c // CHUNK)

    pc = jnp.dot(x, wc_ref[...], preferred_element_type=F32)
    lane = lax.broadcasted_iota(I32, (tm, LANES), 1)
    col = jnp.where(lane < DN_HEADS, _sigmoid(pc),
                    -pc_ref[0:1, :] * _softplus(pc + pc_ref[1:2, :]))
    lower = (same & (r >= c)).astype(F32)
    gcc = jnp.dot(lower, col, preferred_element_type=F32, precision=lax.Precision.HIGHEST)
    cg_ref[...] = jnp.where(lane < DN_HEADS, col, gcc)

    pr = lax.dot_general(wr_ref[...], x, (((1,), (1,)), ((), ())),
                         preferred_element_type=F32)
    sub = lax.broadcasted_iota(I32, (2 * DN_HEADS, tm), 0)
    row = jnp.where(sub < DN_HEADS, _sigmoid(pr),
                    -pr_ref[:, 0:1] * _softplus(pr + pr_ref[:, 1:2]))
    upper = (same & (r <= c)).astype(F32)
    gcr = jnp.dot(row, upper, preferred_element_type=F32, precision=lax.Precision.HIGHEST)
    rg_ref[...] = jnp.where(sub < DN_HEADS, row, gcr)


def _gates(xb, w_col, w_row, p_col, p_row, tm=256):
    n = xb.shape[0]
    return pl.pallas_call(
        functools.partial(_gates_kernel, tm=tm),
        grid=(n // tm,),
        in_specs=[pl.BlockSpec((tm, D_MODEL), lambda i: (i, 0)),
                  pl.BlockSpec((D_MODEL, LANES), lambda i: (0, 0)),
                  pl.BlockSpec((2 * DN_HEADS, D_MODEL), lambda i: (0, 0)),
                  pl.BlockSpec((2, LANES), lambda i: (0, 0)),
                  pl.BlockSpec((2 * DN_HEADS, LANES), lambda i: (0, 0))],
        out_specs=[pl.BlockSpec((tm, LANES), lambda i: (i, 0)),
                   pl.BlockSpec((2 * DN_HEADS, tm), lambda i: (0, i))],
        out_shape=[jax.ShapeDtypeStruct((n, LANES), F32),
                   jax.ShapeDtypeStruct((2 * DN_HEADS, n), F32)],
        compiler_params=_cparams(("parallel",)),
        name="dn_gates",
    )(xb, w_col, w_row, p_col, p_row)


def _dn_kernel(q_ref, k_ref, v_ref, z_ref, cg_ref, rg_ref, cw_ref, nw_ref, o_ref,
               xpad, s_ref, u_s, w_s, qd_s, qk_s, kdt_s, vn_s, o_s):
    tb, nch, hd = DN_TB, DN_TB // CHUNK, DN_HEAD_DIM
    t = pl.program_id(1)

    @pl.when(t == 0)
    def _():
        xpad[:, 0:SUBLANES, :] = jnp.zeros((3, SUBLANES, DN_WIDTH), F32)
        s_ref[...] = jnp.zeros_like(s_ref)

    @pl.when(t > 0)
    def _():
        xpad[:, 0:SUBLANES, :] = xpad[:, tb:tb + SUBLANES, :]

    xpad[0, SUBLANES:SUBLANES + tb, :] = q_ref[...].astype(F32)
    xpad[1, SUBLANES:SUBLANES + tb, :] = k_ref[...].astype(F32)
    xpad[2, SUBLANES:SUBLANES + tb, :] = v_ref[...].astype(F32)
    vn_s[...] = jnp.zeros_like(vn_s)

    r = lax.broadcasted_iota(I32, (tb, tb), 0)
    c = lax.broadcasted_iota(I32, (tb, tb), 1)
    same = (r // CHUNK) == (c // CHUNK)
    low_incl = same & (r >= c)
    strict = same & (r > c)
    blk16 = (r // 16) == (c // 16)
    eye = (r == c).astype(F32)
    lane_chunk = lax.broadcasted_iota(I32, (hd, tb), 1) // CHUNK

    for h in range(DN_HEADS):
        sl = slice(h * hd, (h + 1) * hd)

        def conv(idx):
            acc = None
            for i in range(CONV_WIDTH):
                col0 = idx * DN_WIDTH + h * hd
                start = SUBLANES - (CONV_WIDTH - 1) + i
                term = xpad[idx, start:start + tb, sl] * cw_ref[i:i + 1, col0:col0 + hd]
                acc = term if acc is None else acc + term
            return _silu(acc)

        qh, kh, vh = conv(0), conv(1), conv(2)
        qh = qh * lax.rsqrt(jnp.sum(qh * qh, axis=-1, keepdims=True) + L2_EPS) * (hd ** -0.5)
        kh = kh * lax.rsqrt(jnp.sum(kh * kh, axis=-1, keepdims=True) + L2_EPS)
        beta = cg_ref[:, h:h + 1]
        gc = cg_ref[:, DN_HEADS + h:DN_HEADS + h + 1]
        gcr = rg_ref[DN_HEADS + h:DN_HEADS + h + 1, :]
        dec = jnp.exp(jnp.where(low_incl, gc - gcr, NEG_INF))
        kb = kh * beta
        khb = kh.astype(BF16)
        lmat = jnp.where(strict, _bdot_nt(kb, khb) * dec, 0.0)
        qk = _bdot_nt(qh, khb) * dec
        dmat = jnp.where(blk16, lmat, 0.0)
        omat = lmat - dmat
        s1 = eye - dmat
        d2 = _bdot(dmat, dmat)
        s2 = s1 + _bdot(s1, d2)
        d4 = _bdot(d2, d2)
        s3 = s2 + _bdot(s2, d4)
        d8 = _bdot(d4, d4)
        dinv = s3 + _bdot(s3, d8)
        m1 = _bdot(dinv, omat)
        m2 = _bdot(m1, m1)
        im = eye - m1
        tinv = _bdot(im + _bdot(im, m2), dinv)

        eg = jnp.exp(gc)
        uw = _bdot(tinv, jnp.concatenate([vh * beta, kb * eg], axis=1))
        u_s[h] = uw[:, :hd]
        w_s[h] = uw[:, hd:].astype(BF16)
        qd_s[h] = (qh * eg).astype(BF16)
        qk_s[h] = qk.astype(BF16)
        gl = jnp.concatenate(
            [jnp.broadcast_to(gc[cc * CHUNK + CHUNK - 1:cc * CHUNK + CHUNK, :], (CHUNK, 1))
             for cc in range(nch)], axis=0)
        kdt = (kh * jnp.exp(gl - gc)).T
        for cc in range(nch):
            kdt_s[h, cc] = jnp.where(lane_chunk == cc, kdt, 0.0).astype(BF16)

    def chunk_step(cc, carry):
        r0 = pl.multiple_of(cc * CHUNK, CHUNK)
        rows = pl.ds(r0, CHUNK)
        for h in range(DN_HEADS):
            s_old = s_ref[h]
            s_b = s_old.astype(BF16)
            v_new = u_s[h, rows, :] - jnp.dot(w_s[h, rows, :], s_b, preferred_element_type=F32)
            vn_s[h, rows, :] = v_new.astype(BF16)
            vn_all = vn_s[h]
            o = (jnp.dot(qd_s[h, rows, :], s_b, preferred_element_type=F32)
                 + jnp.dot(qk_s[h, rows, :], vn_all, preferred_element_type=F32))
            g_last = cg_ref[pl.ds(r0 + CHUNK - 1, 1), DN_HEADS + h:DN_HEADS + h + 1]
            s_ref[h] = s_old * jnp.exp(g_last) + jnp.dot(kdt_s[h, cc], vn_all,
                                                         preferred_element_type=F32)
            o_s[rows, h * hd:(h + 1) * hd] = o
        return carry

    lax.fori_loop(0, nch, chunk_step, 0)

    for h in range(DN_HEADS):
        sl = slice(h * hd, (h + 1) * hd)
        o = o_s[:, sl]
        o = o * lax.rsqrt(jnp.mean(o * o, axis=-1, keepdims=True) + RMS_EPS)
        z = z_ref[:, sl].astype(F32)
        o_ref[:, sl] = (o * nw_ref[...] * _silu(z)).astype(o_ref.dtype)


def _deltanet(proj, cg, rg, conv_w, norm_w, batch, seq):
    n = batch * seq
    nt = seq // DN_TB
    tb, hd, nch = DN_TB, DN_HEAD_DIM, DN_TB // CHUNK

    def col(j):
        return pl.BlockSpec((tb, DN_WIDTH), lambda b, t, j=j: (b * nt + t, j))

    return pl.pallas_call(
        _dn_kernel,
        grid=(batch, nt),
        in_specs=[col(COL_Q), col(COL_K), col(COL_V), col(COL_Z),
                  pl.BlockSpec((tb, LANES), lambda b, t: (b * nt + t, 0)),
                  pl.BlockSpec((2 * DN_HEADS, tb), lambda b, t: (0, b * nt + t)),
                  pl.BlockSpec((CONV_WIDTH, 3 * DN_WIDTH), lambda b, t: (0, 0)),
                  pl.BlockSpec((1, hd), lambda b, t: (0, 0))],
        out_specs=pl.BlockSpec((tb, DN_WIDTH), lambda b, t: (b * nt + t, 0)),
        out_shape=jax.ShapeDtypeStruct((n, DN_WIDTH), BF16),
        scratch_shapes=[
            pltpu.VMEM((3, tb + SUBLANES, DN_WIDTH), F32),
            pltpu.VMEM((DN_HEADS, hd, hd), F32),
            pltpu.VMEM((DN_HEADS, tb, hd), F32),
            pltpu.VMEM((DN_HEADS, tb, hd), BF16),
            pltpu.VMEM((DN_HEADS, tb, hd), BF16),
            pltpu.VMEM((DN_HEADS, tb, tb), BF16),
            pltpu.VMEM((DN_HEADS, nch, hd, tb), BF16),
            pltpu.VMEM((DN_HEADS, tb, hd), BF16),
            pltpu.VMEM((tb, DN_WIDTH), F32),
        ],
        compiler_params=_cparams(("parallel", "arbitrary")),
        name="deltanet",
    )(proj, proj, proj, proj, cg, rg, conv_w, norm_w)


def _attn_kernel(q_ref, k0_ref, k1_ref, k2_ref, v0_ref, v1_ref, v2_ref, bias_ref, o_ref):
    t = pl.program_id(1)
    q = q_ref[...]
    k = jnp.concatenate([k0_ref[...], k1_ref[...], k2_ref[...]], axis=0)
    v = jnp.concatenate([v0_ref[...], v1_ref[...], v2_ref[...]], axis=0)
    kidx = lax.broadcasted_iota(I32, (1, AT_KB), 1)
    key_valid = kidx >= (2 - t) * AT_QB
    lane_head = lax.broadcasted_iota(I32, (1, LANES), 1) // AT_HEAD_DIM
    scale = AT_HEAD_DIM ** -0.5
    for p in range(AT_HEADS // 2):
        sl = slice(p * LANES, (p + 1) * LANES)
        qp, kp, vp = q[:, sl], k[:, sl], v[:, sl]
        out = jnp.zeros((AT_QB, LANES), F32)
        for hh in range(2):
            mine = lane_head == hh
            qm = jnp.where(mine, qp, jnp.zeros_like(qp))
            s = lax.dot_general(qm, kp, (((1,), (1,)), ((), ())), preferred_element_type=F32)
            s = s * scale + bias_ref[2 * p + hh]
            s = jnp.where(key_valid, s, NEG_INF)
            e = jnp.exp(s - jnp.max(s, axis=-1, keepdims=True))
            pv = jnp.dot(e.astype(BF16), vp, preferred_element_type=F32)
            pv = pv / jnp.sum(e, axis=-1, keepdims=True)
            out = jnp.where(mine, pv, out)
        o_ref[:, sl] = out.astype(o_ref.dtype)


def _attention(proj, bias, batch, seq):
    n = batch * seq
    nt = seq // AT_QB

    def kv(col, back):
        return pl.BlockSpec((AT_QB, AT_WIDTH),
                            lambda b, t, col=col, back=back: (b * nt + jnp.maximum(t - back, 0), col))

    return pl.pallas_call(
        _attn_kernel,
        grid=(batch, nt),
        in_specs=[pl.BlockSpec((AT_QB, AT_WIDTH), lambda b, t: (b * nt + t, COL_AQ)),
                  kv(COL_AK, 2), kv(COL_AK, 1), kv(COL_AK, 0),
                  kv(COL_AV, 2), kv(COL_AV, 1), kv(COL_AV, 0),
                  pl.BlockSpec((AT_HEADS, AT_QB, AT_KB), lambda b, t: (0, 0, 0))],
        out_specs=pl.BlockSpec((AT_QB, AT_WIDTH), lambda b, t: (b * nt + t, 0)),
        out_shape=jax.ShapeDtypeStruct((n, AT_WIDTH), BF16),
        compiler_params=_cparams(("parallel", "parallel")),
        name="band_attention",
    )(proj, proj, proj, proj, proj, proj, proj, bias)


def _attention_bias(rel_table):
    qi = jnp.arange(AT_QB)[:, None]
    ki = jnp.arange(AT_KB)[None, :]
    rel = jnp.clip(ki - (AT_KB - AT_QB) - qi, -REL_CLIP, REL_CLIP) + REL_CLIP
    bias = rel_table.astype(F32)[:, rel]
    qc, kc = qi // CHUNK, ki // CHUNK
    band = (kc >= qc) & (kc <= qc + LEFT_CHUNKS)
    return jnp.where(band[None], bias, NEG_INF)


def _layer_norm(y, g, b):
    mu = jnp.mean(y, axis=-1, keepdims=True)
    d = y - mu
    var = jnp.mean(d * d, axis=-1, keepdims=True)
    return d * lax.rsqrt(var + LN_EPS) * g + b


def _merge_kernel(dn_ref, at_ref, ga_ref, gb_ref, x_ref, wa_ref, wb_ref, wo_ref, g_ref, b_ref,
                  x1_ref, x1t_ref, x1b_ref):
    ya = jnp.dot(dn_ref[...], wa_ref[...], preferred_element_type=F32)
    yb = jnp.dot(at_ref[...], wb_ref[...], preferred_element_type=F32)
    mix = _sigmoid(ga_ref[...].astype(F32)) * ya + _sigmoid(gb_ref[...].astype(F32)) * yb
    y = ALPHA * x_ref[...] + jnp.dot(mix.astype(BF16), wo_ref[...], preferred_element_type=F32)
    x1 = _layer_norm(y, g_ref[...], b_ref[...])
    x1_ref[...] = x1
    x1b_ref[...] = x1.astype(BF16)
    for s in range(ROW_TILES):
        x1t_ref[:, s, :] = x1[:, s * LANES:(s + 1) * LANES]


def _merge(dn, at, proj, x, wa, wb, wo, g, b, tm=512):
    n = x.shape[0]
    return pl.pallas_call(
        _merge_kernel,
        grid=(n // tm,),
        in_specs=[pl.BlockSpec((tm, DN_WIDTH), lambda i: (i, 0)),
                  pl.BlockSpec((tm, AT_WIDTH), lambda i: (i, 0)),
                  pl.BlockSpec((tm, D_MODEL), lambda i: (i, COL_GA)),
                  pl.BlockSpec((tm, D_MODEL), lambda i: (i, COL_GB)),
                  pl.BlockSpec((tm, D_MODEL), lambda i: (i, 0)),
                  pl.BlockSpec((DN_WIDTH, D_MODEL), lambda i: (0, 0)),
                  pl.BlockSpec((AT_WIDTH, D_MODEL), lambda i: (0, 0)),
                  pl.BlockSpec((D_MODEL, D_MODEL), lambda i: (0, 0)),
                  pl.BlockSpec((1, D_MODEL), lambda i: (0, 0)),
                  pl.BlockSpec((1, D_MODEL), lambda i: (0, 0))],
        out_specs=[pl.BlockSpec((tm, D_MODEL), lambda i: (i, 0)),
                   pl.BlockSpec((tm, ROW_TILES, LANES), lambda i: (i, 0, 0)),
                   pl.BlockSpec((tm, D_MODEL), lambda i: (i, 0))],
        out_shape=[jax.ShapeDtypeStruct((n, D_MODEL), F32),
                   jax.ShapeDtypeStruct((n, ROW_TILES, LANES), F32),
                   jax.ShapeDtypeStruct((n, D_MODEL), BF16)],
        compiler_params=_cparams(("parallel",)),
        name="merge_ln1",
    )(dn, at, proj, proj, x, wa, wb, wo, g, b)


def _router_kernel(x_ref, rw_ref, rb_ref, dest_ref, wt_ref, seg_ref):
    gt, ne = GROUP, N_EXPERTS
    logits = lax.dot_general(rw_ref[...], x_ref[...], (((1,), (1,)), ((), ())),
                             preferred_element_type=F32, precision=lax.Precision.HIGHEST)
    scores = _sigmoid(logits)
    sel = scores + rb_ref[:, 0:1]
    eidx = lax.broadcasted_iota(I32, (ne, gt), 0)
    picked = jnp.zeros((ne, gt), jnp.bool_)
    hits, vals = [], []
    for _ in range(TOP_K):
        m = jnp.max(sel, axis=0, keepdims=True)
        idx = jnp.min(jnp.where(sel == m, eidx, ne), axis=0, keepdims=True)
        hit = eidx == idx
        hits.append(hit)
        vals.append(jnp.sum(jnp.where(hit, scores, 0.0), axis=0, keepdims=True))
        sel = jnp.where(hit, -jnp.inf, sel)
        picked = picked | hit
    total = vals[0]
    for v in vals[1:]:
        total = total + v

    onehot = picked.astype(BF16)
    tr = lax.broadcasted_iota(I32, (gt, gt), 0)
    tc = lax.broadcasted_iota(I32, (gt, gt), 1)
    rank = jnp.dot(onehot, (tr < tc).astype(BF16), preferred_element_type=F32)
    count = jnp.sum(picked.astype(F32), axis=1, keepdims=True)
    tiles = jnp.floor((count + (SUBLANES - 1)) * (1.0 / SUBLANES))
    er = lax.broadcasted_iota(I32, (ne, ne), 0)
    ec = lax.broadcasted_iota(I32, (ne, ne), 1)
    start = SUBLANES * jnp.dot((ec < er).astype(BF16),
                               jnp.broadcast_to(tiles, (ne, LANES)).astype(BF16),
                               preferred_element_type=F32)
    slot = start[:, 0:1] + rank

    sub = lax.broadcasted_iota(I32, (SUBLANES, gt), 0)
    dest = jnp.zeros((SUBLANES, gt), F32)
    wts = jnp.zeros((SUBLANES, gt), F32)
    for kk in range(TOP_K):
        d = jnp.sum(jnp.where(hits[kk], slot, 0.0), axis=0, keepdims=True)
        dest = jnp.where(sub == kk, d, dest)
        wts = jnp.where(sub == kk, vals[kk] / total * ROUTED_SCALE, wts)
    dest_ref[0] = dest.astype(I32)
    wt_ref[0] = wts
    lane = lax.broadcasted_iota(I32, (ne, LANES), 1)
    seg_ref[0] = jnp.where(lane == 0, start, jnp.broadcast_to(count, (ne, LANES))).astype(I32)


def _router(x1, rw_t, rb):
    n = x1.shape[0]
    ng = n // GROUP
    return pl.pallas_call(
        _router_kernel,
        grid=(ng,),
        in_specs=[pl.BlockSpec((GROUP, D_MODEL), lambda g: (g, 0)),
                  pl.BlockSpec((N_EXPERTS, D_MODEL), lambda g: (0, 0)),
                  pl.BlockSpec((N_EXPERTS, LANES), lambda g: (0, 0))],
        out_specs=[pl.BlockSpec((1, SUBLANES, GROUP), lambda g: (g, 0, 0)),
                   pl.BlockSpec((1, SUBLANES, GROUP), lambda g: (g, 0, 0)),
                   pl.BlockSpec((1, N_EXPERTS, LANES), lambda g: (g, 0, 0))],
        out_shape=[jax.ShapeDtypeStruct((ng, SUBLANES, GROUP), I32),
                   jax.ShapeDtypeStruct((ng, SUBLANES, GROUP), F32),
                   jax.ShapeDtypeStruct((ng, N_EXPERTS, LANES), I32)],
        compiler_params=_cparams(("parallel",)),
        name="router",
    )(x1, rw_t, rb)


def _moe_kernel(start_ref, count_ref, xt_ref, dest_hbm, wt_hbm, wgu_ref, wd_ref, out_ref,
                buf, dest_s, wt_s, sem):
    g = pl.program_id(0)
    e = pl.program_id(1)
    ne, gt = N_EXPERTS, GROUP

    def routing_copies():
        return (pltpu.make_async_copy(dest_hbm.at[g], dest_s, sem.at[0]),
                pltpu.make_async_copy(wt_hbm.at[g], wt_s, sem.at[1]))

    @pl.when(e == 0)
    def _():
        for cp in routing_copies():
            cp.start()
        zeros = jnp.zeros((SUBLANES, ROW_TILES, LANES), F32)

        def zero_pad(ee, carry):
            s = start_ref[g * ne + ee]
            cnt = count_ref[g * ne + ee]
            z0 = pl.multiple_of(s + (cnt // SUBLANES) * SUBLANES, SUBLANES)
            buf[pl.ds(z0, SUBLANES)] = zeros
            return carry

        lax.fori_loop(0, ne, zero_pad, 0)
        last = g * ne + ne - 1
        end = start_ref[last] + (count_ref[last] + SUBLANES - 1) // SUBLANES * SUBLANES
        end = pl.multiple_of(end, SUBLANES)
        buf[pl.ds(end, FFN_BM)] = jnp.zeros((FFN_BM, ROW_TILES, LANES), F32)
        for cp in routing_copies():
            cp.wait()

        def scatter(tok, carry):
            row = xt_ref[tok]
            for kk in range(TOP_K):
                buf[dest_s[kk * gt + tok]] = row
            return carry

        lax.fori_loop(0, gt, scatter, 0)

    s = start_ref[g * ne + e]
    cnt = count_ref[g * ne + e]

    def ffn_block(i, carry):
        r0 = pl.multiple_of(s + i * FFN_BM, SUBLANES)
        rows = pl.ds(r0, FFN_BM)
        xin = jnp.concatenate([buf[rows, j, :] for j in range(ROW_TILES)], axis=1)
        gu = jnp.dot(xin.astype(BF16), wgu_ref[0], preferred_element_type=F32)
        hidden = _silu(gu[:, :EXPERT_DIM]) * gu[:, EXPERT_DIM:]
        y = jnp.dot(hidden.astype(BF16), wd_ref[0], preferred_element_type=F32)
        keep = (r0 + lax.broadcasted_iota(I32, (FFN_BM, 1), 0)) < s + cnt
        for j in range(ROW_TILES):
            buf[rows, j, :] = jnp.where(keep, y[:, j * LANES:(j + 1) * LANES],
                                        xin[:, j * LANES:(j + 1) * LANES])
        return carry

    lax.fori_loop(0, (cnt + FFN_BM - 1) // FFN_BM, ffn_block, 0)

    @pl.when(e == ne - 1)
    def _():
        def combine(tok, carry):
            acc = buf[dest_s[tok]] * wt_s[tok]
            for kk in range(1, TOP_K):
                acc = acc + buf[dest_s[kk * gt + tok]] * wt_s[kk * gt + tok]
            out_ref[tok] = acc
            return carry

        lax.fori_loop(0, gt, combine, 0)


def _moe(x1t, dest, wts, starts, counts, wgu, wd):
    n = x1t.shape[0]
    ng = n // GROUP
    grid_spec = pltpu.PrefetchScalarGridSpec(
        num_scalar_prefetch=2,
        grid=(ng, N_EXPERTS),
        in_specs=[pl.BlockSpec((GROUP, ROW_TILES, LANES), lambda g, e, *_: (g, 0, 0)),
                  pl.BlockSpec(memory_space=pl.ANY),
                  pl.BlockSpec(memory_space=pl.ANY),
                  pl.BlockSpec((1, D_MODEL, 2 * EXPERT_DIM), lambda g, e, *_: (e, 0, 0)),
                  pl.BlockSpec((1, EXPERT_DIM, D_MODEL), lambda g, e, *_: (e, 0, 0))],
        out_specs=pl.BlockSpec((GROUP, ROW_TILES, LANES), lambda g, e, *_: (g, 0, 0)),
        scratch_shapes=[pltpu.VMEM((GROUP_ROWS, ROW_TILES, LANES), F32),
                        pltpu.SMEM((SUBLANES * GROUP,), I32),
                        pltpu.SMEM((SUBLANES * GROUP,), F32),
                        pltpu.SemaphoreType.DMA((2,))],
    )
    return pl.pallas_call(
        _moe_kernel,
        grid_spec=grid_spec,
        out_shape=jax.ShapeDtypeStruct((n, ROW_TILES, LANES), F32),
        compiler_params=_cparams(("arbitrary", "arbitrary")),
        name="routed_experts",
    )(starts, counts, x1t, dest, wts, wgu, wd)


def _final_kernel(x1_ref, x1b_ref, rt_ref, wsg_ref, wsd_ref, g_ref, b_ref, o_ref, ob_ref):
    gus = jnp.dot(x1b_ref[...], wsg_ref[...], preferred_element_type=F32)
    hidden = _silu(gus[:, :SHARED_DIM]) * gus[:, SHARED_DIM:]
    shared = jnp.dot(hidden.astype(BF16), wsd_ref[...], preferred_element_type=F32)
    routed = jnp.concatenate([rt_ref[:, j, :] for j in range(ROW_TILES)], axis=1)
    y = ALPHA * x1_ref[...] + (routed + shared)
    x2 = _layer_norm(y, g_ref[...], b_ref[...])
    o_ref[...] = x2
    ob_ref[...] = x2.astype(BF16)


def _final(x1, x1b, routed_t, wsg, wsd, g, b, tm=512):
    n = x1.shape[0]
    return pl.pallas_call(
        _final_kernel,
        grid=(n // tm,),
        in_specs=[pl.BlockSpec((tm, D_MODEL), lambda i: (i, 0)),
                  pl.BlockSpec((tm, D_MODEL), lambda i: (i, 0)),
                  pl.BlockSpec((tm, ROW_TILES, LANES), lambda i: (i, 0, 0)),
                  pl.BlockSpec((D_MODEL, 2 * SHARED_DIM), lambda i: (0, 0)),
                  pl.BlockSpec((SHARED_DIM, D_MODEL), lambda i: (0, 0)),
                  pl.BlockSpec((1, D_MODEL), lambda i: (0, 0)),
                  pl.BlockSpec((1, D_MODEL), lambda i: (0, 0))],
        out_specs=[pl.BlockSpec((tm, D_MODEL), lambda i: (i, 0)),
                   pl.BlockSpec((tm, D_MODEL), lambda i: (i, 0))],
        out_shape=[jax.ShapeDtypeStruct((n, D_MODEL), F32),
                   jax.ShapeDtypeStruct((n, D_MODEL), BF16)],
        compiler_params=_cparams(("parallel",)),
        name="shared_ln2",
    )(x1, x1b, routed_t, wsg, wsd, g, b)


def _layer(x, xb, batch, seq, w_in, conv_w, a_log, dt_bias, dn_norm_w, rel_table, w_branch_a,
           w_branch_b, w_out, ln1_g, ln1_b, router_w, router_bias, w_gate_up, w_down,
           ws_gate_up, ws_down, ln2_g, ln2_b):
    c_small = 4 * DN_WIDTH
    c_att = c_small + 2 * DN_HEADS
    c_gate = c_att + 3 * AT_WIDTH
    w_big = jnp.concatenate([w_in[:, :c_small], w_in[:, c_gate:], w_in[:, c_att:c_gate]],
                            axis=1).astype(BF16)
    w_small = w_in[:, c_small:c_att]
    w_col = jnp.pad(w_small, ((0, 0), (0, LANES - 2 * DN_HEADS))).astype(BF16)
    w_row = w_small.T.astype(BF16)
    decay_rate = jnp.exp(a_log.astype(F32))
    zeros8 = jnp.zeros((DN_HEADS,), F32)
    p_col = jnp.pad(jnp.stack([jnp.concatenate([zeros8, decay_rate]),
                               jnp.concatenate([zeros8, dt_bias.astype(F32)])]),
                    ((0, 0), (0, LANES - 2 * DN_HEADS)))
    p_row = jnp.pad(p_col[:, :2 * DN_HEADS].T, ((0, 0), (0, LANES - 2)))

    proj = _matmul(xb, w_big, BF16, tm=1024, tn=1280)
    cg, rg = _gates(xb, w_col, w_row, p_col, p_row)
    dn = _deltanet(proj, cg, rg, conv_w.astype(F32), dn_norm_w.astype(F32).reshape(1, DN_HEAD_DIM),
                   batch, seq)
    at = _attention(proj, _attention_bias(rel_table), batch, seq)
    x1, x1t, x1b = _merge(dn, at, proj, x, w_branch_a.astype(BF16), w_branch_b.astype(BF16),
                          w_out.astype(BF16), ln1_g.reshape(1, -1), ln1_b.reshape(1, -1))
    dest, wts, seg = _router(x1, router_w.T.astype(F32),
                             jnp.broadcast_to(router_bias.astype(F32)[:, None], (N_EXPERTS, LANES)))
    ng = dest.shape[0]
    routed_t = _moe(x1t, dest.reshape(ng, SUBLANES * GROUP), wts.reshape(ng, SUBLANES * GROUP),
                    seg[:, :, 0].reshape(-1), seg[:, :, 1].reshape(-1),
                    w_gate_up.astype(BF16), w_down.astype(BF16))
    return _final(x1, x1b, routed_t, ws_gate_up.astype(BF16), ws_down.astype(BF16),
                  ln2_g.reshape(1, -1), ln2_b.reshape(1, -1))


def kernel(x, w_in, conv_w, a_log, dt_bias, dn_norm_w, rel_table, w_branch_a, w_branch_b, w_out,
           ln1_g, ln1_b, router_w, router_bias, expert_w_gate_up, expert_w_down,
           shared_w_gate_up, shared_w_down, ln2_g, ln2_b):
    batch, seq, d = x.shape
    assert d == D_MODEL and seq % DN_TB == 0 and seq % AT_QB == 0 and (batch * seq) % GROUP == 0
    xf = x.reshape(batch * seq, d).astype(F32)
    xb = xf.astype(BF16)
    for l in range(DEPTH):
        xf, xb = _layer(xf, xb, batch, seq, w_in[l], conv_w[l], a_log[l], dt_bias[l], dn_norm_w[l],
                        rel_table[l], w_branch_a[l], w_branch_b[l], w_out[l], ln1_g[l], ln1_b[l],
                        router_w[l], router_bias[l], expert_w_gate_up[l], expert_w_down[l],
                        shared_w_gate_up[l], shared_w_down[l], ln2_g[l], ln2_b[l])
    return xf.reshape(batch, seq, d).astype(x.dtype)
```

```python
import functools

import jax
import jax.numpy as jnp
from jax import lax
from jax.experimental import pallas as pl
from jax.experimental.pallas import tpu as pltpu

F32 = jnp.float32
BF16 = jnp.bfloat16
I32 = jnp.int32

D_MODEL = 1024
DEPTH = 2
CHUNK = 64
DN_HEADS = 8
DN_HEAD_DIM = 128
DN_WIDTH = DN_HEADS * DN_HEAD_DIM
CONV_WIDTH = 4
AT_HEADS = 8
AT_HEAD_DIM = 64
AT_WIDTH = AT_HEADS * AT_HEAD_DIM
LEFT_CHUNKS = 8
REL_CLIP = 256
N_EXPERTS = 64
TOP_K = 6
EXPERT_DIM = 256
SHARED_DIM = 256
ROUTED_SCALE = 2.5
ALPHA = (2 * DEPTH) ** 0.25
LN_EPS = 1e-5
RMS_EPS = 1e-6
L2_EPS = 1e-6
NEG_INF = -1e30

LANES = 128
SUBLANES = 8
ROW_TILES = D_MODEL // LANES

DN_TB = 256
DN_SB = 128
DN_HEADS_PER_PASS = 2
AT_QB = 256
AT_KB = 3 * AT_QB
GROUP = 1024
FFN_BM = 128
GROUP_ROWS = GROUP * TOP_K + N_EXPERTS * (SUBLANES - 1) + FFN_BM
GROUP_ROWS = -(-GROUP_ROWS // SUBLANES) * SUBLANES
VMEM_LIMIT = 56 * 1024 * 1024

COL_Q, COL_K, COL_V, COL_Z, COL_GA, COL_GB = 0, 1, 2, 3, 4, 5
COL_AQ, COL_AK, COL_AV = 12, 13, 14
BIG_COLS = 4 * DN_WIDTH + 2 * D_MODEL + 3 * AT_WIDTH


def _sigmoid(x):
    return 1.0 / (1.0 + jnp.exp(-x))


def _silu(x):
    return x * _sigmoid(x)


def _softplus(x):
    return jnp.maximum(x, 0.0) + jnp.log1p(jnp.exp(-jnp.abs(x)))


def _bdot(a, b):
    return jnp.dot(a.astype(BF16), b.astype(BF16), preferred_element_type=F32)


def _bdot_nt(a, b):
    return lax.dot_general(a.astype(BF16), b.astype(BF16), (((1,), (1,)), ((), ())),
                           preferred_element_type=F32)


def _cparams(sem):
    return pltpu.CompilerParams(dimension_semantics=sem, vmem_limit_bytes=VMEM_LIMIT)


def _mm_kernel(x_ref, w_ref, o_ref):
    o_ref[...] = jnp.dot(x_ref[...], w_ref[...], preferred_element_type=F32).astype(o_ref.dtype)


def _matmul(x, w, out_dtype, tm, tn):
    n, k = x.shape
    m = w.shape[1]
    return pl.pallas_call(
        _mm_kernel,
        grid=(n // tm, m // tn),
        in_specs=[pl.BlockSpec((tm, k), lambda i, j: (i, 0)),
                  pl.BlockSpec((k, tn), lambda i, j: (0, j))],
        out_specs=pl.BlockSpec((tm, tn), lambda i, j: (i, j)),
        out_shape=jax.ShapeDtypeStruct((n, m), out_dtype),
        compiler_params=_cparams(("parallel", "parallel")),
        name="inproj",
    )(x, w)


def _gates_kernel(x_ref, wc_ref, wr_ref, pc_ref, pr_ref, cg_ref, rg_ref, *, tm):
    x = x_ref[...]
    r = lax.broadcasted_iota(I32, (tm, tm), 0)
    c = lax.broadcasted_iota(I32, (tm, tm), 1)
    same = (r // CHUNK) == (c // CHUNK)

    pc = jnp.dot(x, wc_ref[...], preferred_element_type=F32)
    lane = lax.broadcasted_iota(I32, (tm, LANES), 1)
    col = jnp.where(lane < DN_HEADS, _sigmoid(pc),
                    -pc_ref[0:1, :] * _softplus(pc + pc_ref[1:2, :]))
    lower = (same & (r >= c)).astype(F32)
    gcc = jnp.dot(lower, col, preferred_element_type=F32, precision=lax.Precision.HIGHEST)
    cg_ref[...] = jnp.where(lane < DN_HEADS, col, gcc)

    pr = lax.dot_general(wr_ref[...], x, (((1,), (1,)), ((), ())),
                         preferred_element_type=F32)
    sub = lax.broadcasted_iota(I32, (2 * DN_HEADS, tm), 0)
    row = jnp.where(sub < DN_HEADS, _sigmoid(pr),
                    -pr_ref[:, 0:1] * _softplus(pr + pr_ref[:, 1:2]))
    upper = (same & (r <= c)).astype(F32)
    gcr = jnp.dot(row, upper, preferred_element_type=F32, precision=lax.Precision.HIGHEST)
    rg_ref[...] = jnp.where(sub < DN_HEADS, row, gcr)


def _gates(xb, w_col, w_row, p_col, p_row, tm=256):
    n = xb.shape[0]
    return pl.pallas_call(
        functools.partial(_gates_kernel, tm=tm),
        grid=(n // tm,),
        in_specs=[pl.BlockSpec((tm, D_MODEL), lambda i: (i, 0)),
                  pl.BlockSpec((D_MODEL, LANES), lambda i: (0, 0)),
                  pl.BlockSpec((2 * DN_HEADS, D_MODEL), lambda i: (0, 0)),
                  pl.BlockSpec((2, LANES), lambda i: (0, 0)),
                  pl.BlockSpec((2 * DN_HEADS, LANES), lambda i: (0, 0))],
        out_specs=[pl.BlockSpec((tm, LANES), lambda i: (i, 0)),
                   pl.BlockSpec((2 * DN_HEADS, tm), lambda i: (0, i))],
        out_shape=[jax.ShapeDtypeStruct((n, LANES), F32),
                   jax.ShapeDtypeStruct((2 * DN_HEADS, n), F32)],
        compiler_params=_cparams(("parallel",)),
        name="dn_gates",
    )(xb, w_col, w_row, p_col, p_row)


def _dn_kernel(q_ref, k_ref, v_ref, z_ref, cg_ref, rg_ref, cw_ref, nw_ref, o_ref,
               xpad, s_ref, u_s, w_s, qd_s, qk_s, kdt_s, vn_s, o_s):
    tb, nch, hd = DN_TB, DN_TB // CHUNK, DN_HEAD_DIM
    t = pl.program_id(1)

    @pl.when(t == 0)
    def _():
        xpad[:, 0:SUBLANES, :] = jnp.zeros((3, SUBLANES, DN_WIDTH), F32)
        s_ref[...] = jnp.zeros_like(s_ref)

    @pl.when(t > 0)
    def _():
        xpad[:, 0:SUBLANES, :] = xpad[:, tb:tb + SUBLANES, :]

    xpad[0, SUBLANES:SUBLANES + tb, :] = q_ref[...].astype(F32)
    xpad[1, SUBLANES:SUBLANES + tb, :] = k_ref[...].astype(F32)
    xpad[2, SUBLANES:SUBLANES + tb, :] = v_ref[...].astype(F32)
    vn_s[...] = jnp.zeros_like(vn_s)

    sb = DN_SB
    halves = tb // sb
    r = lax.broadcasted_iota(I32, (sb, sb), 0)
    c = lax.broadcasted_iota(I32, (sb, sb), 1)
    same = (r // CHUNK) == (c // CHUNK)
    low_incl = same & (r >= c)
    strict = same & (r > c)
    blk16 = (r // 16) == (c // 16)
    eye = (r == c).astype(F32)
    lane_chunk = lax.broadcasted_iota(I32, (hd, sb), 1) // CHUNK

    def each(fn, *lists):
        return [fn(*args) for args in zip(*lists)]

    for h0 in range(0, DN_HEADS, DN_HEADS_PER_PASS):
        inst = [(h, half) for h in range(h0, h0 + DN_HEADS_PER_PASS) for half in range(halves)]

        def conv(idx, h, half):
            acc = None
            for i in range(CONV_WIDTH):
                col0 = idx * DN_WIDTH + h * hd
                start = SUBLANES - (CONV_WIDTH - 1) + i + half * sb
                term = (xpad[idx, start:start + sb, h * hd:(h + 1) * hd]
                        * cw_ref[i:i + 1, col0:col0 + hd])
                acc = term if acc is None else acc + term
            return _silu(acc)

        qh = [conv(0, h, half) for h, half in inst]
        kh = [conv(1, h, half) for h, half in inst]
        vh = [conv(2, h, half) for h, half in inst]
        qh = each(lambda x: x * lax.rsqrt(jnp.sum(x * x, axis=-1, keepdims=True) + L2_EPS)
                  * (hd ** -0.5), qh)
        kh = each(lambda x: x * lax.rsqrt(jnp.sum(x * x, axis=-1, keepdims=True) + L2_EPS), kh)
        beta = [cg_ref[half * sb:(half + 1) * sb, h:h + 1] for h, half in inst]
        gc = [cg_ref[half * sb:(half + 1) * sb, DN_HEADS + h:DN_HEADS + h + 1] for h, half in inst]
        gcr = [rg_ref[DN_HEADS + h:DN_HEADS + h + 1, half * sb:(half + 1) * sb] for h, half in inst]
        dec = each(lambda a, b: jnp.exp(jnp.where(low_incl, a - b, NEG_INF)), gc, gcr)
        kb = each(lambda a, b: a * b, kh, beta)
        khb = each(lambda a: a.astype(BF16), kh)
        lmat = each(lambda a, b, d: jnp.where(strict, _bdot_nt(a, b) * d, 0.0), kb, khb, dec)
        qk = each(lambda a, b, d: _bdot_nt(a, b) * d, qh, khb, dec)
        dmat = each(lambda a: jnp.where(blk16, a, 0.0), lmat)
        omat = each(lambda a, b: a - b, lmat, dmat)
        s1 = each(lambda a: eye - a, dmat)
        d2 = each(lambda a: _bdot(a, a), dmat)
        s2 = each(lambda a, b: a + _bdot(a, b), s1, d2)
        d4 = each(lambda a: _bdot(a, a), d2)
        s3 = each(lambda a, b: a + _bdot(a, b), s2, d4)
        d8 = each(lambda a: _bdot(a, a), d4)
        dinv = each(lambda a, b: a + _bdot(a, b), s3, d8)
        m1 = each(_bdot, dinv, omat)
        m2 = each(lambda a: _bdot(a, a), m1)
        im = each(lambda a: eye - a, m1)
        pm = each(lambda a, b: a + _bdot(a, b), im, m2)
        tinv = each(_bdot, pm, dinv)
        eg = each(jnp.exp, gc)
        rhs = each(lambda v, b, k, e: jnp.concatenate([v * b, k * e], axis=1), vh, beta, kb, eg)
        uw = each(_bdot, tinv, rhs)
        for i, (h, half) in enumerate(inst):
            rs = slice(half * sb, (half + 1) * sb)
            u_s[h, rs, :] = uw[i][:, :hd]
            w_s[h, rs, :] = uw[i][:, hd:].astype(BF16)
            qd_s[h, rs, :] = (qh[i] * eg[i]).astype(BF16)
            qk_s[h, rs, :] = qk[i].astype(BF16)
            gl = jnp.concatenate(
                [jnp.broadcast_to(gc[i][cc * CHUNK + CHUNK - 1:cc * CHUNK + CHUNK, :], (CHUNK, 1))
                 for cc in range(sb // CHUNK)], axis=0)
            kdt = (kh[i] * jnp.exp(gl - gc[i])).T
            for cc in range(sb // CHUNK):
                kdt_s[h, half * (sb // CHUNK) + cc] = jnp.where(lane_chunk == cc, kdt,
                                                                0.0).astype(BF16)

    heads = list(range(DN_HEADS))

    def chunk_step(cc, carry):
        r0 = pl.multiple_of(cc * CHUNK, CHUNK)
        rows = pl.ds(r0, CHUNK)
        sub_rows = pl.ds(pl.multiple_of((cc // (sb // CHUNK)) * sb, sb), sb)
        s_old = [s_ref[h] for h in heads]
        s_b = each(lambda a: a.astype(BF16), s_old)
        v_new = [u_s[h, rows, :] - jnp.dot(w_s[h, rows, :], s_b[h], preferred_element_type=F32)
                 for h in heads]
        o_inter = [jnp.dot(qd_s[h, rows, :], s_b[h], preferred_element_type=F32) for h in heads]
        for h in heads:
            vn_s[h, rows, :] = v_new[h].astype(BF16)
        vn_all = [vn_s[h, sub_rows, :] for h in heads]
        o_intra = [jnp.dot(qk_s[h, rows, :], vn_all[h], preferred_element_type=F32) for h in heads]
        s_add = [jnp.dot(kdt_s[h, cc], vn_all[h], preferred_element_type=F32) for h in heads]
        for h in heads:
            g_last = cg_ref[pl.ds(r0 + CHUNK - 1, 1), DN_HEADS + h:DN_HEADS + h + 1]
            s_ref[h] = s_old[h] * jnp.exp(g_last) + s_add[h]
            o_s[rows, h * hd:(h + 1) * hd] = o_inter[h] + o_intra[h]
        return carry

    lax.fori_loop(0, nch, chunk_step, 0)

    for h in range(DN_HEADS):
        sl = slice(h * hd, (h + 1) * hd)
        o = o_s[:, sl]
        o = o * lax.rsqrt(jnp.mean(o * o, axis=-1, keepdims=True) + RMS_EPS)
        z = z_ref[:, sl].astype(F32)
        o_ref[:, sl] = (o * nw_ref[...] * _silu(z)).astype(o_ref.dtype)


def _deltanet(proj, cg, rg, conv_w, norm_w, batch, seq):
    n = batch * seq
    nt = seq // DN_TB
    tb, hd, nch = DN_TB, DN_HEAD_DIM, DN_TB // CHUNK

    def col(j):
        return pl.BlockSpec((tb, DN_WIDTH), lambda b, t, j=j: (b * nt + t, j))

    return pl.pallas_call(
        _dn_kernel,
        grid=(batch, nt),
        in_specs=[col(COL_Q), col(COL_K), col(COL_V), col(COL_Z),
                  pl.BlockSpec((tb, LANES), lambda b, t: (b * nt + t, 0)),
                  pl.BlockSpec((2 * DN_HEADS, tb), lambda b, t: (0, b * nt + t)),
                  pl.BlockSpec((CONV_WIDTH, 3 * DN_WIDTH), lambda b, t: (0, 0)),
                  pl.BlockSpec((1, hd), lambda b, t: (0, 0))],
        out_specs=pl.BlockSpec((tb, DN_WIDTH), lambda b, t: (b * nt + t, 0)),
        out_shape=jax.ShapeDtypeStruct((n, DN_WIDTH), BF16),
        scratch_shapes=[
            pltpu.VMEM((3, tb + SUBLANES, DN_WIDTH), F32),
            pltpu.VMEM((DN_HEADS, hd, hd), F32),
            pltpu.VMEM((DN_HEADS, tb, hd), F32),
            pltpu.VMEM((DN_HEADS, tb, hd), BF16),
            pltpu.VMEM((DN_HEADS, tb, hd), BF16),
            pltpu.VMEM((DN_HEADS, tb, DN_SB), BF16),
            pltpu.VMEM((DN_HEADS, nch, hd, DN_SB), BF16),
            pltpu.VMEM((DN_HEADS, tb, hd), BF16),
            pltpu.VMEM((tb, DN_WIDTH), F32),
        ],
        compiler_params=_cparams(("parallel", "arbitrary")),
        name="deltanet",
    )(proj, proj, proj, proj, cg, rg, conv_w, norm_w)


def _attn_kernel(q_ref, k0_ref, k1_ref, k2_ref, v0_ref, v1_ref, v2_ref, bias_ref, o_ref):
    t = pl.program_id(1)
    q = q_ref[...]
    k = jnp.concatenate([k0_ref[...], k1_ref[...], k2_ref[...]], axis=0)
    v = jnp.concatenate([v0_ref[...], v1_ref[...], v2_ref[...]], axis=0)
    kidx = lax.broadcasted_iota(I32, (1, AT_KB), 1)
    key_valid = kidx >= (2 - t) * AT_QB
    lane_head = lax.broadcasted_iota(I32, (1, LANES), 1) // AT_HEAD_DIM
    scale = AT_HEAD_DIM ** -0.5
    for p in range(AT_HEADS // 2):
        sl = slice(p * LANES, (p + 1) * LANES)
        qp, kp, vp = q[:, sl], k[:, sl], v[:, sl]
        out = jnp.zeros((AT_QB, LANES), F32)
        for hh in range(2):
            mine = lane_head == hh
            qm = jnp.where(mine, qp, jnp.zeros_like(qp))
            s = lax.dot_general(qm, kp, (((1,), (1,)), ((), ())), preferred_element_type=F32)
            s = s * scale + bias_ref[2 * p + hh]
            s = jnp.where(key_valid, s, NEG_INF)
            e = jnp.exp(s - jnp.max(s, axis=-1, keepdims=True))
            pv = jnp.dot(e.astype(BF16), vp, preferred_element_type=F32)
            pv = pv / jnp.sum(e, axis=-1, keepdims=True)
            out = jnp.where(mine, pv, out)
        o_ref[:, sl] = out.astype(o_ref.dtype)


def _attention(proj, bias, batch, seq):
    n = batch * seq
    nt = seq // AT_QB

    def kv(col, back):
        return pl.BlockSpec((AT_QB, AT_WIDTH),
                            lambda b, t, col=col, back=back: (b * nt + jnp.maximum(t - back, 0), col))

    return pl.pallas_call(
        _attn_kernel,
        grid=(batch, nt),
        in_specs=[pl.BlockSpec((AT_QB, AT_WIDTH), lambda b, t: (b * nt + t, COL_AQ)),
                  kv(COL_AK, 2), kv(COL_AK, 1), kv(COL_AK, 0),
                  kv(COL_AV, 2), kv(COL_AV, 1), kv(COL_AV, 0),
                  pl.BlockSpec((AT_HEADS, AT_QB, AT_KB), lambda b, t: (0, 0, 0))],
        out_specs=pl.BlockSpec((AT_QB, AT_WIDTH), lambda b, t: (b * nt + t, 0)),
        out_shape=jax.ShapeDtypeStruct((n, AT_WIDTH), BF16),
        compiler_params=_cparams(("parallel", "parallel")),
        name="band_attention",
    )(proj, proj, proj, proj, proj, proj, proj, bias)


def _bias_kernel(f_ref, o_ref):
    width = f_ref.shape[1]
    qi = lax.broadcasted_iota(I32, (AT_QB, AT_KB), 0) // CHUNK
    ki = lax.broadcasted_iota(I32, (AT_QB, AT_KB), 1) // CHUNK
    band = (ki >= qi) & (ki <= qi + LEFT_CHUNKS)
    for h in range(AT_HEADS):
        base = jnp.broadcast_to(f_ref[h:h + 1, :], (AT_QB, width))
        rolled = pltpu.roll(base, width - (AT_QB - 1), 1, stride=1, stride_axis=0)
        o_ref[h] = jnp.where(band, rolled[:, :AT_KB], NEG_INF)


def _attention_bias(rel_table):
    table = rel_table.astype(F32)
    n_clipped = (AT_KB - 1) - REL_CLIP
    f = jnp.concatenate([jnp.broadcast_to(table[:, :1], (AT_HEADS, n_clipped)),
                         table[:, :AT_QB + REL_CLIP + 1]], axis=1)
    return pl.pallas_call(
        _bias_kernel,
        out_shape=jax.ShapeDtypeStruct((AT_HEADS, AT_QB, AT_KB), F32),
        name="attn_bias",
    )(f)


def _layer_norm(y, g, b):
    mu = jnp.mean(y, axis=-1, keepdims=True)
    d = y - mu
    var = jnp.mean(d * d, axis=-1, keepdims=True)
    return d * lax.rsqrt(var + LN_EPS) * g + b


def _merge_kernel(dn_ref, at_ref, ga_ref, gb_ref, x_ref, wa_ref, wb_ref, wo_ref, g_ref, b_ref,
                  x1_ref, x1t_ref, x1b_ref):
    ya = jnp.dot(dn_ref[...], wa_ref[...], preferred_element_type=F32)
    yb = jnp.dot(at_ref[...], wb_ref[...], preferred_element_type=F32)
    mix = _sigmoid(ga_ref[...].astype(F32)) * ya + _sigmoid(gb_ref[...].astype(F32)) * yb
    y = ALPHA * x_ref[...] + jnp.dot(mix.astype(BF16), wo_ref[...], preferred_element_type=F32)
    x1 = _layer_norm(y, g_ref[...], b_ref[...])
    x1_ref[...] = x1
    x1b_ref[...] = x1.astype(BF16)
    tm = x1.shape[0]
    for j in range(ROW_TILES):
        x1t_ref[pl.ds(j, tm, stride=ROW_TILES), :] = x1[:, j * LANES:(j + 1) * LANES]


def _merge(dn, at, proj, x, wa, wb, wo, g, b, tm=512):
    n = x.shape[0]
    return pl.pallas_call(
        _merge_kernel,
        grid=(n // tm,),
        in_specs=[pl.BlockSpec((tm, DN_WIDTH), lambda i: (i, 0)),
                  pl.BlockSpec((tm, AT_WIDTH), lambda i: (i, 0)),
                  pl.BlockSpec((tm, D_MODEL), lambda i: (i, COL_GA)),
                  pl.BlockSpec((tm, D_MODEL), lambda i: (i, COL_GB)),
                  pl.BlockSpec((tm, D_MODEL), lambda i: (i, 0)),
                  pl.BlockSpec((DN_WIDTH, D_MODEL), lambda i: (0, 0)),
                  pl.BlockSpec((AT_WIDTH, D_MODEL), lambda i: (0, 0)),
                  pl.BlockSpec((D_MODEL, D_MODEL), lambda i: (0, 0)),
                  pl.BlockSpec((1, D_MODEL), lambda i: (0, 0)),
                  pl.BlockSpec((1, D_MODEL), lambda i: (0, 0))],
        out_specs=[pl.BlockSpec((tm, D_MODEL), lambda i: (i, 0)),
                   pl.BlockSpec((tm * ROW_TILES, LANES), lambda i: (i, 0)),
                   pl.BlockSpec((tm, D_MODEL), lambda i: (i, 0))],
        out_shape=[jax.ShapeDtypeStruct((n, D_MODEL), F32),
                   jax.ShapeDtypeStruct((n * ROW_TILES, LANES), F32),
                   jax.ShapeDtypeStruct((n, D_MODEL), BF16)],
        compiler_params=_cparams(("parallel",)),
        name="merge_ln1",
    )(dn, at, proj, proj, x, wa, wb, wo, g, b)


def _router_kernel(x_ref, rw_ref, rb_ref, dest_ref, wt_ref, seg_ref):
    gt, ne = GROUP, N_EXPERTS
    logits = lax.dot_general(rw_ref[...], x_ref[...], (((1,), (1,)), ((), ())),
                             preferred_element_type=F32, precision=lax.Precision.HIGHEST)
    scores = _sigmoid(logits)
    sel = scores + rb_ref[:, 0:1]
    eidx = lax.broadcasted_iota(I32, (ne, gt), 0)
    picked = jnp.zeros((ne, gt), jnp.bool_)
    hits, vals = [], []
    for _ in range(TOP_K):
        m = jnp.max(sel, axis=0, keepdims=True)
        idx = jnp.min(jnp.where(sel == m, eidx, ne), axis=0, keepdims=True)
        hit = eidx == idx
        hits.append(hit)
        vals.append(jnp.sum(jnp.where(hit, scores, 0.0), axis=0, keepdims=True))
        sel = jnp.where(hit, -jnp.inf, sel)
        picked = picked | hit
    total = vals[0]
    for v in vals[1:]:
        total = total + v

    onehot = picked.astype(BF16)
    tr = lax.broadcasted_iota(I32, (gt, gt), 0)
    tc = lax.broadcasted_iota(I32, (gt, gt), 1)
    rank = jnp.dot(onehot, (tr < tc).astype(BF16), preferred_element_type=F32)
    count = jnp.sum(picked.astype(F32), axis=1, keepdims=True)
    tiles = jnp.floor((count + (SUBLANES - 1)) * (1.0 / SUBLANES))
    er = lax.broadcasted_iota(I32, (ne, ne), 0)
    ec = lax.broadcasted_iota(I32, (ne, ne), 1)
    start = SUBLANES * jnp.dot((ec < er).astype(BF16),
                               jnp.broadcast_to(tiles, (ne, LANES)).astype(BF16),
                               preferred_element_type=F32)
    slot = start[:, 0:1] + rank

    sub = lax.broadcasted_iota(I32, (SUBLANES, gt), 0)
    dest = jnp.zeros((SUBLANES, gt), F32)
    wts = jnp.zeros((SUBLANES, gt), F32)
    for kk in range(TOP_K):
        d = jnp.sum(jnp.where(hits[kk], slot, 0.0), axis=0, keepdims=True)
        dest = jnp.where(sub == kk, d, dest)
        wts = jnp.where(sub == kk, vals[kk] / total * ROUTED_SCALE, wts)
    dest_ref[0] = dest.astype(I32) * ROW_TILES
    wt_ref[0] = wts
    lane = lax.broadcasted_iota(I32, (ne, LANES), 1)
    seg_ref[0] = jnp.where(lane == 0, start, jnp.broadcast_to(count, (ne, LANES))).astype(I32)


def _router(x1, rw_t, rb):
    n = x1.shape[0]
    ng = n // GROUP
    return pl.pallas_call(
        _router_kernel,
        grid=(ng,),
        in_specs=[pl.BlockSpec((GROUP, D_MODEL), lambda g: (g, 0)),
                  pl.BlockSpec((N_EXPERTS, D_MODEL), lambda g: (0, 0)),
                  pl.BlockSpec((N_EXPERTS, LANES), lambda g: (0, 0))],
        out_specs=[pl.BlockSpec((1, SUBLANES, GROUP), lambda g: (g, 0, 0)),
                   pl.BlockSpec((1, SUBLANES, GROUP), lambda g: (g, 0, 0)),
                   pl.BlockSpec((1, N_EXPERTS, LANES), lambda g: (g, 0, 0))],
        out_shape=[jax.ShapeDtypeStruct((ng, SUBLANES, GROUP), I32),
                   jax.ShapeDtypeStruct((ng, SUBLANES, GROUP), F32),
                   jax.ShapeDtypeStruct((ng, N_EXPERTS, LANES), I32)],
        compiler_params=_cparams(("parallel",)),
        name="router",
    )(x1, rw_t, rb)


def _moe_kernel(start_ref, count_ref, xt_ref, dest_hbm, wt_hbm, wgu_ref, wd_ref, out_ref,
                buf, dest_s, wt_s, sem):
    g = pl.program_id(0)
    e = pl.program_id(1)
    ne, gt = N_EXPERTS, GROUP

    def routing_copies():
        return (pltpu.make_async_copy(dest_hbm.at[g], dest_s, sem.at[0]),
                pltpu.make_async_copy(wt_hbm.at[g], wt_s, sem.at[1]))

    def tile_of(row8):
        return pl.ds(pl.multiple_of(row8, ROW_TILES), ROW_TILES)

    @pl.when(e == 0)
    def _():
        for cp in routing_copies():
            cp.start()
        zeros = jnp.zeros((SUBLANES * ROW_TILES, LANES), F32)

        def zero_pad(ee, carry):
            s = start_ref[g * ne + ee]
            cnt = count_ref[g * ne + ee]
            z0 = pl.multiple_of((s + (cnt // SUBLANES) * SUBLANES) * ROW_TILES, SUBLANES * ROW_TILES)
            buf[pl.ds(z0, SUBLANES * ROW_TILES), :] = zeros
            return carry

        lax.fori_loop(0, ne, zero_pad, 0)
        last = g * ne + ne - 1
        end = start_ref[last] + (count_ref[last] + SUBLANES - 1) // SUBLANES * SUBLANES
        end = pl.multiple_of(end * ROW_TILES, SUBLANES * ROW_TILES)
        buf[pl.ds(end, FFN_BM * ROW_TILES), :] = jnp.zeros((FFN_BM * ROW_TILES, LANES), F32)
        for cp in routing_copies():
            cp.wait()

        def scatter(tok, carry):
            row = xt_ref[tile_of(tok * ROW_TILES), :]
            for kk in range(TOP_K):
                buf[tile_of(dest_s[kk * gt + tok]), :] = row
            return carry

        lax.fori_loop(0, gt, scatter, 0)

    s = start_ref[g * ne + e]
    cnt = count_ref[g * ne + e]

    def ffn_block(i, carry):
        r0 = pl.multiple_of(s + i * FFN_BM, SUBLANES)
        base = pl.multiple_of(r0 * ROW_TILES, SUBLANES * ROW_TILES)
        xin = jnp.concatenate([buf[pl.ds(base + j, FFN_BM, stride=ROW_TILES), :]
                               for j in range(ROW_TILES)], axis=1)
        gu = jnp.dot(xin.astype(BF16), wgu_ref[0], preferred_element_type=F32)
        hidden = _silu(gu[:, :EXPERT_DIM]) * gu[:, EXPERT_DIM:]
        y = jnp.dot(hidden.astype(BF16), wd_ref[0], preferred_element_type=F32)
        keep = (r0 + lax.broadcasted_iota(I32, (FFN_BM, 1), 0)) < s + cnt
        for j in range(ROW_TILES):
            buf[pl.ds(base + j, FFN_BM, stride=ROW_TILES), :] = jnp.where(
                keep, y[:, j * LANES:(j + 1) * LANES], xin[:, j * LANES:(j + 1) * LANES])
        return carry

    lax.fori_loop(0, (cnt + FFN_BM - 1) // FFN_BM, ffn_block, 0)

    @pl.when(e == ne - 1)
    def _():
        def combine(tok, carry):
            acc = buf[tile_of(dest_s[tok]), :] * wt_s[tok]
            for kk in range(1, TOP_K):
                acc = acc + buf[tile_of(dest_s[kk * gt + tok]), :] * wt_s[kk * gt + tok]
            out_ref[tile_of(tok * ROW_TILES), :] = acc
            return carry

        lax.fori_loop(0, gt, combine, 0)


def _moe(x1t, dest, wts, starts, counts, wgu, wd):
    n = x1t.shape[0] // ROW_TILES
    ng = n // GROUP
    grid_spec = pltpu.PrefetchScalarGridSpec(
        num_scalar_prefetch=2,
        grid=(ng, N_EXPERTS),
        in_specs=[pl.BlockSpec((GROUP * ROW_TILES, LANES), lambda g, e, *_: (g, 0)),
                  pl.BlockSpec(memory_space=pl.ANY),
                  pl.BlockSpec(memory_space=pl.ANY),
                  pl.BlockSpec((1, D_MODEL, 2 * EXPERT_DIM), lambda g, e, *_: (e, 0, 0)),
                  pl.BlockSpec((1, EXPERT_DIM, D_MODEL), lambda g, e, *_: (e, 0, 0))],
        out_specs=pl.BlockSpec((GROUP * ROW_TILES, LANES), lambda g, e, *_: (g, 0)),
        scratch_shapes=[pltpu.VMEM((GROUP_ROWS * ROW_TILES, LANES), F32),
                        pltpu.SMEM((SUBLANES * GROUP,), I32),
                        pltpu.SMEM((SUBLANES * GROUP,), F32),
                        pltpu.SemaphoreType.DMA((2,))],
    )
    return pl.pallas_call(
        _moe_kernel,
        grid_spec=grid_spec,
        out_shape=jax.ShapeDtypeStruct((n * ROW_TILES, LANES), F32),
        compiler_params=_cparams(("arbitrary", "arbitrary")),
        name="routed_experts",
    )(starts, counts, x1t, dest, wts, wgu, wd)


def _final_kernel(x1_ref, x1b_ref, rt_ref, wsg_ref, wsd_ref, g_ref, b_ref, o_ref, ob_ref):
    gus = jnp.dot(x1b_ref[...], wsg_ref[...], preferred_element_type=F32)
    hidden = _silu(gus[:, :SHARED_DIM]) * gus[:, SHARED_DIM:]
    shared = jnp.dot(hidden.astype(BF16), wsd_ref[...], preferred_element_type=F32)
    tm = x1_ref.shape[0]
    routed = jnp.concatenate([rt_ref[pl.ds(j, tm, stride=ROW_TILES), :]
                              for j in range(ROW_TILES)], axis=1)
    y = ALPHA * x1_ref[...] + (routed + shared)
    x2 = _layer_norm(y, g_ref[...], b_ref[...])
    o_ref[...] = x2
    ob_ref[...] = x2.astype(BF16)


def _final(x1, x1b, routed_t, wsg, wsd, g, b, tm=512):
    n = x1.shape[0]
    return pl.pallas_call(
        _final_kernel,
        grid=(n // tm,),
        in_specs=[pl.BlockSpec((tm, D_MODEL), lambda i: (i, 0)),
                  pl.BlockSpec((tm, D_MODEL), lambda i: (i, 0)),
                  pl.BlockSpec((tm * ROW_TILES, LANES), lambda i: (i, 0)),
                  pl.BlockSpec((D_MODEL, 2 * SHARED_DIM), lambda i: (0, 0)),
                  pl.BlockSpec((SHARED_DIM, D_MODEL), lambda i: (0, 0)),
                  pl.BlockSpec((1, D_MODEL), lambda i: (0, 0)),
                  pl.BlockSpec((1, D_MODEL), lambda i: (0, 0))],
        out_specs=[pl.BlockSpec((tm, D_MODEL), lambda i: (i, 0)),
                   pl.BlockSpec((tm, D_MODEL), lambda i: (i, 0))],
        out_shape=[jax.ShapeDtypeStruct((n, D_MODEL), F32),
                   jax.ShapeDtypeStruct((n, D_MODEL), BF16)],
        compiler_params=_cparams(("parallel",)),
        name="shared_ln2",
    )(x1, x1b, routed_t, wsg, wsd, g, b)


def _layer(x, xb, batch, seq, w_in, conv_w, a_log, dt_bias, dn_norm_w, rel_table, w_branch_a,
           w_branch_b, w_out, ln1_g, ln1_b, router_w, router_bias, w_gate_up, w_down,
           ws_gate_up, ws_down, ln2_g, ln2_b):
    c_small = 4 * DN_WIDTH
    c_att = c_small + 2 * DN_HEADS
    c_gate = c_att + 3 * AT_WIDTH
    w_big = jnp.concatenate([w_in[:, :c_small], w_in[:, c_gate:], w_in[:, c_att:c_gate]],
                            axis=1).astype(BF16)
    w_small = w_in[:, c_small:c_att]
    w_col = jnp.pad(w_small, ((0, 0), (0, LANES - 2 * DN_HEADS))).astype(BF16)
    w_row = w_small.T.astype(BF16)
    decay_rate = jnp.exp(a_log.astype(F32))
    zeros8 = jnp.zeros((DN_HEADS,), F32)
    p_col = jnp.pad(jnp.stack([jnp.concatenate([zeros8, decay_rate]),
                               jnp.concatenate([zeros8, dt_bias.astype(F32)])]),
                    ((0, 0), (0, LANES - 2 * DN_HEADS)))
    p_row = jnp.pad(p_col[:, :2 * DN_HEADS].T, ((0, 0), (0, LANES - 2)))

    proj = _matmul(xb, w_big, BF16, tm=1024, tn=1280)
    cg, rg = _gates(xb, w_col, w_row, p_col, p_row)
    dn = _deltanet(proj, cg, rg, conv_w.astype(F32), dn_norm_w.astype(F32).reshape(1, DN_HEAD_DIM),
                   batch, seq)
    at = _attention(proj, _attention_bias(rel_table), batch, seq)
    x1, x1t, x1b = _merge(dn, at, proj, x, w_branch_a.astype(BF16), w_branch_b.astype(BF16),
                          w_out.astype(BF16), ln1_g.reshape(1, -1), ln1_b.reshape(1, -1))
    dest, wts, seg = _router(x1, router_w.T.astype(F32),
                             jnp.broadcast_to(router_bias.astype(F32)[:, None], (N_EXPERTS, LANES)))
    ng = dest.shape[0]
    routed_t = _moe(x1t, dest.reshape(ng, SUBLANES * GROUP), wts.reshape(ng, SUBLANES * GROUP),
                    seg[:, :, 0].reshape(-1), seg[:, :, 1].reshape(-1),
                    w_gate_up.astype(BF16), w_down.astype(BF16))
    return _final(x1, x1b, routed_t, ws_gate_up.astype(BF16), ws_down.astype(BF16),
                  ln2_g.reshape(1, -1), ln2_b.reshape(1, -1))


def kernel(x, w_in, conv_w, a_log, dt_bias, dn_norm_w, rel_table, w_branch_a, w_branch_b, w_out,
           ln1_g, ln1_b, router_w, router_bias, expert_w_gate_up, expert_w_down,
           shared_w_gate_up, shared_w_down, ln2_g, ln2_b):
    batch, seq, d = x.shape
    assert d == D_MODEL and seq % DN_TB == 0 and seq % AT_QB == 0 and (batch * seq) % GROUP == 0
    xf = x.reshape(batch * seq, d).astype(F32)
    xb = xf.astype(BF16)
    for l in range(DEPTH):
        xf, xb = _layer(xf, xb, batch, seq, w_in[l], conv_w[l], a_log[l], dt_bias[l], dn_norm_w[l],
                        rel_table[l], w_branch_a[l], w_branch_b[l], w_out[l], ln1_g[l], ln1_b[l],
                        router_w[l], router_bias[l], expert_w_gate_up[l], expert_w_down[l],
                        shared_w_gate_up[l], shared_w_down[l], ln2_g[l], ln2_b[l])
    return xf.reshape(batch, seq, d).astype(x.dtype)
```

```python
import functools

import jax
import jax.numpy as jnp
from jax import lax
from jax.experimental import pallas as pl
from jax.experimental.pallas import tpu as pltpu

F32 = jnp.float32
BF16 = jnp.bfloat16
I32 = jnp.int32

D_MODEL = 1024
DEPTH = 2
CHUNK = 64
DN_HEADS = 8
DN_HEAD_DIM = 128
DN_WIDTH = DN_HEADS * DN_HEAD_DIM
CONV_WIDTH = 4
AT_HEADS = 8
AT_HEAD_DIM = 64
AT_WIDTH = AT_HEADS * AT_HEAD_DIM
LEFT_CHUNKS = 8
REL_CLIP = 256
N_EXPERTS = 64
TOP_K = 6
EXPERT_DIM = 256
SHARED_DIM = 256
ROUTED_SCALE = 2.5
ALPHA = (2 * DEPTH) ** 0.25
LN_EPS = 1e-5
RMS_EPS = 1e-6
L2_EPS = 1e-6
NEG_INF = -1e30

LANES = 128
SUBLANES = 8
ROW_TILES = D_MODEL // LANES

DN_TB = 256
DN_SB = 128
DN_HEADS_PER_PASS = 2
AT_QB = 256
AT_KB = 3 * AT_QB
GROUP = 1024
FFN_BM = 128
MOE_EPS = 2
ROW_LOOP_UNROLL = 4
GROUP_ROWS = GROUP * TOP_K + N_EXPERTS * (SUBLANES - 1) + FFN_BM
GROUP_ROWS = -(-GROUP_ROWS // SUBLANES) * SUBLANES
VMEM_LIMIT = 56 * 1024 * 1024

COL_Q, COL_K, COL_V, COL_Z, COL_GA, COL_GB = 0, 1, 2, 3, 4, 5
COL_AQ, COL_AK, COL_AV = 12, 13, 14
BIG_COLS = 4 * DN_WIDTH + 2 * D_MODEL + 3 * AT_WIDTH


def _sigmoid(x):
    return 1.0 / (1.0 + jnp.exp(-x))


def _silu(x):
    return x * _sigmoid(x)


def _softplus(x):
    return jnp.maximum(x, 0.0) + jnp.log1p(jnp.exp(-jnp.abs(x)))


def _bdot(a, b):
    return jnp.dot(a.astype(BF16), b.astype(BF16), preferred_element_type=F32)


def _bdot_nt(a, b):
    return lax.dot_general(a.astype(BF16), b.astype(BF16), (((1,), (1,)), ((), ())),
                           preferred_element_type=F32)


def _cparams(sem):
    return pltpu.CompilerParams(dimension_semantics=sem, vmem_limit_bytes=VMEM_LIMIT)


def _mm_kernel(x_ref, w_ref, o_ref):
    o_ref[...] = jnp.dot(x_ref[...], w_ref[...], preferred_element_type=F32).astype(o_ref.dtype)


def _matmul(x, w, out_dtype, tm, tn):
    n, k = x.shape
    m = w.shape[1]
    return pl.pallas_call(
        _mm_kernel,
        grid=(n // tm, m // tn),
        in_specs=[pl.BlockSpec((tm, k), lambda i, j: (i, 0)),
                  pl.BlockSpec((k, tn), lambda i, j: (0, j))],
        out_specs=pl.BlockSpec((tm, tn), lambda i, j: (i, j)),
        out_shape=jax.ShapeDtypeStruct((n, m), out_dtype),
        compiler_params=_cparams(("parallel", "parallel")),
        name="inproj",
    )(x, w)


def _gates_kernel(x_ref, wc_ref, wr_ref, pc_ref, pr_ref, cg_ref, rg_ref, *, tm):
    x = x_ref[...]
    r = lax.broadcasted_iota(I32, (tm, tm), 0)
    c = lax.broadcasted_iota(I32, (tm, tm), 1)
    same = (r // CHUNK) == (c // CHUNK)

    pc = jnp.dot(x, wc_ref[...], preferred_element_type=F32)
    lane = lax.broadcasted_iota(I32, (tm, LANES), 1)
    col = jnp.where(lane < DN_HEADS, _sigmoid(pc),
                    -pc_ref[0:1, :] * _softplus(pc + pc_ref[1:2, :]))
    lower = (same & (r >= c)).astype(F32)
    gcc = jnp.dot(lower, col, preferred_element_type=F32, precision=lax.Precision.HIGHEST)
    cg_ref[...] = jnp.where(lane < DN_HEADS, col, gcc)

    pr = lax.dot_general(wr_ref[...], x, (((1,), (1,)), ((), ())),
                         preferred_element_type=F32)
    sub = lax.broadcasted_iota(I32, (2 * DN_HEADS, tm), 0)
    row = jnp.where(sub < DN_HEADS, _sigmoid(pr),
                    -pr_ref[:, 0:1] * _softplus(pr + pr_ref[:, 1:2]))
    upper = (same & (r <= c)).astype(F32)
    gcr = jnp.dot(row, upper, preferred_element_type=F32, precision=lax.Precision.HIGHEST)
    rg_ref[...] = jnp.where(sub < DN_HEADS, row, gcr)


def _gates(xb, w_col, w_row, p_col, p_row, tm=256):
    n = xb.shape[0]
    return pl.pallas_call(
        functools.partial(_gates_kernel, tm=tm),
        grid=(n // tm,),
        in_specs=[pl.BlockSpec((tm, D_MODEL), lambda i: (i, 0)),
                  pl.BlockSpec((D_MODEL, LANES), lambda i: (0, 0)),
                  pl.BlockSpec((2 * DN_HEADS, D_MODEL), lambda i: (0, 0)),
                  pl.BlockSpec((2, LANES), lambda i: (0, 0)),
                  pl.BlockSpec((2 * DN_HEADS, LANES), lambda i: (0, 0))],
        out_specs=[pl.BlockSpec((tm, LANES), lambda i: (i, 0)),
                   pl.BlockSpec((2 * DN_HEADS, tm), lambda i: (0, i))],
        out_shape=[jax.ShapeDtypeStruct((n, LANES), F32),
                   jax.ShapeDtypeStruct((2 * DN_HEADS, n), F32)],
        compiler_params=_cparams(("parallel",)),
        name="dn_gates",
    )(xb, w_col, w_row, p_col, p_row)


def _dn_kernel(q_ref, k_ref, v_ref, z_ref, cg_ref, rg_ref, cw_ref, nw_ref, o_ref,
               xpad, s_ref, u_s, w_s, qd_s, qk_s, kdt_s, vn_s, o_s):
    tb, nch, hd = DN_TB, DN_TB // CHUNK, DN_HEAD_DIM
    t = pl.program_id(1)

    @pl.when(t == 0)
    def _():
        xpad[:, 0:SUBLANES, :] = jnp.zeros((3, SUBLANES, DN_WIDTH), F32)
        s_ref[...] = jnp.zeros_like(s_ref)

    @pl.when(t > 0)
    def _():
        xpad[:, 0:SUBLANES, :] = xpad[:, tb:tb + SUBLANES, :]

    xpad[0, SUBLANES:SUBLANES + tb, :] = q_ref[...].astype(F32)
    xpad[1, SUBLANES:SUBLANES + tb, :] = k_ref[...].astype(F32)
    xpad[2, SUBLANES:SUBLANES + tb, :] = v_ref[...].astype(F32)
    vn_s[...] = jnp.zeros_like(vn_s)

    sb = DN_SB
    halves = tb // sb
    r = lax.broadcasted_iota(I32, (sb, sb), 0)
    c = lax.broadcasted_iota(I32, (sb, sb), 1)
    same = (r // CHUNK) == (c // CHUNK)
    low_incl = same & (r >= c)
    strict = same & (r > c)
    blk16 = (r // 16) == (c // 16)
    eye = (r == c).astype(F32)
    lane_chunk = lax.broadcasted_iota(I32, (hd, sb), 1) // CHUNK

    def each(fn, *lists):
        return [fn(*args) for args in zip(*lists)]

    for h0 in range(0, DN_HEADS, DN_HEADS_PER_PASS):
        inst = [(h, half) for h in range(h0, h0 + DN_HEADS_PER_PASS) for half in range(halves)]

        def conv(idx, h, half):
            acc = None
            for i in range(CONV_WIDTH):
                col0 = idx * DN_WIDTH + h * hd
                start = SUBLANES - (CONV_WIDTH - 1) + i + half * sb
                term = (xpad[idx, start:start + sb, h * hd:(h + 1) * hd]
                        * cw_ref[i:i + 1, col0:col0 + hd])
                acc = term if acc is None else acc + term
            return _silu(acc)

        qh = [conv(0, h, half) for h, half in inst]
        kh = [conv(1, h, half) for h, half in inst]
        vh = [conv(2, h, half) for h, half in inst]
        qh = each(lambda x: x * lax.rsqrt(jnp.sum(x * x, axis=-1, keepdims=True) + L2_EPS)
                  * (hd ** -0.5), qh)
        kh = each(lambda x: x * lax.rsqrt(jnp.sum(x * x, axis=-1, keepdims=True) + L2_EPS), kh)
        beta = [cg_ref[half * sb:(half + 1) * sb, h:h + 1] for h, half in inst]
        gc = [cg_ref[half * sb:(half + 1) * sb, DN_HEADS + h:DN_HEADS + h + 1] for h, half in inst]
        gcr = [rg_ref[DN_HEADS + h:DN_HEADS + h + 1, half * sb:(half + 1) * sb] for h, half in inst]
        dec = each(lambda a, b: jnp.exp(jnp.where(low_incl, a - b, NEG_INF)), gc, gcr)
        kb = each(lambda a, b: a * b, kh, beta)
        khb = each(lambda a: a.astype(BF16), kh)
        lmat = each(lambda a, b, d: jnp.where(strict, _bdot_nt(a, b) * d, 0.0), kb, khb, dec)
        qk = each(lambda a, b, d: _bdot_nt(a, b) * d, qh, khb, dec)
        dmat = each(lambda a: jnp.where(blk16, a, 0.0), lmat)
        omat = each(lambda a, b: a - b, lmat, dmat)
        s1 = each(lambda a: eye - a, dmat)
        d2 = each(lambda a: _bdot(a, a), dmat)
        s2 = each(lambda a, b: a + _bdot(a, b), s1, d2)
        d4 = each(lambda a: _bdot(a, a), d2)
        s3 = each(lambda a, b: a + _bdot(a, b), s2, d4)
        d8 = each(lambda a: _bdot(a, a), d4)
        dinv = each(lambda a, b: a + _bdot(a, b), s3, d8)
        m1 = each(_bdot, dinv, omat)
        m2 = each(lambda a: _bdot(a, a), m1)
        im = each(lambda a: eye - a, m1)
        pm = each(lambda a, b: a + _bdot(a, b), im, m2)
        tinv = each(_bdot, pm, dinv)
        eg = each(jnp.exp, gc)
        rhs = each(lambda v, b, k, e: jnp.concatenate([v * b, k * e], axis=1), vh, beta, kb, eg)
        uw = each(_bdot, tinv, rhs)
        for i, (h, half) in enumerate(inst):
            rs = slice(half * sb, (half + 1) * sb)
            u_s[h, rs, :] = uw[i][:, :hd]
            w_s[h, rs, :] = uw[i][:, hd:].astype(BF16)
            qd_s[h, rs, :] = (qh[i] * eg[i]).astype(BF16)
            qk_s[h, rs, :] = qk[i].astype(BF16)
            gl = jnp.concatenate(
                [jnp.broadcast_to(gc[i][cc * CHUNK + CHUNK - 1:cc * CHUNK + CHUNK, :], (CHUNK, 1))
                 for cc in range(sb // CHUNK)], axis=0)
            kdt = (kh[i] * jnp.exp(gl - gc[i])).T
            for cc in range(sb // CHUNK):
                kdt_s[h, half * (sb // CHUNK) + cc] = jnp.where(lane_chunk == cc, kdt,
                                                                0.0).astype(BF16)

    heads = list(range(DN_HEADS))

    def chunk_step(cc, carry):
        r0 = pl.multiple_of(cc * CHUNK, CHUNK)
        rows = pl.ds(r0, CHUNK)
        sub_rows = pl.ds(pl.multiple_of((cc // (sb // CHUNK)) * sb, sb), sb)
        s_old = [s_ref[h] for h in heads]
        s_b = each(lambda a: a.astype(BF16), s_old)
        v_new = [u_s[h, rows, :] - jnp.dot(w_s[h, rows, :], s_b[h], preferred_element_type=F32)
                 for h in heads]
        o_inter = [jnp.dot(qd_s[h, rows, :], s_b[h], preferred_element_type=F32) for h in heads]
        for h in heads:
            vn_s[h, rows, :] = v_new[h].astype(BF16)
        vn_all = [vn_s[h, sub_rows, :] for h in heads]
        o_intra = [jnp.dot(qk_s[h, rows, :], vn_all[h], preferred_element_type=F32) for h in heads]
        s_add = [jnp.dot(kdt_s[h, cc], vn_all[h], preferred_element_type=F32) for h in heads]
        for h in heads:
            g_last = cg_ref[pl.ds(r0 + CHUNK - 1, 1), DN_HEADS + h:DN_HEADS + h + 1]
            s_ref[h] = s_old[h] * jnp.exp(g_last) + s_add[h]
            o_s[rows, h * hd:(h + 1) * hd] = o_inter[h] + o_intra[h]
        return carry

    lax.fori_loop(0, nch, chunk_step, 0)

    for h in range(DN_HEADS):
        sl = slice(h * hd, (h + 1) * hd)
        o = o_s[:, sl]
        o = o * lax.rsqrt(jnp.mean(o * o, axis=-1, keepdims=True) + RMS_EPS)
        z = z_ref[:, sl].astype(F32)
        o_ref[:, sl] = (o * nw_ref[...] * _silu(z)).astype(o_ref.dtype)


def _deltanet(proj, cg, rg, conv_w, norm_w, batch, seq):
    n = batch * seq
    nt = seq // DN_TB
    tb, hd, nch = DN_TB, DN_HEAD_DIM, DN_TB // CHUNK

    def col(j):
        return pl.BlockSpec((tb, DN_WIDTH), lambda b, t, j=j: (b * nt + t, j))

    return pl.pallas_call(
        _dn_kernel,
        grid=(batch, nt),
        in_specs=[col(COL_Q), col(COL_K), col(COL_V), col(COL_Z),
                  pl.BlockSpec((tb, LANES), lambda b, t: (b * nt + t, 0)),
                  pl.BlockSpec((2 * DN_HEADS, tb), lambda b, t: (0, b * nt + t)),
                  pl.BlockSpec((CONV_WIDTH, 3 * DN_WIDTH), lambda b, t: (0, 0)),
                  pl.BlockSpec((1, hd), lambda b, t: (0, 0))],
        out_specs=pl.BlockSpec((tb, DN_WIDTH), lambda b, t: (b * nt + t, 0)),
        out_shape=jax.ShapeDtypeStruct((n, DN_WIDTH), BF16),
        scratch_shapes=[
            pltpu.VMEM((3, tb + SUBLANES, DN_WIDTH), F32),
            pltpu.VMEM((DN_HEADS, hd, hd), F32),
            pltpu.VMEM((DN_HEADS, tb, hd), F32),
            pltpu.VMEM((DN_HEADS, tb, hd), BF16),
            pltpu.VMEM((DN_HEADS, tb, hd), BF16),
            pltpu.VMEM((DN_HEADS, tb, DN_SB), BF16),
            pltpu.VMEM((DN_HEADS, nch, hd, DN_SB), BF16),
            pltpu.VMEM((DN_HEADS, tb, hd), BF16),
            pltpu.VMEM((tb, DN_WIDTH), F32),
        ],
        compiler_params=_cparams(("parallel", "arbitrary")),
        name="deltanet",
    )(proj, proj, proj, proj, cg, rg, conv_w, norm_w)


def _attn_kernel(q_ref, k0_ref, k1_ref, k2_ref, v0_ref, v1_ref, v2_ref, bias_ref, o_ref):
    t = pl.program_id(1)
    q = q_ref[...]
    k = jnp.concatenate([k0_ref[...], k1_ref[...], k2_ref[...]], axis=0)
    v = jnp.concatenate([v0_ref[...], v1_ref[...], v2_ref[...]], axis=0)
    kidx = lax.broadcasted_iota(I32, (1, AT_KB), 1)
    key_valid = kidx >= (2 - t) * AT_QB
    lane_head = lax.broadcasted_iota(I32, (1, LANES), 1) // AT_HEAD_DIM
    scale = AT_HEAD_DIM ** -0.5
    for p in range(AT_HEADS // 2):
        sl = slice(p * LANES, (p + 1) * LANES)
        qp, kp, vp = q[:, sl], k[:, sl], v[:, sl]
        out = jnp.zeros((AT_QB, LANES), F32)
        for hh in range(2):
            mine = lane_head == hh
            qm = jnp.where(mine, qp, jnp.zeros_like(qp))
            s = lax.dot_general(qm, kp, (((1,), (1,)), ((), ())), preferred_element_type=F32)
            s = s * scale + bias_ref[2 * p + hh]
            s = jnp.where(key_valid, s, NEG_INF)
            e = jnp.exp(s - jnp.max(s, axis=-1, keepdims=True))
            pv = jnp.dot(e.astype(BF16), vp, preferred_element_type=F32)
            pv = pv / jnp.sum(e, axis=-1, keepdims=True)
            out = jnp.where(mine, pv, out)
        o_ref[:, sl] = out.astype(o_ref.dtype)


def _attention(proj, bias, batch, seq):
    n = batch * seq
    nt = seq // AT_QB

    def kv(col, back):
        return pl.BlockSpec((AT_QB, AT_WIDTH),
                            lambda b, t, col=col, back=back: (b * nt + jnp.maximum(t - back, 0), col))

    return pl.pallas_call(
        _attn_kernel,
        grid=(batch, nt),
        in_specs=[pl.BlockSpec((AT_QB, AT_WIDTH), lambda b, t: (b * nt + t, COL_AQ)),
                  kv(COL_AK, 2), kv(COL_AK, 1), kv(COL_AK, 0),
                  kv(COL_AV, 2), kv(COL_AV, 1), kv(COL_AV, 0),
                  pl.BlockSpec((AT_HEADS, AT_QB, AT_KB), lambda b, t: (0, 0, 0))],
        out_specs=pl.BlockSpec((AT_QB, AT_WIDTH), lambda b, t: (b * nt + t, 0)),
        out_shape=jax.ShapeDtypeStruct((n, AT_WIDTH), BF16),
        compiler_params=_cparams(("parallel", "parallel")),
        name="band_attention",
    )(proj, proj, proj, proj, proj, proj, proj, bias)


def _bias_kernel(f_ref, o_ref):
    width = f_ref.shape[1]
    qi = lax.broadcasted_iota(I32, (AT_QB, AT_KB), 0) // CHUNK
    ki = lax.broadcasted_iota(I32, (AT_QB, AT_KB), 1) // CHUNK
    band = (ki >= qi) & (ki <= qi + LEFT_CHUNKS)
    for h in range(AT_HEADS):
        base = jnp.broadcast_to(f_ref[h:h + 1, :], (AT_QB, width))
        rolled = pltpu.roll(base, width - (AT_QB - 1), 1, stride=1, stride_axis=0)
        o_ref[h] = jnp.where(band, rolled[:, :AT_KB], NEG_INF)


def _attention_bias(rel_table):
    table = rel_table.astype(F32)
    n_clipped = (AT_KB - 1) - REL_CLIP
    f = jnp.concatenate([jnp.broadcast_to(table[:, :1], (AT_HEADS, n_clipped)),
                         table[:, :AT_QB + REL_CLIP + 1]], axis=1)
    return pl.pallas_call(
        _bias_kernel,
        out_shape=jax.ShapeDtypeStruct((AT_HEADS, AT_QB, AT_KB), F32),
        name="attn_bias",
    )(f)


def _layer_norm(y, g, b):
    mu = jnp.mean(y, axis=-1, keepdims=True)
    d = y - mu
    var = jnp.mean(d * d, axis=-1, keepdims=True)
    return d * lax.rsqrt(var + LN_EPS) * g + b


def _merge_kernel(dn_ref, at_ref, ga_ref, gb_ref, x_ref, wa_ref, wb_ref, wo_ref, g_ref, b_ref,
                  x1_ref, x1t_ref, x1b_ref):
    ya = jnp.dot(dn_ref[...], wa_ref[...], preferred_element_type=F32)
    yb = jnp.dot(at_ref[...], wb_ref[...], preferred_element_type=F32)
    mix = _sigmoid(ga_ref[...].astype(F32)) * ya + _sigmoid(gb_ref[...].astype(F32)) * yb
    y = ALPHA * x_ref[...] + jnp.dot(mix.astype(BF16), wo_ref[...], preferred_element_type=F32)
    x1 = _layer_norm(y, g_ref[...], b_ref[...])
    x1_ref[...] = x1
    x1b_ref[...] = x1.astype(BF16)
    tm = x1.shape[0]
    for j in range(ROW_TILES):
        x1t_ref[pl.ds(j, tm, stride=ROW_TILES), :] = x1[:, j * LANES:(j + 1) * LANES]


def _merge(dn, at, proj, x, wa, wb, wo, g, b, tm=512):
    n = x.shape[0]
    return pl.pallas_call(
        _merge_kernel,
        grid=(n // tm,),
        in_specs=[pl.BlockSpec((tm, DN_WIDTH), lambda i: (i, 0)),
                  pl.BlockSpec((tm, AT_WIDTH), lambda i: (i, 0)),
                  pl.BlockSpec((tm, D_MODEL), lambda i: (i, COL_GA)),
                  pl.BlockSpec((tm, D_MODEL), lambda i: (i, COL_GB)),
                  pl.BlockSpec((tm, D_MODEL), lambda i: (i, 0)),
                  pl.BlockSpec((DN_WIDTH, D_MODEL), lambda i: (0, 0)),
                  pl.BlockSpec((AT_WIDTH, D_MODEL), lambda i: (0, 0)),
                  pl.BlockSpec((D_MODEL, D_MODEL), lambda i: (0, 0)),
                  pl.BlockSpec((1, D_MODEL), lambda i: (0, 0)),
                  pl.BlockSpec((1, D_MODEL), lambda i: (0, 0))],
        out_specs=[pl.BlockSpec((tm, D_MODEL), lambda i: (i, 0)),
                   pl.BlockSpec((tm * ROW_TILES, LANES), lambda i: (i, 0)),
                   pl.BlockSpec((tm, D_MODEL), lambda i: (i, 0))],
        out_shape=[jax.ShapeDtypeStruct((n, D_MODEL), F32),
                   jax.ShapeDtypeStruct((n * ROW_TILES, LANES), F32),
                   jax.ShapeDtypeStruct((n, D_MODEL), BF16)],
        compiler_params=_cparams(("parallel",)),
        name="merge_ln1",
    )(dn, at, proj, proj, x, wa, wb, wo, g, b)


def _router_kernel(x_ref, rw_ref, rb_ref, dest_ref, wt_ref, seg_ref):
    gt, ne = GROUP, N_EXPERTS
    logits = lax.dot_general(rw_ref[...], x_ref[...], (((1,), (1,)), ((), ())),
                             preferred_element_type=F32, precision=lax.Precision.HIGHEST)
    scores = _sigmoid(logits)
    sel = scores + rb_ref[:, 0:1]
    eidx = lax.broadcasted_iota(I32, (ne, gt), 0)
    picked = jnp.zeros((ne, gt), jnp.bool_)
    hits, vals = [], []
    for _ in range(TOP_K):
        m = jnp.max(sel, axis=0, keepdims=True)
        idx = jnp.min(jnp.where(sel == m, eidx, ne), axis=0, keepdims=True)
        hit = eidx == idx
        hits.append(hit)
        vals.append(jnp.sum(jnp.where(hit, scores, 0.0), axis=0, keepdims=True))
        sel = jnp.where(hit, -jnp.inf, sel)
        picked = picked | hit
    total = vals[0]
    for v in vals[1:]:
        total = total + v

    onehot = picked.astype(BF16)
    tr = lax.broadcasted_iota(I32, (gt, gt), 0)
    tc = lax.broadcasted_iota(I32, (gt, gt), 1)
    rank = jnp.dot(onehot, (tr < tc).astype(BF16), preferred_element_type=F32)
    count = jnp.sum(picked.astype(F32), axis=1, keepdims=True)
    tiles = jnp.floor((count + (SUBLANES - 1)) * (1.0 / SUBLANES))
    er = lax.broadcasted_iota(I32, (ne, ne), 0)
    ec = lax.broadcasted_iota(I32, (ne, ne), 1)
    start = SUBLANES * jnp.dot((ec < er).astype(BF16),
                               jnp.broadcast_to(tiles, (ne, LANES)).astype(BF16),
                               preferred_element_type=F32)
    slot = start[:, 0:1] + rank

    sub = lax.broadcasted_iota(I32, (SUBLANES, gt), 0)
    dest = jnp.zeros((SUBLANES, gt), F32)
    wts = jnp.zeros((SUBLANES, gt), F32)
    for kk in range(TOP_K):
        d = jnp.sum(jnp.where(hits[kk], slot, 0.0), axis=0, keepdims=True)
        dest = jnp.where(sub == kk, d, dest)
        wts = jnp.where(sub == kk, vals[kk] / total * ROUTED_SCALE, wts)
    dest_ref[0] = dest.astype(I32) * ROW_TILES
    wt_ref[0] = wts
    lane = lax.broadcasted_iota(I32, (ne, LANES), 1)
    seg_ref[0] = jnp.where(lane == 0, start, jnp.broadcast_to(count, (ne, LANES))).astype(I32)


def _router(x1, rw_t, rb):
    n = x1.shape[0]
    ng = n // GROUP
    return pl.pallas_call(
        _router_kernel,
        grid=(ng,),
        in_specs=[pl.BlockSpec((GROUP, D_MODEL), lambda g: (g, 0)),
                  pl.BlockSpec((N_EXPERTS, D_MODEL), lambda g: (0, 0)),
                  pl.BlockSpec((N_EXPERTS, LANES), lambda g: (0, 0))],
        out_specs=[pl.BlockSpec((1, SUBLANES, GROUP), lambda g: (g, 0, 0)),
                   pl.BlockSpec((1, SUBLANES, GROUP), lambda g: (g, 0, 0)),
                   pl.BlockSpec((1, N_EXPERTS, LANES), lambda g: (g, 0, 0))],
        out_shape=[jax.ShapeDtypeStruct((ng, SUBLANES, GROUP), I32),
                   jax.ShapeDtypeStruct((ng, SUBLANES, GROUP), F32),
                   jax.ShapeDtypeStruct((ng, N_EXPERTS, LANES), I32)],
        compiler_params=_cparams(("parallel",)),
        name="router",
    )(x1, rw_t, rb)


def _moe_kernel(start_ref, count_ref, xt_ref, dest_hbm, wt_hbm, wgu_ref, wd_ref, out_ref,
                buf, dest_s, wt_s, sem):
    g = pl.program_id(0)
    e = pl.program_id(1)
    ne, gt = N_EXPERTS, GROUP

    def routing_copies():
        return (pltpu.make_async_copy(dest_hbm.at[g], dest_s, sem.at[0]),
                pltpu.make_async_copy(wt_hbm.at[g], wt_s, sem.at[1]))

    def tile_of(row8):
        return pl.ds(pl.multiple_of(row8, ROW_TILES), ROW_TILES)

    @pl.when(e == 0)
    def _():
        for cp in routing_copies():
            cp.start()
        zeros = jnp.zeros((SUBLANES * ROW_TILES, LANES), F32)

        def zero_pad(ee, carry):
            s = start_ref[g * ne + ee]
            cnt = count_ref[g * ne + ee]
            z0 = pl.multiple_of((s + (cnt // SUBLANES) * SUBLANES) * ROW_TILES, SUBLANES * ROW_TILES)
            buf[pl.ds(z0, SUBLANES * ROW_TILES), :] = zeros
            return carry

        lax.fori_loop(0, ne, zero_pad, 0)
        last = g * ne + ne - 1
        end = start_ref[last] + (count_ref[last] + SUBLANES - 1) // SUBLANES * SUBLANES
        end = pl.multiple_of(end * ROW_TILES, SUBLANES * ROW_TILES)
        buf[pl.ds(end, FFN_BM * ROW_TILES), :] = jnp.zeros((FFN_BM * ROW_TILES, LANES), F32)
        for cp in routing_copies():
            cp.wait()

        def scatter(t0, carry):
            for u in range(ROW_LOOP_UNROLL):
                tok = t0 * ROW_LOOP_UNROLL + u
                row = xt_ref[tile_of(tok * ROW_TILES), :]
                for kk in range(TOP_K):
                    buf[tile_of(dest_s[kk * gt + tok]), :] = row
            return carry

        lax.fori_loop(0, gt // ROW_LOOP_UNROLL, scatter, 0)

    experts = list(range(MOE_EPS))
    seg = [(start_ref[g * ne + e * MOE_EPS + k], count_ref[g * ne + e * MOE_EPS + k])
           for k in experts]
    n_blocks = [(cnt + FFN_BM - 1) // FFN_BM for _, cnt in seg]
    last_block_row = GROUP_ROWS - FFN_BM

    def ffn_block(i, carry):
        r0 = [pl.multiple_of(jnp.minimum(s + i * FFN_BM, last_block_row), SUBLANES) for s, _ in seg]
        base = [pl.multiple_of(r * ROW_TILES, SUBLANES * ROW_TILES) for r in r0]
        xin = [jnp.concatenate([buf[pl.ds(base[k] + j, FFN_BM, stride=ROW_TILES), :]
                                for j in range(ROW_TILES)], axis=1) for k in experts]
        gu = [jnp.dot(xin[k].astype(BF16), wgu_ref[k], preferred_element_type=F32)
              for k in experts]
        hidden = [(_silu(gu[k][:, :EXPERT_DIM]) * gu[k][:, EXPERT_DIM:]).astype(BF16)
                  for k in experts]
        y = [jnp.dot(hidden[k], wd_ref[k], preferred_element_type=F32) for k in experts]
        for k in experts:
            s, cnt = seg[k]
            keep = (r0[k] + lax.broadcasted_iota(I32, (FFN_BM, 1), 0)) < s + cnt
            for j in range(ROW_TILES):
                buf[pl.ds(base[k] + j, FFN_BM, stride=ROW_TILES), :] = jnp.where(
                    keep, y[k][:, j * LANES:(j + 1) * LANES], xin[k][:, j * LANES:(j + 1) * LANES])
        return carry

    total_blocks = n_blocks[0]
    for nb in n_blocks[1:]:
        total_blocks = jnp.maximum(total_blocks, nb)
    lax.fori_loop(0, total_blocks, ffn_block, 0)

    @pl.when(e == pl.num_programs(1) - 1)
    def _():
        def combine(t0, carry):
            for u in range(ROW_LOOP_UNROLL):
                tok = t0 * ROW_LOOP_UNROLL + u
                acc = buf[tile_of(dest_s[tok]), :] * wt_s[tok]
                for kk in range(1, TOP_K):
                    acc = acc + buf[tile_of(dest_s[kk * gt + tok]), :] * wt_s[kk * gt + tok]
                out_ref[tile_of(tok * ROW_TILES), :] = acc
            return carry

        lax.fori_loop(0, gt // ROW_LOOP_UNROLL, combine, 0)


def _moe(x1t, dest, wts, starts, counts, wgu, wd):
    n = x1t.shape[0] // ROW_TILES
    ng = n // GROUP
    grid_spec = pltpu.PrefetchScalarGridSpec(
        num_scalar_prefetch=2,
        grid=(ng, N_EXPERTS // MOE_EPS),
        in_specs=[pl.BlockSpec((GROUP * ROW_TILES, LANES), lambda g, e, *_: (g, 0)),
                  pl.BlockSpec(memory_space=pl.ANY),
                  pl.BlockSpec(memory_space=pl.ANY),
                  pl.BlockSpec((MOE_EPS, D_MODEL, 2 * EXPERT_DIM), lambda g, e, *_: (e, 0, 0)),
                  pl.BlockSpec((MOE_EPS, EXPERT_DIM, D_MODEL), lambda g, e, *_: (e, 0, 0))],
        out_specs=pl.BlockSpec((GROUP * ROW_TILES, LANES), lambda g, e, *_: (g, 0)),
        scratch_shapes=[pltpu.VMEM((GROUP_ROWS * ROW_TILES, LANES), F32),
                        pltpu.SMEM((SUBLANES * GROUP,), I32),
                        pltpu.SMEM((SUBLANES * GROUP,), F32),
                        pltpu.SemaphoreType.DMA((2,))],
    )
    return pl.pallas_call(
        _moe_kernel,
        grid_spec=grid_spec,
        out_shape=jax.ShapeDtypeStruct((n * ROW_TILES, LANES), F32),
        compiler_params=_cparams(("arbitrary", "arbitrary")),
        name="routed_experts",
    )(starts, counts, x1t, dest, wts, wgu, wd)


def _final_kernel(x1_ref, x1b_ref, rt_ref, wsg_ref, wsd_ref, g_ref, b_ref, o_ref, ob_ref):
    gus = jnp.dot(x1b_ref[...], wsg_ref[...], preferred_element_type=F32)
    hidden = _silu(gus[:, :SHARED_DIM]) * gus[:, SHARED_DIM:]
    shared = jnp.dot(hidden.astype(BF16), wsd_ref[...], preferred_element_type=F32)
    tm = x1_ref.shape[0]
    routed = jnp.concatenate([rt_ref[pl.ds(j, tm, stride=ROW_TILES), :]
                              for j in range(ROW_TILES)], axis=1)
    y = ALPHA * x1_ref[...] + (routed + shared)
    x2 = _layer_norm(y, g_ref[...], b_ref[...])
    o_ref[...] = x2
    ob_ref[...] = x2.astype(BF16)


def _final(x1, x1b, routed_t, wsg, wsd, g, b, tm=512):
    n = x1.shape[0]
    return pl.pallas_call(
        _final_kernel,
        grid=(n // tm,),
        in_specs=[pl.BlockSpec((tm, D_MODEL), lambda i: (i, 0)),
                  pl.BlockSpec((tm, D_MODEL), lambda i: (i, 0)),
                  pl.BlockSpec((tm * ROW_TILES, LANES), lambda i: (i, 0)),
                  pl.BlockSpec((D_MODEL, 2 * SHARED_DIM), lambda i: (0, 0)),
                  pl.BlockSpec((SHARED_DIM, D_MODEL), lambda i: (0, 0)),
                  pl.BlockSpec((1, D_MODEL), lambda i: (0, 0)),
                  pl.BlockSpec((1, D_MODEL), lambda i: (0, 0))],
        out_specs=[pl.BlockSpec((tm, D_MODEL), lambda i: (i, 0)),
                   pl.BlockSpec((tm, D_MODEL), lambda i: (i, 0))],
        out_shape=[jax.ShapeDtypeStruct((n, D_MODEL), F32),
                   jax.ShapeDtypeStruct((n, D_MODEL), BF16)],
        compiler_params=_cparams(("parallel",)),
        name="shared_ln2",
    )(x1, x1b, routed_t, wsg, wsd, g, b)


def _layer(x, xb, batch, seq, w_in, conv_w, a_log, dt_bias, dn_norm_w, rel_table, w_branch_a,
           w_branch_b, w_out, ln1_g, ln1_b, router_w, router_bias, w_gate_up, w_down,
           ws_gate_up, ws_down, ln2_g, ln2_b):
    c_small = 4 * DN_WIDTH
    c_att = c_small + 2 * DN_HEADS
    c_gate = c_att + 3 * AT_WIDTH
    w_big = jnp.concatenate([w_in[:, :c_small], w_in[:, c_gate:], w_in[:, c_att:c_gate]],
                            axis=1).astype(BF16)
    w_small = w_in[:, c_small:c_att]
    w_col = jnp.pad(w_small, ((0, 0), (0, LANES - 2 * DN_HEADS))).astype(BF16)
    w_row = w_small.T.astype(BF16)
    decay_rate = jnp.exp(a_log.astype(F32))
    zeros8 = jnp.zeros((DN_HEADS,), F32)
    p_col = jnp.pad(jnp.stack([jnp.concatenate([zeros8, decay_rate]),
                               jnp.concatenate([zeros8, dt_bias.astype(F32)])]),
                    ((0, 0), (0, LANES - 2 * DN_HEADS)))
    p_row = jnp.pad(p_col[:, :2 * DN_HEADS].T, ((0, 0), (0, LANES - 2)))

    proj = _matmul(xb, w_big, BF16, tm=1024, tn=1280)
    cg, rg = _gates(xb, w_col, w_row, p_col, p_row)
    dn = _deltanet(proj, cg, rg, conv_w.astype(F32), dn_norm_w.astype(F32).reshape(1, DN_HEAD_DIM),
                   batch, seq)
    at = _attention(proj, _attention_bias(rel_table), batch, seq)
    x1, x1t, x1b = _merge(dn, at, proj, x, w_branch_a.astype(BF16), w_branch_b.astype(BF16),
                          w_out.astype(BF16), ln1_g.reshape(1, -1), ln1_b.reshape(1, -1))
    dest, wts, seg = _router(x1, router_w.T.astype(F32),
                             jnp.broadcast_to(router_bias.astype(F32)[:, None], (N_EXPERTS, LANES)))
    ng = dest.shape[0]
    routed_t = _moe(x1t, dest.reshape(ng, SUBLANES * GROUP), wts.reshape(ng, SUBLANES * GROUP),
                    seg[:, :, 0].reshape(-1), seg[:, :, 1].reshape(-1),
                    w_gate_up.astype(BF16), w_down.astype(BF16))
    return _final(x1, x1b, routed_t, ws_gate_up.astype(BF16), ws_down.astype(BF16),
                  ln2_g.reshape(1, -1), ln2_b.reshape(1, -1))


def kernel(x, w_in, conv_w, a_log, dt_bias, dn_norm_w, rel_table, w_branch_a, w_branch_b, w_out,
           ln1_g, ln1_b, router_w, router_bias, expert_w_gate_up, expert_w_down,
           shared_w_gate_up, shared_w_down, ln2_g, ln2_b):
    batch, seq, d = x.shape
    assert d == D_MODEL and seq % DN_TB == 0 and seq % AT_QB == 0 and (batch * seq) % GROUP == 0
    xf = x.reshape(batch * seq, d).astype(F32)
    xb = xf.astype(BF16)
    for l in range(DEPTH):
        xf, xb = _layer(xf, xb, batch, seq, w_in[l], conv_w[l], a_log[l], dt_bias[l], dn_norm_w[l],
                        rel_table[l], w_branch_a[l], w_branch_b[l], w_out[l], ln1_g[l], ln1_b[l],
                        router_w[l], router_bias[l], expert_w_gate_up[l], expert_w_down[l],
                        shared_w_gate_up[l], shared_w_down[l], ln2_g[l], ln2_b[l])
    return xf.reshape(batch, seq, d).astype(x.dtype)
```

```python
import functools

import jax
import jax.numpy as jnp
from jax import lax
from jax.experimental import pallas as pl
from jax.experimental.pallas import tpu as pltpu

F32 = jnp.float32
BF16 = jnp.bfloat16
I32 = jnp.int32

D_MODEL = 1024
DEPTH = 2
CHUNK = 64
DN_HEADS = 8
DN_HEAD_DIM = 128
DN_WIDTH = DN_HEADS * DN_HEAD_DIM
CONV_WIDTH = 4
AT_HEADS = 8
AT_HEAD_DIM = 64
AT_WIDTH = AT_HEADS * AT_HEAD_DIM
LEFT_CHUNKS = 8
REL_CLIP = 256
N_EXPERTS = 64
TOP_K = 6
EXPERT_DIM = 256
SHARED_DIM = 256
ROUTED_SCALE = 2.5
ALPHA = (2 * DEPTH) ** 0.25
LN_EPS = 1e-5
RMS_EPS = 1e-6
L2_EPS = 1e-6
NEG_INF = -1e30

LANES = 128
SUBLANES = 8
ROW_TILES = D_MODEL // LANES

DN_TB = 256
DN_SB = 128
DN_HEADS_PER_PASS = 8
AT_QB = 256
AT_KB = 3 * AT_QB
GROUP = 1024
FFN_BM = 128
MOE_EPS = 2
ROW_LOOP_UNROLL = 4
GROUP_ROWS = GROUP * TOP_K + N_EXPERTS * (SUBLANES - 1) + FFN_BM
GROUP_ROWS = -(-GROUP_ROWS // SUBLANES) * SUBLANES
VMEM_LIMIT = 56 * 1024 * 1024

COL_Q, COL_K, COL_V, COL_Z, COL_GA, COL_GB = 0, 1, 2, 3, 4, 5
COL_AQ, COL_AK, COL_AV = 12, 13, 14
BIG_COLS = 4 * DN_WIDTH + 2 * D_MODEL + 3 * AT_WIDTH


def _sigmoid(x):
    return 1.0 / (1.0 + jnp.exp(-x))


def _silu(x):
    return x * _sigmoid(x)


def _softplus(x):
    return jnp.maximum(x, 0.0) + jnp.log1p(jnp.exp(-jnp.abs(x)))


def _bdot(a, b):
    return jnp.dot(a.astype(BF16), b.astype(BF16), preferred_element_type=F32)


def _bdot_nt(a, b):
    return lax.dot_general(a.astype(BF16), b.astype(BF16), (((1,), (1,)), ((), ())),
                           preferred_element_type=F32)


def _cparams(sem):
    return pltpu.CompilerParams(dimension_semantics=sem, vmem_limit_bytes=VMEM_LIMIT)


def _mm_kernel(x_ref, w_ref, o_ref):
    o_ref[...] = jnp.dot(x_ref[...], w_ref[...], preferred_element_type=F32).astype(o_ref.dtype)


def _matmul(x, w, out_dtype, tm, tn):
    n, k = x.shape
    m = w.shape[1]
    return pl.pallas_call(
        _mm_kernel,
        grid=(n // tm, m // tn),
        in_specs=[pl.BlockSpec((tm, k), lambda i, j: (i, 0)),
                  pl.BlockSpec((k, tn), lambda i, j: (0, j))],
        out_specs=pl.BlockSpec((tm, tn), lambda i, j: (i, j)),
        out_shape=jax.ShapeDtypeStruct((n, m), out_dtype),
        compiler_params=_cparams(("parallel", "parallel")),
        name="inproj",
    )(x, w)


def _gates_kernel(x_ref, wc_ref, wr_ref, pc_ref, pr_ref, cg_ref, rg_ref, *, tm):
    x = x_ref[...]
    r = lax.broadcasted_iota(I32, (tm, tm), 0)
    c = lax.broadcasted_iota(I32, (tm, tm), 1)
    same = (r // CHUNK) == (c // CHUNK)

    pc = jnp.dot(x, wc_ref[...], preferred_element_type=F32)
    lane = lax.broadcasted_iota(I32, (tm, LANES), 1)
    col = jnp.where(lane < DN_HEADS, _sigmoid(pc),
                    -pc_ref[0:1, :] * _softplus(pc + pc_ref[1:2, :]))
    lower = (same & (r >= c)).astype(F32)
    gcc = jnp.dot(lower, col, preferred_element_type=F32, precision=lax.Precision.HIGHEST)
    cg_ref[...] = jnp.where(lane < DN_HEADS, col, gcc)

    pr = lax.dot_general(wr_ref[...], x, (((1,), (1,)), ((), ())),
                         preferred_element_type=F32)
    sub = lax.broadcasted_iota(I32, (2 * DN_HEADS, tm), 0)
    row = jnp.where(sub < DN_HEADS, _sigmoid(pr),
                    -pr_ref[:, 0:1] * _softplus(pr + pr_ref[:, 1:2]))
    upper = (same & (r <= c)).astype(F32)
    gcr = jnp.dot(row, upper, preferred_element_type=F32, precision=lax.Precision.HIGHEST)
    rg_ref[...] = jnp.where(sub < DN_HEADS, row, gcr)


def _gates(xb, w_col, w_row, p_col, p_row, tm=256):
    n = xb.shape[0]
    return pl.pallas_call(
        functools.partial(_gates_kernel, tm=tm),
        grid=(n // tm,),
        in_specs=[pl.BlockSpec((tm, D_MODEL), lambda i: (i, 0)),
                  pl.BlockSpec((D_MODEL, LANES), lambda i: (0, 0)),
                  pl.BlockSpec((2 * DN_HEADS, D_MODEL), lambda i: (0, 0)),
                  pl.BlockSpec((2, LANES), lambda i: (0, 0)),
                  pl.BlockSpec((2 * DN_HEADS, LANES), lambda i: (0, 0))],
        out_specs=[pl.BlockSpec((tm, LANES), lambda i: (i, 0)),
                   pl.BlockSpec((2 * DN_HEADS, tm), lambda i: (0, i))],
        out_shape=[jax.ShapeDtypeStruct((n, LANES), F32),
                   jax.ShapeDtypeStruct((2 * DN_HEADS, n), F32)],
        compiler_params=_cparams(("parallel",)),
        name="dn_gates",
    )(xb, w_col, w_row, p_col, p_row)


def _dn_kernel(q_ref, k_ref, v_ref, z_ref, cg_ref, rg_ref, cw_ref, nw_ref, o_ref,
               xpad, s_ref, u_s, wq_s, qkk_s, vn_s, o_s, mask_s):
    tb, nch, hd = DN_TB, DN_TB // CHUNK, DN_HEAD_DIM
    t = pl.program_id(1)

    sb = DN_SB
    halves = tb // sb

    @pl.when(t == 0)
    def _():
        xpad[:, 0:SUBLANES, :] = jnp.zeros((3, SUBLANES, DN_WIDTH), F32)
        s_ref[...] = jnp.zeros_like(s_ref)
        r = lax.broadcasted_iota(I32, (sb, sb), 0)
        c = lax.broadcasted_iota(I32, (sb, sb), 1)
        same = (r // CHUNK) == (c // CHUNK)
        mask_s[0] = jnp.where(same & (r >= c), 0.0, NEG_INF)
        mask_s[1] = (same & (r > c)).astype(F32)
        mask_s[2] = ((r // 16) == (c // 16)).astype(F32)
        mask_s[3] = (r == c).astype(F32)

    @pl.when(t > 0)
    def _():
        xpad[:, 0:SUBLANES, :] = xpad[:, tb:tb + SUBLANES, :]

    xpad[0, SUBLANES:SUBLANES + tb, :] = q_ref[...].astype(F32)
    xpad[1, SUBLANES:SUBLANES + tb, :] = k_ref[...].astype(F32)
    xpad[2, SUBLANES:SUBLANES + tb, :] = v_ref[...].astype(F32)
    vn_s[...] = jnp.zeros_like(vn_s)

    lane_chunk = lax.broadcasted_iota(I32, (hd, sb), 1) // CHUNK
    ones = jnp.ones((hd, hd), BF16)

    def row_sum_sq(x):
        return jnp.dot((x * x).astype(BF16), ones, preferred_element_type=F32)

    def each(fn, *lists):
        return [fn(*args) for args in zip(*lists)]

    for h0 in range(0, DN_HEADS, DN_HEADS_PER_PASS):
        inst = [(h, half) for h in range(h0, h0 + DN_HEADS_PER_PASS) for half in range(halves)]

        def conv(idx, h, half):
            block = xpad[idx, half * sb:half * sb + sb + SUBLANES, h * hd:(h + 1) * hd]
            acc = None
            for i in range(CONV_WIDTH):
                col0 = idx * DN_WIDTH + h * hd
                back = CONV_WIDTH - 1 - i
                shifted = block if back == 0 else pltpu.roll(block, back, 0)
                term = shifted[SUBLANES:SUBLANES + sb, :] * cw_ref[i:i + 1, col0:col0 + hd]
                acc = term if acc is None else acc + term
            return _silu(acc)

        qh = [conv(0, h, half) for h, half in inst]
        kh = [conv(1, h, half) for h, half in inst]
        vh = [conv(2, h, half) for h, half in inst]
        qh = each(lambda x: x * lax.rsqrt(row_sum_sq(x) + L2_EPS) * (hd ** -0.5), qh)
        kh = each(lambda x: x * lax.rsqrt(row_sum_sq(x) + L2_EPS), kh)
        beta = [cg_ref[half * sb:(half + 1) * sb, h:h + 1] for h, half in inst]
        gc = [cg_ref[half * sb:(half + 1) * sb, DN_HEADS + h:DN_HEADS + h + 1] for h, half in inst]
        gcr = [rg_ref[DN_HEADS + h:DN_HEADS + h + 1, half * sb:(half + 1) * sb] for h, half in inst]
        dec = each(lambda a, b: jnp.exp((a - b) + mask_s[0]), gc, gcr)
        kb = each(lambda a, b: a * b, kh, beta)
        khb = each(lambda a: a.astype(BF16), kh)
        lmat = each(lambda a, b, d: _bdot_nt(a, b) * d * mask_s[1], kb, khb, dec)
        qk = each(lambda a, b, d: _bdot_nt(a, b) * d, qh, khb, dec)
        eye = mask_s[3]
        dmat = each(lambda a: a * mask_s[2], lmat)
        omat = each(lambda a, b: a - b, lmat, dmat)
        s1 = each(lambda a: eye - a, dmat)
        d2 = each(lambda a: _bdot(a, a), dmat)
        s2 = each(lambda a, b: a + _bdot(a, b), s1, d2)
        d4 = each(lambda a: _bdot(a, a), d2)
        s3 = each(lambda a, b: a + _bdot(a, b), s2, d4)
        d8 = each(lambda a: _bdot(a, a), d4)
        dinv = each(lambda a, b: a + _bdot(a, b), s3, d8)
        m1 = each(_bdot, dinv, omat)
        m2 = each(lambda a: _bdot(a, a), m1)
        im = each(lambda a: eye - a, m1)
        pm = each(lambda a, b: a + _bdot(a, b), im, m2)
        tinv = each(_bdot, pm, dinv)
        eg = each(jnp.exp, gc)
        rhs = each(lambda v, b, k, e: jnp.concatenate([v * b, k * e], axis=1), vh, beta, kb, eg)
        uw = each(_bdot, tinv, rhs)
        for i, (h, half) in enumerate(inst):
            rs = slice(half * sb, (half + 1) * sb)
            u_s[h, rs, :] = uw[i][:, :hd]
            w_b = uw[i][:, hd:].astype(BF16)
            qd_b = (qh[i] * eg[i]).astype(BF16)
            qk_b = qk[i].astype(BF16)
            gl = jnp.concatenate(
                [jnp.broadcast_to(gc[i][cc * CHUNK + CHUNK - 1:cc * CHUNK + CHUNK, :], (CHUNK, 1))
                 for cc in range(sb // CHUNK)], axis=0)
            kdt = (kh[i] * jnp.exp(gl - gc[i])).T
            for cc in range(sb // CHUNK):
                chunk = half * (sb // CHUNK) + cc
                cr = slice(cc * CHUNK, (cc + 1) * CHUNK)
                wq_s[h, chunk, 0:CHUNK, :] = w_b[cr, :]
                wq_s[h, chunk, CHUNK:2 * CHUNK, :] = qd_b[cr, :]
                qkk_s[h, chunk, 0:CHUNK, :] = qk_b[cr, :]
                qkk_s[h, chunk, CHUNK:CHUNK + hd, :] = jnp.where(lane_chunk == cc, kdt,
                                                                  0.0).astype(BF16)

    heads = list(range(DN_HEADS))

    def chunk_step(cc, carry):
        r0 = pl.multiple_of(cc * CHUNK, CHUNK)
        rows = pl.ds(r0, CHUNK)
        sub_rows = pl.ds(pl.multiple_of((cc // (sb // CHUNK)) * sb, sb), sb)
        s_old = [s_ref[h] for h in heads]
        s_b = each(lambda a: a.astype(BF16), s_old)
        by_state = [jnp.dot(wq_s[h, cc], s_b[h], preferred_element_type=F32) for h in heads]
        for h in heads:
            vn_s[h, rows, :] = (u_s[h, rows, :] - by_state[h][:CHUNK, :]).astype(BF16)
        vn_all = [vn_s[h, sub_rows, :] for h in heads]
        by_vnew = [jnp.dot(qkk_s[h, cc], vn_all[h], preferred_element_type=F32) for h in heads]
        for h in heads:
            g_last = cg_ref[pl.ds(r0 + CHUNK - 1, 1), DN_HEADS + h:DN_HEADS + h + 1]
            s_ref[h] = s_old[h] * jnp.exp(g_last) + by_vnew[h][CHUNK:, :]
            o_s[rows, h * hd:(h + 1) * hd] = by_state[h][CHUNK:, :] + by_vnew[h][:CHUNK, :]
        return carry

    lax.fori_loop(0, nch, chunk_step, 0)

    for h in range(DN_HEADS):
        sl = slice(h * hd, (h + 1) * hd)
        o = o_s[:, sl]
        o = o * lax.rsqrt(row_sum_sq(o) * (1.0 / hd) + RMS_EPS)
        z = z_ref[:, sl].astype(F32)
        o_ref[:, sl] = (o * nw_ref[...] * _silu(z)).astype(o_ref.dtype)


def _deltanet(proj, cg, rg, conv_w, norm_w, batch, seq):
    n = batch * seq
    nt = seq // DN_TB
    tb, hd, nch = DN_TB, DN_HEAD_DIM, DN_TB // CHUNK

    def col(j):
        return pl.BlockSpec((tb, DN_WIDTH), lambda b, t, j=j: (b * nt + t, j))

    return pl.pallas_call(
        _dn_kernel,
        grid=(batch, nt),
        in_specs=[col(COL_Q), col(COL_K), col(COL_V), col(COL_Z),
                  pl.BlockSpec((tb, LANES), lambda b, t: (b * nt + t, 0)),
                  pl.BlockSpec((2 * DN_HEADS, tb), lambda b, t: (0, b * nt + t)),
                  pl.BlockSpec((CONV_WIDTH, 3 * DN_WIDTH), lambda b, t: (0, 0)),
                  pl.BlockSpec((1, hd), lambda b, t: (0, 0))],
        out_specs=pl.BlockSpec((tb, DN_WIDTH), lambda b, t: (b * nt + t, 0)),
        out_shape=jax.ShapeDtypeStruct((n, DN_WIDTH), BF16),
        scratch_shapes=[
            pltpu.VMEM((3, tb + SUBLANES, DN_WIDTH), F32),
            pltpu.VMEM((DN_HEADS, hd, hd), F32),
            pltpu.VMEM((DN_HEADS, tb, hd), F32),
            pltpu.VMEM((DN_HEADS, nch, 2 * CHUNK, hd), BF16),
            pltpu.VMEM((DN_HEADS, nch, CHUNK + hd, DN_SB), BF16),
            pltpu.VMEM((DN_HEADS, tb, hd), BF16),
            pltpu.VMEM((tb, DN_WIDTH), F32),
            pltpu.VMEM((4, DN_SB, DN_SB), F32),
        ],
        compiler_params=_cparams(("parallel", "arbitrary")),
        name="deltanet",
    )(proj, proj, proj, proj, cg, rg, conv_w, norm_w)


def _attn_kernel(q_ref, k0_ref, k1_ref, k2_ref, v0_ref, v1_ref, v2_ref, bias_ref, o_ref):
    t = pl.program_id(1)
    q = q_ref[...]
    k = jnp.concatenate([k0_ref[...], k1_ref[...], k2_ref[...]], axis=0)
    v = jnp.concatenate([v0_ref[...], v1_ref[...], v2_ref[...]], axis=0)
    kidx = lax.broadcasted_iota(I32, (1, AT_KB), 1)
    key_valid = kidx >= (2 - t) * AT_QB
    lane_head = lax.broadcasted_iota(I32, (1, LANES), 1) // AT_HEAD_DIM
    scale = AT_HEAD_DIM ** -0.5
    def scores(head):
        p, mine = head // 2, lane_head == head % 2
        sl = slice(p * LANES, (p + 1) * LANES)
        qm = jnp.where(mine, q[:, sl] * scale, jnp.zeros((AT_QB, LANES), BF16))
        s = lax.dot_general(qm, k[:, sl], (((1,), (1,)), ((), ())), preferred_element_type=F32)
        return jnp.where(key_valid, s + bias_ref[head], NEG_INF)

    def weighted_values(head, s, out):
        p, mine = head // 2, lane_head == head % 2
        vp = v[:, p * LANES:(p + 1) * LANES]
        e = jnp.exp(s - jnp.max(s, axis=-1, keepdims=True))
        v_aug = jnp.where(mine, vp, jnp.ones_like(vp))
        pv = jnp.dot(e.astype(BF16), v_aug, preferred_element_type=F32)
        row_sum = pltpu.roll(pv, AT_HEAD_DIM, 1)
        return jnp.where(mine, pv / row_sum, out)

    outs = [jnp.zeros((AT_QB, LANES), F32)] * (AT_HEADS // 2)
    s_prev = scores(0)
    for head in range(AT_HEADS):
        s_next = scores(head + 1) if head + 1 < AT_HEADS else None
        outs[head // 2] = weighted_values(head, s_prev, outs[head // 2])
        s_prev = s_next
    for p in range(AT_HEADS // 2):
        o_ref[:, p * LANES:(p + 1) * LANES] = outs[p].astype(o_ref.dtype)


def _attention(proj, bias, batch, seq):
    n = batch * seq
    nt = seq // AT_QB

    def kv(col, back):
        return pl.BlockSpec((AT_QB, AT_WIDTH),
                            lambda b, t, col=col, back=back: (b * nt + jnp.maximum(t - back, 0), col))

    return pl.pallas_call(
        _attn_kernel,
        grid=(batch, nt),
        in_specs=[pl.BlockSpec((AT_QB, AT_WIDTH), lambda b, t: (b * nt + t, COL_AQ)),
                  kv(COL_AK, 2), kv(COL_AK, 1), kv(COL_AK, 0),
                  kv(COL_AV, 2), kv(COL_AV, 1), kv(COL_AV, 0),
                  pl.BlockSpec((AT_HEADS, AT_QB, AT_KB), lambda b, t: (0, 0, 0))],
        out_specs=pl.BlockSpec((AT_QB, AT_WIDTH), lambda b, t: (b * nt + t, 0)),
        out_shape=jax.ShapeDtypeStruct((n, AT_WIDTH), BF16),
        compiler_params=_cparams(("parallel", "parallel")),
        name="band_attention",
    )(proj, proj, proj, proj, proj, proj, proj, bias)


def _bias_kernel(f_ref, o_ref):
    width = f_ref.shape[1]
    qi = lax.broadcasted_iota(I32, (AT_QB, AT_KB), 0) // CHUNK
    ki = lax.broadcasted_iota(I32, (AT_QB, AT_KB), 1) // CHUNK
    band = (ki >= qi) & (ki <= qi + LEFT_CHUNKS)
    for h in range(AT_HEADS):
        base = jnp.broadcast_to(f_ref[h:h + 1, :], (AT_QB, width))
        rolled = pltpu.roll(base, width - (AT_QB - 1), 1, stride=1, stride_axis=0)
        o_ref[h] = jnp.where(band, rolled[:, :AT_KB], NEG_INF)


def _attention_bias(rel_table):
    table = rel_table.astype(F32)
    n_clipped = (AT_KB - 1) - REL_CLIP
    f = jnp.concatenate([jnp.broadcast_to(table[:, :1], (AT_HEADS, n_clipped)),
                         table[:, :AT_QB + REL_CLIP + 1]], axis=1)
    return pl.pallas_call(
        _bias_kernel,
        out_shape=jax.ShapeDtypeStruct((AT_HEADS, AT_QB, AT_KB), F32),
        name="attn_bias",
    )(f)


def _layer_norm(y, g, b):
    mu = jnp.mean(y, axis=-1, keepdims=True)
    d = y - mu
    var = jnp.mean(d * d, axis=-1, keepdims=True)
    return d * lax.rsqrt(var + LN_EPS) * g + b


def _merge_kernel(dn_ref, at_ref, ga_ref, gb_ref, x_ref, wa_ref, wb_ref, wo_ref, g_ref, b_ref,
                  x1_ref, x1t_ref, x1b_ref):
    ya = jnp.dot(dn_ref[...], wa_ref[...], preferred_element_type=F32)
    yb = jnp.dot(at_ref[...], wb_ref[...], preferred_element_type=F32)
    mix = _sigmoid(ga_ref[...].astype(F32)) * ya + _sigmoid(gb_ref[...].astype(F32)) * yb
    y = ALPHA * x_ref[...] + jnp.dot(mix.astype(BF16), wo_ref[...], preferred_element_type=F32)
    x1 = _layer_norm(y, g_ref[...], b_ref[...])
    x1_ref[...] = x1
    x1b_ref[...] = x1.astype(BF16)
    tm = x1.shape[0]
    for j in range(ROW_TILES):
        x1t_ref[pl.ds(j, tm, stride=ROW_TILES), :] = x1[:, j * LANES:(j + 1) * LANES]


def _merge(dn, at, proj, x, wa, wb, wo, g, b, tm=512):
    n = x.shape[0]
    return pl.pallas_call(
        _merge_kernel,
        grid=(n // tm,),
        in_specs=[pl.BlockSpec((tm, DN_WIDTH), lambda i: (i, 0)),
                  pl.BlockSpec((tm, AT_WIDTH), lambda i: (i, 0)),
                  pl.BlockSpec((tm, D_MODEL), lambda i: (i, COL_GA)),
                  pl.BlockSpec((tm, D_MODEL), lambda i: (i, COL_GB)),
                  pl.BlockSpec((tm, D_MODEL), lambda i: (i, 0)),
                  pl.BlockSpec((DN_WIDTH, D_MODEL), lambda i: (0, 0)),
                  pl.BlockSpec((AT_WIDTH, D_MODEL), lambda i: (0, 0)),
                  pl.BlockSpec((D_MODEL, D_MODEL), lambda i: (0, 0)),
                  pl.BlockSpec((1, D_MODEL), lambda i: (0, 0)),
                  pl.BlockSpec((1, D_MODEL), lambda i: (0, 0))],
        out_specs=[pl.BlockSpec((tm, D_MODEL), lambda i: (i, 0)),
                   pl.BlockSpec((tm * ROW_TILES, LANES), lambda i: (i, 0)),
                   pl.BlockSpec((tm, D_MODEL), lambda i: (i, 0))],
        out_shape=[jax.ShapeDtypeStruct((n, D_MODEL), F32),
                   jax.ShapeDtypeStruct((n * ROW_TILES, LANES), F32),
                   jax.ShapeDtypeStruct((n, D_MODEL), BF16)],
        compiler_params=_cparams(("parallel",)),
        name="merge_ln1",
    )(dn, at, proj, proj, x, wa, wb, wo, g, b)


def _router_kernel(x_ref, rw_ref, rb_ref, dest_ref, wt_ref, seg_ref):
    gt, ne = GROUP, N_EXPERTS
    logits = lax.dot_general(rw_ref[...], x_ref[...], (((1,), (1,)), ((), ())),
                             preferred_element_type=F32, precision=lax.Precision.HIGHEST)
    scores = _sigmoid(logits)
    sel = scores + rb_ref[:, 0:1]
    eidx = lax.broadcasted_iota(I32, (ne, gt), 0)
    picked = jnp.zeros((ne, gt), jnp.bool_)
    hits, vals = [], []
    for _ in range(TOP_K):
        m = jnp.max(sel, axis=0, keepdims=True)
        idx = jnp.min(jnp.where(sel == m, eidx, ne), axis=0, keepdims=True)
        hit = eidx == idx
        hits.append(hit)
        vals.append(jnp.sum(jnp.where(hit, scores, 0.0), axis=0, keepdims=True))
        sel = jnp.where(hit, -jnp.inf, sel)
        picked = picked | hit
    total = vals[0]
    for v in vals[1:]:
        total = total + v

    onehot = picked.astype(BF16)
    tr = lax.broadcasted_iota(I32, (gt, gt), 0)
    tc = lax.broadcasted_iota(I32, (gt, gt), 1)
    rank = jnp.dot(onehot, (tr < tc).astype(BF16), preferred_element_type=F32)
    count = jnp.sum(picked.astype(F32), axis=1, keepdims=True)
    tiles = jnp.floor((count + (SUBLANES - 1)) * (1.0 / SUBLANES))
    er = lax.broadcasted_iota(I32, (ne, ne), 0)
    ec = lax.broadcasted_iota(I32, (ne, ne), 1)
    start = SUBLANES * jnp.dot((ec < er).astype(BF16),
                               jnp.broadcast_to(tiles, (ne, LANES)).astype(BF16),
                               preferred_element_type=F32)
    slot = start[:, 0:1] + rank

    sub = lax.broadcasted_iota(I32, (SUBLANES, gt), 0)
    dest = jnp.zeros((SUBLANES, gt), F32)
    wts = jnp.zeros((SUBLANES, gt), F32)
    for kk in range(TOP_K):
        d = jnp.sum(jnp.where(hits[kk], slot, 0.0), axis=0, keepdims=True)
        dest = jnp.where(sub == kk, d, dest)
        wts = jnp.where(sub == kk, vals[kk] / total * ROUTED_SCALE, wts)
    dest_ref[0] = dest.astype(I32) * ROW_TILES
    wt_ref[0] = wts
    lane = lax.broadcasted_iota(I32, (ne, LANES), 1)
    seg_ref[0] = jnp.where(lane == 0, start, jnp.broadcast_to(count, (ne, LANES))).astype(I32)


def _router(x1, rw_t, rb):
    n = x1.shape[0]
    ng = n // GROUP
    return pl.pallas_call(
        _router_kernel,
        grid=(ng,),
        in_specs=[pl.BlockSpec((GROUP, D_MODEL), lambda g: (g, 0)),
                  pl.BlockSpec((N_EXPERTS, D_MODEL), lambda g: (0, 0)),
                  pl.BlockSpec((N_EXPERTS, LANES), lambda g: (0, 0))],
        out_specs=[pl.BlockSpec((1, SUBLANES, GROUP), lambda g: (g, 0, 0)),
                   pl.BlockSpec((1, SUBLANES, GROUP), lambda g: (g, 0, 0)),
                   pl.BlockSpec((1, N_EXPERTS, LANES), lambda g: (g, 0, 0))],
        out_shape=[jax.ShapeDtypeStruct((ng, SUBLANES, GROUP), I32),
                   jax.ShapeDtypeStruct((ng, SUBLANES, GROUP), F32),
                   jax.ShapeDtypeStruct((ng, N_EXPERTS, LANES), I32)],
        compiler_params=_cparams(("parallel",)),
        name="router",
    )(x1, rw_t, rb)


def _moe_kernel(start_ref, count_ref, xt_ref, dest_hbm, wt_hbm, wgu_ref, wd_ref, out_ref,
                buf, dest_s, wt_s, sem):
    g = pl.program_id(0)
    e = pl.program_id(1)
    ne, gt = N_EXPERTS, GROUP

    def routing_copies():
        return (pltpu.make_async_copy(dest_hbm.at[g], dest_s, sem.at[0]),
                pltpu.make_async_copy(wt_hbm.at[g], wt_s, sem.at[1]))

    def tile_of(row8):
        return pl.ds(pl.multiple_of(row8, ROW_TILES), ROW_TILES)

    @pl.when(e == 0)
    def _():
        for cp in routing_copies():
            cp.start()
        zeros = jnp.zeros((SUBLANES * ROW_TILES, LANES), F32)

        def zero_pad(ee, carry):
            s = start_ref[g * ne + ee]
            cnt = count_ref[g * ne + ee]
            z0 = pl.multiple_of((s + (cnt // SUBLANES) * SUBLANES) * ROW_TILES, SUBLANES * ROW_TILES)
            buf[pl.ds(z0, SUBLANES * ROW_TILES), :] = zeros
            return carry

        lax.fori_loop(0, ne, zero_pad, 0)
        last = g * ne + ne - 1
        end = start_ref[last] + (count_ref[last] + SUBLANES - 1) // SUBLANES * SUBLANES
        end = pl.multiple_of(end * ROW_TILES, SUBLANES * ROW_TILES)
        buf[pl.ds(end, FFN_BM * ROW_TILES), :] = jnp.zeros((FFN_BM * ROW_TILES, LANES), F32)
        for cp in routing_copies():
            cp.wait()

        def scatter(t0, carry):
            for u in range(ROW_LOOP_UNROLL):
                tok = t0 * ROW_LOOP_UNROLL + u
                row = xt_ref[tile_of(tok * ROW_TILES), :]
                for kk in range(TOP_K):
                    buf[tile_of(dest_s[kk * gt + tok]), :] = row
            return carry

        lax.fori_loop(0, gt // ROW_LOOP_UNROLL, scatter, 0)

    experts = list(range(MOE_EPS))
    seg = [(start_ref[g * ne + e * MOE_EPS + k], count_ref[g * ne + e * MOE_EPS + k])
           for k in experts]
    n_blocks = [(cnt + FFN_BM - 1) // FFN_BM for _, cnt in seg]
    last_block_row = GROUP_ROWS - FFN_BM

    def ffn_block(i, carry):
        r0 = [pl.multiple_of(jnp.minimum(s + i * FFN_BM, last_block_row), SUBLANES) for s, _ in seg]
        base = [pl.multiple_of(r * ROW_TILES, SUBLANES * ROW_TILES) for r in r0]
        xin = [jnp.concatenate([buf[pl.ds(base[k] + j, FFN_BM, stride=ROW_TILES), :]
                                for j in range(ROW_TILES)], axis=1) for k in experts]
        gu = [jnp.dot(xin[k].astype(BF16), wgu_ref[k], preferred_element_type=F32)
              for k in experts]
        hidden = [(_silu(gu[k][:, :EXPERT_DIM]) * gu[k][:, EXPERT_DIM:]).astype(BF16)
                  for k in experts]
        y = [jnp.dot(hidden[k], wd_ref[k], preferred_element_type=F32) for k in experts]
        for k in experts:
            s, cnt = seg[k]
            keep = (r0[k] + lax.broadcasted_iota(I32, (FFN_BM, 1), 0)) < s + cnt
            for j in range(ROW_TILES):
                buf[pl.ds(base[k] + j, FFN_BM, stride=ROW_TILES), :] = jnp.where(
                    keep, y[k][:, j * LANES:(j + 1) * LANES], xin[k][:, j * LANES:(j + 1) * LANES])
        return carry

    total_blocks = n_blocks[0]
    for nb in n_blocks[1:]:
        total_blocks = jnp.maximum(total_blocks, nb)
    lax.fori_loop(0, total_blocks, ffn_block, 0)

    @pl.when(e == pl.num_programs(1) - 1)
    def _():
        def combine(t0, carry):
            for u in range(ROW_LOOP_UNROLL):
                tok = t0 * ROW_LOOP_UNROLL + u
                acc = buf[tile_of(dest_s[tok]), :] * wt_s[tok]
                for kk in range(1, TOP_K):
                    acc = acc + buf[tile_of(dest_s[kk * gt + tok]), :] * wt_s[kk * gt + tok]
                out_ref[tile_of(tok * ROW_TILES), :] = acc
            return carry

        lax.fori_loop(0, gt // ROW_LOOP_UNROLL, combine, 0)


def _moe(x1t, dest, wts, starts, counts, wgu, wd):
    n = x1t.shape[0] // ROW_TILES
    ng = n // GROUP
    grid_spec = pltpu.PrefetchScalarGridSpec(
        num_scalar_prefetch=2,
        grid=(ng, N_EXPERTS // MOE_EPS),
        in_specs=[pl.BlockSpec((GROUP * ROW_TILES, LANES), lambda g, e, *_: (g, 0)),
                  pl.BlockSpec(memory_space=pl.ANY),
                  pl.BlockSpec(memory_space=pl.ANY),
                  pl.BlockSpec((MOE_EPS, D_MODEL, 2 * EXPERT_DIM), lambda g, e, *_: (e, 0, 0)),
                  pl.BlockSpec((MOE_EPS, EXPERT_DIM, D_MODEL), lambda g, e, *_: (e, 0, 0))],
        out_specs=pl.BlockSpec((GROUP * ROW_TILES, LANES), lambda g, e, *_: (g, 0)),
        scratch_shapes=[pltpu.VMEM((GROUP_ROWS * ROW_TILES, LANES), F32),
                        pltpu.SMEM((SUBLANES * GROUP,), I32),
                        pltpu.SMEM((SUBLANES * GROUP,), F32),
                        pltpu.SemaphoreType.DMA((2,))],
    )
    return pl.pallas_call(
        _moe_kernel,
        grid_spec=grid_spec,
        out_shape=jax.ShapeDtypeStruct((n * ROW_TILES, LANES), F32),
        compiler_params=_cparams(("arbitrary", "arbitrary")),
        name="routed_experts",
    )(starts, counts, x1t, dest, wts, wgu, wd)


def _final_kernel(x1_ref, x1b_ref, rt_ref, wsg_ref, wsd_ref, g_ref, b_ref, o_ref, ob_ref):
    gus = jnp.dot(x1b_ref[...], wsg_ref[...], preferred_element_type=F32)
    hidden = _silu(gus[:, :SHARED_DIM]) * gus[:, SHARED_DIM:]
    shared = jnp.dot(hidden.astype(BF16), wsd_ref[...], preferred_element_type=F32)
    tm = x1_ref.shape[0]
    routed = jnp.concatenate([rt_ref[pl.ds(j, tm, stride=ROW_TILES), :]
                              for j in range(ROW_TILES)], axis=1)
    y = ALPHA * x1_ref[...] + (routed + shared)
    x2 = _layer_norm(y, g_ref[...], b_ref[...])
    o_ref[...] = x2
    ob_ref[...] = x2.astype(BF16)


def _final(x1, x1b, routed_t, wsg, wsd, g, b, tm=512):
    n = x1.shape[0]
    return pl.pallas_call(
        _final_kernel,
        grid=(n // tm,),
        in_specs=[pl.BlockSpec((tm, D_MODEL), lambda i: (i, 0)),
                  pl.BlockSpec((tm, D_MODEL), lambda i: (i, 0)),
                  pl.BlockSpec((tm * ROW_TILES, LANES), lambda i: (i, 0)),
                  pl.BlockSpec((D_MODEL, 2 * SHARED_DIM), lambda i: (0, 0)),
                  pl.BlockSpec((SHARED_DIM, D_MODEL), lambda i: (0, 0)),
                  pl.BlockSpec((1, D_MODEL), lambda i: (0, 0)),
                  pl.BlockSpec((1, D_MODEL), lambda i: (0, 0))],
        out_specs=[pl.BlockSpec((tm, D_MODEL), lambda i: (i, 0)),
                   pl.BlockSpec((tm, D_MODEL), lambda i: (i, 0))],
        out_shape=[jax.ShapeDtypeStruct((n, D_MODEL), F32),
                   jax.ShapeDtypeStruct((n, D_MODEL), BF16)],
        compiler_params=_cparams(("parallel",)),
        name="shared_ln2",
    )(x1, x1b, routed_t, wsg, wsd, g, b)


def _layer(x, xb, batch, seq, w_in, conv_w, a_log, dt_bias, dn_norm_w, rel_table, w_branch_a,
           w_branch_b, w_out, ln1_g, ln1_b, router_w, router_bias, w_gate_up, w_down,
           ws_gate_up, ws_down, ln2_g, ln2_b):
    c_small = 4 * DN_WIDTH
    c_att = c_small + 2 * DN_HEADS
    c_gate = c_att + 3 * AT_WIDTH
    w_big = jnp.concatenate([w_in[:, :c_small], w_in[:, c_gate:], w_in[:, c_att:c_gate]],
                            axis=1).astype(BF16)
    w_small = w_in[:, c_small:c_att]
    w_col = jnp.pad(w_small, ((0, 0), (0, LANES - 2 * DN_HEADS))).astype(BF16)
    w_row = w_small.T.astype(BF16)
    decay_rate = jnp.exp(a_log.astype(F32))
    zeros8 = jnp.zeros((DN_HEADS,), F32)
    p_col = jnp.pad(jnp.stack([jnp.concatenate([zeros8, decay_rate]),
                               jnp.concatenate([zeros8, dt_bias.astype(F32)])]),
                    ((0, 0), (0, LANES - 2 * DN_HEADS)))
    p_row = jnp.pad(p_col[:, :2 * DN_HEADS].T, ((0, 0), (0, LANES - 2)))

    proj = _matmul(xb, w_big, BF16, tm=1024, tn=1280)
    cg, rg = _gates(xb, w_col, w_row, p_col, p_row)
    dn = _deltanet(proj, cg, rg, conv_w.astype(F32), dn_norm_w.astype(F32).reshape(1, DN_HEAD_DIM),
                   batch, seq)
    at = _attention(proj, _attention_bias(rel_table), batch, seq)
    x1, x1t, x1b = _merge(dn, at, proj, x, w_branch_a.astype(BF16), w_branch_b.astype(BF16),
                          w_out.astype(BF16), ln1_g.reshape(1, -1), ln1_b.reshape(1, -1))
    dest, wts, seg = _router(x1, router_w.T.astype(F32),
                             jnp.broadcast_to(router_bias.astype(F32)[:, None], (N_EXPERTS, LANES)))
    ng = dest.shape[0]
    routed_t = _moe(x1t, dest.reshape(ng, SUBLANES * GROUP), wts.reshape(ng, SUBLANES * GROUP),
                    seg[:, :, 0].reshape(-1), seg[:, :, 1].reshape(-1),
                    w_gate_up.astype(BF16), w_down.astype(BF16))
    return _final(x1, x1b, routed_t, ws_gate_up.astype(BF16), ws_down.astype(BF16),
                  ln2_g.reshape(1, -1), ln2_b.reshape(1, -1))


def kernel(x, w_in, conv_w, a_log, dt_bias, dn_norm_w, rel_table, w_branch_a, w_branch_b, w_out,
           ln1_g, ln1_b, router_w, router_bias, expert_w_gate_up, expert_w_down,
           shared_w_gate_up, shared_w_down, ln2_g, ln2_b):
    batch, seq, d = x.shape
    assert d == D_MODEL and seq % DN_TB == 0 and seq % AT_QB == 0 and (batch * seq) % GROUP == 0
    xf = x.reshape(batch * seq, d).astype(F32)
    xb = xf.astype(BF16)
    for l in range(DEPTH):
        xf, xb = _layer(xf, xb, batch, seq, w_in[l], conv_w[l], a_log[l], dt_bias[l], dn_norm_w[l],
                        rel_table[l], w_branch_a[l], w_branch_b[l], w_out[l], ln1_g[l], ln1_b[l],
                        router_w[l], router_bias[l], expert_w_gate_up[l], expert_w_down[l],
                        shared_w_gate_up[l], shared_w_down[l], ln2_g[l], ln2_b[l])
    return xf.reshape(batch, seq, d).astype(x.dtype)
```

```python
import functools

import jax
import jax.numpy as jnp
from jax import lax
from jax.experimental import pallas as pl
from jax.experimental.pallas import tpu as pltpu

F32 = jnp.float32
BF16 = jnp.bfloat16
I32 = jnp.int32

D_MODEL = 1024
DEPTH = 2
CHUNK = 64
DN_HEADS = 8
DN_HEAD_DIM = 128
DN_WIDTH = DN_HEADS * DN_HEAD_DIM
CONV_WIDTH = 4
AT_HEADS = 8
AT_HEAD_DIM = 64
AT_WIDTH = AT_HEADS * AT_HEAD_DIM
LEFT_CHUNKS = 8
REL_CLIP = 256
N_EXPERTS = 64
TOP_K = 6
EXPERT_DIM = 256
SHARED_DIM = 256
ROUTED_SCALE = 2.5
ALPHA = (2 * DEPTH) ** 0.25
LN_EPS = 1e-5
RMS_EPS = 1e-6
L2_EPS = 1e-6
NEG_INF = -1e30

LANES = 128
SUBLANES = 8
ROW_TILES = D_MODEL // LANES

DN_TB = 256
DN_SB = 128
DN_HEADS_PER_PASS = 8
AT_QB = 256
AT_KB = 3 * AT_QB
GROUP = 1024
FFN_BM = 128
MOE_EPS = 4
ROW_LOOP_UNROLL = 4
GROUP_ROWS = GROUP * TOP_K + N_EXPERTS * (SUBLANES - 1) + FFN_BM
GROUP_ROWS = -(-GROUP_ROWS // SUBLANES) * SUBLANES
VMEM_LIMIT = 56 * 1024 * 1024

COL_Q, COL_K, COL_V, COL_Z, COL_GA, COL_GB = 0, 1, 2, 3, 4, 5
COL_AQ, COL_AK, COL_AV = 12, 13, 14
BIG_COLS = 4 * DN_WIDTH + 2 * D_MODEL + 3 * AT_WIDTH


def _sigmoid(x):
    return 1.0 / (1.0 + jnp.exp(-x))


def _silu(x):
    return x * _sigmoid(x)


def _softplus(x):
    return jnp.maximum(x, 0.0) + jnp.log1p(jnp.exp(-jnp.abs(x)))


def _bdot(a, b):
    return jnp.dot(a.astype(BF16), b.astype(BF16), preferred_element_type=F32)


def _bdot_nt(a, b):
    return lax.dot_general(a.astype(BF16), b.astype(BF16), (((1,), (1,)), ((), ())),
                           preferred_element_type=F32)


def _cparams(sem):
    return pltpu.CompilerParams(dimension_semantics=sem, vmem_limit_bytes=VMEM_LIMIT)


def _mm_kernel(x_ref, w_ref, o_ref):
    o_ref[...] = jnp.dot(x_ref[...].astype(BF16), w_ref[...],
                         preferred_element_type=F32).astype(o_ref.dtype)


def _matmul(x, w_all, layer, out_dtype, tm, tn):
    n, k = x.shape
    m = w_all.shape[1]
    return pl.pallas_call(
        _mm_kernel,
        grid=(n // tm, m // tn),
        in_specs=[pl.BlockSpec((tm, k), lambda i, j: (i, 0)),
                  pl.BlockSpec((k, tn), lambda i, j: (layer, j))],
        out_specs=pl.BlockSpec((tm, tn), lambda i, j: (i, j)),
        out_shape=jax.ShapeDtypeStruct((n, m), out_dtype),
        compiler_params=_cparams(("parallel", "parallel")),
        name="inproj",
    )(x, w_all)


def _gates_kernel(x_ref, wc_ref, wr_ref, pc_ref, pr_ref, cg_ref, rg_ref, *, tm):
    x = x_ref[...].astype(BF16)
    r = lax.broadcasted_iota(I32, (tm, tm), 0)
    c = lax.broadcasted_iota(I32, (tm, tm), 1)
    same = (r // CHUNK) == (c // CHUNK)

    pc = jnp.dot(x, wc_ref[...], preferred_element_type=F32)
    lane = lax.broadcasted_iota(I32, (tm, LANES), 1)
    col = jnp.where(lane < DN_HEADS, _sigmoid(pc),
                    -pc_ref[0:1, :] * _softplus(pc + pc_ref[1:2, :]))
    lower = (same & (r >= c)).astype(F32)
    gcc = jnp.dot(lower, col, preferred_element_type=F32, precision=lax.Precision.HIGHEST)
    cg_ref[...] = jnp.where(lane < DN_HEADS, col, gcc)

    pr = lax.dot_general(wr_ref[...], x, (((1,), (1,)), ((), ())),
                         preferred_element_type=F32)
    sub = lax.broadcasted_iota(I32, (2 * DN_HEADS, tm), 0)
    row = jnp.where(sub < DN_HEADS, _sigmoid(pr),
                    -pr_ref[:, 0:1] * _softplus(pr + pr_ref[:, 1:2]))
    upper = (same & (r <= c)).astype(F32)
    gcr = jnp.dot(row, upper, preferred_element_type=F32, precision=lax.Precision.HIGHEST)
    rg_ref[...] = jnp.where(sub < DN_HEADS, row, gcr)


def _gates(xb, w_col, w_row, p_col, p_row, tm=256):
    n = xb.shape[0]
    return pl.pallas_call(
        functools.partial(_gates_kernel, tm=tm),
        grid=(n // tm,),
        in_specs=[pl.BlockSpec((tm, D_MODEL), lambda i: (i, 0)),
                  pl.BlockSpec((D_MODEL, LANES), lambda i: (0, 0)),
                  pl.BlockSpec((2 * DN_HEADS, D_MODEL), lambda i: (0, 0)),
                  pl.BlockSpec((2, LANES), lambda i: (0, 0)),
                  pl.BlockSpec((2 * DN_HEADS, LANES), lambda i: (0, 0))],
        out_specs=[pl.BlockSpec((tm, LANES), lambda i: (i, 0)),
                   pl.BlockSpec((2 * DN_HEADS, tm), lambda i: (0, i))],
        out_shape=[jax.ShapeDtypeStruct((n, LANES), F32),
                   jax.ShapeDtypeStruct((2 * DN_HEADS, n), F32)],
        compiler_params=_cparams(("parallel",)),
        name="dn_gates",
    )(xb, w_col, w_row, p_col, p_row)


def _dn_kernel(q_ref, k_ref, v_ref, z_ref, cg_ref, rg_ref, cw_ref, nw_ref, o_ref,
               xpad, s_ref, u_s, wq_s, qkk_s, vn_s, o_s, mask_s):
    tb, nch, hd = DN_TB, DN_TB // CHUNK, DN_HEAD_DIM
    t = pl.program_id(1)

    sb = DN_SB
    halves = tb // sb

    @pl.when(t == 0)
    def _():
        xpad[:, 0:SUBLANES, :] = jnp.zeros((3, SUBLANES, DN_WIDTH), F32)
        s_ref[...] = jnp.zeros_like(s_ref)
        r = lax.broadcasted_iota(I32, (sb, sb), 0)
        c = lax.broadcasted_iota(I32, (sb, sb), 1)
        same = (r // CHUNK) == (c // CHUNK)
        mask_s[0] = jnp.where(same & (r >= c), 0.0, NEG_INF)
        mask_s[1] = (same & (r > c)).astype(F32)
        mask_s[2] = ((r // 16) == (c // 16)).astype(F32)
        mask_s[3] = (r == c).astype(F32)

    @pl.when(t > 0)
    def _():
        xpad[:, 0:SUBLANES, :] = xpad[:, tb:tb + SUBLANES, :]

    xpad[0, SUBLANES:SUBLANES + tb, :] = q_ref[...].astype(F32)
    xpad[1, SUBLANES:SUBLANES + tb, :] = k_ref[...].astype(F32)
    xpad[2, SUBLANES:SUBLANES + tb, :] = v_ref[...].astype(F32)
    vn_s[...] = jnp.zeros_like(vn_s)

    lane_chunk = lax.broadcasted_iota(I32, (hd, sb), 1) // CHUNK
    ones = jnp.ones((hd, hd), BF16)

    def row_sum_sq(x):
        return jnp.dot((x * x).astype(BF16), ones, preferred_element_type=F32)

    def each(fn, *lists):
        return [fn(*args) for args in zip(*lists)]

    for h0 in range(0, DN_HEADS, DN_HEADS_PER_PASS):
        inst = [(h, half) for h in range(h0, h0 + DN_HEADS_PER_PASS) for half in range(halves)]

        def conv(idx, h, half):
            block = xpad[idx, half * sb:half * sb + sb + SUBLANES, h * hd:(h + 1) * hd]
            acc = None
            for i in range(CONV_WIDTH):
                col0 = idx * DN_WIDTH + h * hd
                back = CONV_WIDTH - 1 - i
                shifted = block if back == 0 else pltpu.roll(block, back, 0)
                term = shifted[SUBLANES:SUBLANES + sb, :] * cw_ref[i:i + 1, col0:col0 + hd]
                acc = term if acc is None else acc + term
            return _silu(acc)

        qh = [conv(0, h, half) for h, half in inst]
        kh = [conv(1, h, half) for h, half in inst]
        vh = [conv(2, h, half) for h, half in inst]
        qh = each(lambda x: x * lax.rsqrt(row_sum_sq(x) + L2_EPS) * (hd ** -0.5), qh)
        kh = each(lambda x: x * lax.rsqrt(row_sum_sq(x) + L2_EPS), kh)
        beta = [cg_ref[half * sb:(half + 1) * sb, h:h + 1] for h, half in inst]
        gc = [cg_ref[half * sb:(half + 1) * sb, DN_HEADS + h:DN_HEADS + h + 1] for h, half in inst]
        gcr = [rg_ref[DN_HEADS + h:DN_HEADS + h + 1, half * sb:(half + 1) * sb] for h, half in inst]
        dec = each(lambda a, b: jnp.exp((a - b) + mask_s[0]), gc, gcr)
        kb = each(lambda a, b: a * b, kh, beta)
        khb = each(lambda a: a.astype(BF16), kh)
        lmat = each(lambda a, b, d: _bdot_nt(a, b) * d * mask_s[1], kb, khb, dec)
        qk = each(lambda a, b, d: _bdot_nt(a, b) * d, qh, khb, dec)
        eye = mask_s[3]
        dmat = each(lambda a: a * mask_s[2], lmat)
        omat = each(lambda a, b: a - b, lmat, dmat)
        s1 = each(lambda a: eye - a, dmat)
        d2 = each(lambda a: _bdot(a, a), dmat)
        s2 = each(lambda a, b: a + _bdot(a, b), s1, d2)
        d4 = each(lambda a: _bdot(a, a), d2)
        s3 = each(lambda a, b: a + _bdot(a, b), s2, d4)
        d8 = each(lambda a: _bdot(a, a), d4)
        dinv = each(lambda a, b: a + _bdot(a, b), s3, d8)
        m1 = each(_bdot, dinv, omat)
        m2 = each(lambda a: _bdot(a, a), m1)
        im = each(lambda a: eye - a, m1)
        pm = each(lambda a, b: a + _bdot(a, b), im, m2)
        tinv = each(_bdot, pm, dinv)
        eg = each(jnp.exp, gc)
        rhs = each(lambda v, b, k, e: jnp.concatenate([v * b, k * e], axis=1), vh, beta, kb, eg)
        uw = each(_bdot, tinv, rhs)
        for i, (h, half) in enumerate(inst):
            rs = slice(half * sb, (half + 1) * sb)
            u_s[h, rs, :] = uw[i][:, :hd]
            w_b = uw[i][:, hd:].astype(BF16)
            qd_b = (qh[i] * eg[i]).astype(BF16)
            qk_b = qk[i].astype(BF16)
            gl = jnp.concatenate(
                [jnp.broadcast_to(gc[i][cc * CHUNK + CHUNK - 1:cc * CHUNK + CHUNK, :], (CHUNK, 1))
                 for cc in range(sb // CHUNK)], axis=0)
            kdt = (kh[i] * jnp.exp(gl - gc[i])).T
            for cc in range(sb // CHUNK):
                chunk = half * (sb // CHUNK) + cc
                cr = slice(cc * CHUNK, (cc + 1) * CHUNK)
                wq_s[h, chunk, 0:CHUNK, :] = w_b[cr, :]
                wq_s[h, chunk, CHUNK:2 * CHUNK, :] = qd_b[cr, :]
                qkk_s[h, chunk, 0:CHUNK, :] = qk_b[cr, :]
                qkk_s[h, chunk, CHUNK:CHUNK + hd, :] = jnp.where(lane_chunk == cc, kdt,
                                                                  0.0).astype(BF16)

    heads = list(range(DN_HEADS))

    def chunk_step(cc, carry):
        r0 = pl.multiple_of(cc * CHUNK, CHUNK)
        rows = pl.ds(r0, CHUNK)
        sub_rows = pl.ds(pl.multiple_of((cc // (sb // CHUNK)) * sb, sb), sb)
        s_old = [s_ref[h] for h in heads]
        s_b = each(lambda a: a.astype(BF16), s_old)
        by_state = [jnp.dot(wq_s[h, cc], s_b[h], preferred_element_type=F32) for h in heads]
        for h in heads:
            vn_s[h, rows, :] = (u_s[h, rows, :] - by_state[h][:CHUNK, :]).astype(BF16)
        vn_all = [vn_s[h, sub_rows, :] for h in heads]
        by_vnew = [jnp.dot(qkk_s[h, cc], vn_all[h], preferred_element_type=F32) for h in heads]
        for h in heads:
            g_last = cg_ref[pl.ds(r0 + CHUNK - 1, 1), DN_HEADS + h:DN_HEADS + h + 1]
            s_ref[h] = s_old[h] * jnp.exp(g_last) + by_vnew[h][CHUNK:, :]
            o_s[rows, h * hd:(h + 1) * hd] = by_state[h][CHUNK:, :] + by_vnew[h][:CHUNK, :]
        return carry

    lax.fori_loop(0, nch, chunk_step, 0)

    for h in range(DN_HEADS):
        sl = slice(h * hd, (h + 1) * hd)
        o = o_s[:, sl]
        o = o * lax.rsqrt(row_sum_sq(o) * (1.0 / hd) + RMS_EPS)
        z = z_ref[:, sl].astype(F32)
        o_ref[:, sl] = (o * nw_ref[...] * _silu(z)).astype(o_ref.dtype)


def _deltanet(proj, cg, rg, conv_w, norm_w, batch, seq):
    n = batch * seq
    nt = seq // DN_TB
    tb, hd, nch = DN_TB, DN_HEAD_DIM, DN_TB // CHUNK

    def col(j):
        return pl.BlockSpec((tb, DN_WIDTH), lambda b, t, j=j: (b * nt + t, j))

    return pl.pallas_call(
        _dn_kernel,
        grid=(batch, nt),
        in_specs=[col(COL_Q), col(COL_K), col(COL_V), col(COL_Z),
                  pl.BlockSpec((tb, LANES), lambda b, t: (b * nt + t, 0)),
                  pl.BlockSpec((2 * DN_HEADS, tb), lambda b, t: (0, b * nt + t)),
                  pl.BlockSpec((CONV_WIDTH, 3 * DN_WIDTH), lambda b, t: (0, 0)),
                  pl.BlockSpec((1, hd), lambda b, t: (0, 0))],
        out_specs=pl.BlockSpec((tb, DN_WIDTH), lambda b, t: (b * nt + t, 0)),
        out_shape=jax.ShapeDtypeStruct((n, DN_WIDTH), BF16),
        scratch_shapes=[
            pltpu.VMEM((3, tb + SUBLANES, DN_WIDTH), F32),
            pltpu.VMEM((DN_HEADS, hd, hd), F32),
            pltpu.VMEM((DN_HEADS, tb, hd), F32),
            pltpu.VMEM((DN_HEADS, nch, 2 * CHUNK, hd), BF16),
            pltpu.VMEM((DN_HEADS, nch, CHUNK + hd, DN_SB), BF16),
            pltpu.VMEM((DN_HEADS, tb, hd), BF16),
            pltpu.VMEM((tb, DN_WIDTH), F32),
            pltpu.VMEM((4, DN_SB, DN_SB), F32),
        ],
        compiler_params=_cparams(("parallel", "arbitrary")),
        name="deltanet",
    )(proj, proj, proj, proj, cg, rg, conv_w, norm_w)


def _attn_kernel(q_ref, k0_ref, k1_ref, k2_ref, v0_ref, v1_ref, v2_ref, bias_ref, o_ref):
    t = pl.program_id(1)
    q = q_ref[...]
    k = jnp.concatenate([k0_ref[...], k1_ref[...], k2_ref[...]], axis=0)
    v = jnp.concatenate([v0_ref[...], v1_ref[...], v2_ref[...]], axis=0)
    kidx = lax.broadcasted_iota(I32, (1, AT_KB), 1)
    key_valid = kidx >= (2 - t) * AT_QB
    lane_head = lax.broadcasted_iota(I32, (1, LANES), 1) // AT_HEAD_DIM
    scale = AT_HEAD_DIM ** -0.5
    def scores(head):
        p, mine = head // 2, lane_head == head % 2
        sl = slice(p * LANES, (p + 1) * LANES)
        qm = jnp.where(mine, q[:, sl] * scale, jnp.zeros((AT_QB, LANES), BF16))
        s = lax.dot_general(qm, k[:, sl], (((1,), (1,)), ((), ())), preferred_element_type=F32)
        return jnp.where(key_valid, s + bias_ref[head], NEG_INF)

    def weighted_values(head, s, out):
        p, mine = head // 2, lane_head == head % 2
        vp = v[:, p * LANES:(p + 1) * LANES]
        e = jnp.exp(s - jnp.max(s, axis=-1, keepdims=True))
        v_aug = jnp.where(mine, vp, jnp.ones_like(vp))
        pv = jnp.dot(e.astype(BF16), v_aug, preferred_element_type=F32)
        row_sum = pltpu.roll(pv, AT_HEAD_DIM, 1)
        return jnp.where(mine, pv / row_sum, out)

    outs = [jnp.zeros((AT_QB, LANES), F32)] * (AT_HEADS // 2)
    s_prev = scores(0)
    for head in range(AT_HEADS):
        s_next = scores(head + 1) if head + 1 < AT_HEADS else None
        outs[head // 2] = weighted_values(head, s_prev, outs[head // 2])
        s_prev = s_next
    for p in range(AT_HEADS // 2):
        o_ref[:, p * LANES:(p + 1) * LANES] = outs[p].astype(o_ref.dtype)


def _attention(proj, bias, batch, seq):
    n = batch * seq
    nt = seq // AT_QB

    def kv(col, back):
        return pl.BlockSpec((AT_QB, AT_WIDTH),
                            lambda b, t, col=col, back=back: (b * nt + jnp.maximum(t - back, 0), col))

    return pl.pallas_call(
        _attn_kernel,
        grid=(batch, nt),
        in_specs=[pl.BlockSpec((AT_QB, AT_WIDTH), lambda b, t: (b * nt + t, COL_AQ)),
                  kv(COL_AK, 2), kv(COL_AK, 1), kv(COL_AK, 0),
                  kv(COL_AV, 2), kv(COL_AV, 1), kv(COL_AV, 0),
                  pl.BlockSpec((AT_HEADS, AT_QB, AT_KB), lambda b, t: (0, 0, 0))],
        out_specs=pl.BlockSpec((AT_QB, AT_WIDTH), lambda b, t: (b * nt + t, 0)),
        out_shape=jax.ShapeDtypeStruct((n, AT_WIDTH), BF16),
        compiler_params=_cparams(("parallel", "parallel")),
        name="band_attention",
    )(proj, proj, proj, proj, proj, proj, proj, bias)


def _bias_kernel(f_ref, o_ref):
    width = f_ref.shape[1]
    qi = lax.broadcasted_iota(I32, (AT_QB, AT_KB), 0) // CHUNK
    ki = lax.broadcasted_iota(I32, (AT_QB, AT_KB), 1) // CHUNK
    band = (ki >= qi) & (ki <= qi + LEFT_CHUNKS)
    for h in range(AT_HEADS):
        base = jnp.broadcast_to(f_ref[h:h + 1, :], (AT_QB, width))
        rolled = pltpu.roll(base, width - (AT_QB - 1), 1, stride=1, stride_axis=0)
        o_ref[h] = jnp.where(band, rolled[:, :AT_KB], NEG_INF)


def _attention_bias(rel_table):
    table = rel_table.astype(F32)
    n_clipped = (AT_KB - 1) - REL_CLIP
    f = jnp.concatenate([jnp.broadcast_to(table[:, :1], (AT_HEADS, n_clipped)),
                         table[:, :AT_QB + REL_CLIP + 1]], axis=1)
    return pl.pallas_call(
        _bias_kernel,
        out_shape=jax.ShapeDtypeStruct((AT_HEADS, AT_QB, AT_KB), F32),
        name="attn_bias",
    )(f)


def _layer_norm(y, g, b):
    mu = jnp.mean(y, axis=-1, keepdims=True)
    d = y - mu
    var = jnp.mean(d * d, axis=-1, keepdims=True)
    return d * lax.rsqrt(var + LN_EPS) * g + b


def _load_row_tiles(ref):
    m = ref.shape[0] // ROW_TILES
    return jnp.concatenate([ref[pl.ds(j, m, stride=ROW_TILES), :] for j in range(ROW_TILES)],
                           axis=1)


def _store_row_tiles(ref, x):
    m = x.shape[0]
    for j in range(ROW_TILES):
        ref[pl.ds(j, m, stride=ROW_TILES), :] = x[:, j * LANES:(j + 1) * LANES]


def _merge_kernel(dn_ref, at_ref, ga_ref, gb_ref, x_ref, wa_ref, wb_ref, wo_ref, g_ref, b_ref,
                  x1t_ref):
    ya = jnp.dot(dn_ref[...], wa_ref[...], preferred_element_type=F32)
    yb = jnp.dot(at_ref[...], wb_ref[...], preferred_element_type=F32)
    mix = _sigmoid(ga_ref[...].astype(F32)) * ya + _sigmoid(gb_ref[...].astype(F32)) * yb
    y = ALPHA * x_ref[...] + jnp.dot(mix.astype(BF16), wo_ref[...], preferred_element_type=F32)
    _store_row_tiles(x1t_ref, _layer_norm(y, g_ref[...], b_ref[...]))


def _merge(dn, at, proj, x, wa, wb, wo, g, b, tm=512):
    n = x.shape[0]
    return pl.pallas_call(
        _merge_kernel,
        grid=(n // tm,),
        in_specs=[pl.BlockSpec((tm, DN_WIDTH), lambda i: (i, 0)),
                  pl.BlockSpec((tm, AT_WIDTH), lambda i: (i, 0)),
                  pl.BlockSpec((tm, D_MODEL), lambda i: (i, COL_GA)),
                  pl.BlockSpec((tm, D_MODEL), lambda i: (i, COL_GB)),
                  pl.BlockSpec((tm, D_MODEL), lambda i: (i, 0)),
                  pl.BlockSpec((DN_WIDTH, D_MODEL), lambda i: (0, 0)),
                  pl.BlockSpec((AT_WIDTH, D_MODEL), lambda i: (0, 0)),
                  pl.BlockSpec((D_MODEL, D_MODEL), lambda i: (0, 0)),
                  pl.BlockSpec((1, D_MODEL), lambda i: (0, 0)),
                  pl.BlockSpec((1, D_MODEL), lambda i: (0, 0))],
        out_specs=pl.BlockSpec((tm * ROW_TILES, LANES), lambda i: (i, 0)),
        out_shape=jax.ShapeDtypeStruct((n * ROW_TILES, LANES), F32),
        compiler_params=_cparams(("parallel",)),
        name="merge_ln1",
    )(dn, at, proj, proj, x, wa, wb, wo, g, b)


def _router_kernel(xt_ref, rw_ref, rb_ref, dest_ref, wt_ref, seg_ref):
    gt, ne = GROUP, N_EXPERTS
    logits = lax.dot_general(rw_ref[...], _load_row_tiles(xt_ref), (((1,), (1,)), ((), ())),
                             preferred_element_type=F32, precision=lax.Precision.HIGHEST)
    scores = _sigmoid(logits)
    sel = scores + rb_ref[:, 0:1]
    eidx = lax.broadcasted_iota(I32, (ne, gt), 0)
    picked = jnp.zeros((ne, gt), jnp.bool_)
    hits, vals = [], []
    for _ in range(TOP_K):
        m = jnp.max(sel, axis=0, keepdims=True)
        idx = jnp.min(jnp.where(sel == m, eidx, ne), axis=0, keepdims=True)
        hit = eidx == idx
        hits.append(hit)
        vals.append(jnp.sum(jnp.where(hit, scores, 0.0), axis=0, keepdims=True))
        sel = jnp.where(hit, -jnp.inf, sel)
        picked = picked | hit
    total = vals[0]
    for v in vals[1:]:
        total = total + v

    onehot = picked.astype(BF16)
    tr = lax.broadcasted_iota(I32, (gt, gt), 0)
    tc = lax.broadcasted_iota(I32, (gt, gt), 1)
    rank = jnp.dot(onehot, (tr < tc).astype(BF16), preferred_element_type=F32)
    count = jnp.sum(picked.astype(F32), axis=1, keepdims=True)
    tiles = jnp.floor((count + (SUBLANES - 1)) * (1.0 / SUBLANES))
    er = lax.broadcasted_iota(I32, (ne, ne), 0)
    ec = lax.broadcasted_iota(I32, (ne, ne), 1)
    start = SUBLANES * jnp.dot((ec < er).astype(BF16),
                               jnp.broadcast_to(tiles, (ne, LANES)).astype(BF16),
                               preferred_element_type=F32)
    slot = start[:, 0:1] + rank

    sub = lax.broadcasted_iota(I32, (SUBLANES, gt), 0)
    dest = jnp.zeros((SUBLANES, gt), F32)
    wts = jnp.zeros((SUBLANES, gt), F32)
    for kk in range(TOP_K):
        d = jnp.sum(jnp.where(hits[kk], slot, 0.0), axis=0, keepdims=True)
        dest = jnp.where(sub == kk, d, dest)
        wts = jnp.where(sub == kk, vals[kk] / total * ROUTED_SCALE, wts)
    dest_ref[0] = dest.astype(I32) * ROW_TILES
    wt_ref[0] = wts
    lane = lax.broadcasted_iota(I32, (ne, LANES), 1)
    seg_ref[0] = jnp.where(lane == 0, start, jnp.broadcast_to(count, (ne, LANES))).astype(I32)


def _router(x1t, rw_t, rb):
    n = x1t.shape[0] // ROW_TILES
    ng = n // GROUP
    return pl.pallas_call(
        _router_kernel,
        grid=(ng,),
        in_specs=[pl.BlockSpec((GROUP * ROW_TILES, LANES), lambda g: (g, 0)),
                  pl.BlockSpec((N_EXPERTS, D_MODEL), lambda g: (0, 0)),
                  pl.BlockSpec((N_EXPERTS, LANES), lambda g: (0, 0))],
        out_specs=[pl.BlockSpec((1, SUBLANES, GROUP), lambda g: (g, 0, 0)),
                   pl.BlockSpec((1, SUBLANES, GROUP), lambda g: (g, 0, 0)),
                   pl.BlockSpec((1, N_EXPERTS, LANES), lambda g: (g, 0, 0))],
        out_shape=[jax.ShapeDtypeStruct((ng, SUBLANES, GROUP), I32),
                   jax.ShapeDtypeStruct((ng, SUBLANES, GROUP), F32),
                   jax.ShapeDtypeStruct((ng, N_EXPERTS, LANES), I32)],
        compiler_params=_cparams(("parallel",)),
        name="router",
    )(x1t, rw_t, rb)


def _moe_kernel(start_ref, count_ref, xt_ref, dest_hbm, wt_hbm, wgu_ref, wd_ref, out_ref,
                buf, dest_s, wt_s, sem):
    g = pl.program_id(0)
    e = pl.program_id(1)
    ne, gt = N_EXPERTS, GROUP

    def routing_copies():
        return (pltpu.make_async_copy(dest_hbm.at[g], dest_s, sem.at[0]),
                pltpu.make_async_copy(wt_hbm.at[g], wt_s, sem.at[1]))

    def tile_of(row8):
        return pl.ds(pl.multiple_of(row8, ROW_TILES), ROW_TILES)

    @pl.when(e == 0)
    def _():
        for cp in routing_copies():
            cp.start()
        zeros = jnp.zeros((SUBLANES * ROW_TILES, LANES), F32)

        def zero_pad(ee, carry):
            s = start_ref[g * ne + ee]
            cnt = count_ref[g * ne + ee]
            z0 = pl.multiple_of((s + (cnt // SUBLANES) * SUBLANES) * ROW_TILES, SUBLANES * ROW_TILES)
            buf[pl.ds(z0, SUBLANES * ROW_TILES), :] = zeros
            return carry

        lax.fori_loop(0, ne, zero_pad, 0)
        last = g * ne + ne - 1
        end = start_ref[last] + (count_ref[last] + SUBLANES - 1) // SUBLANES * SUBLANES
        end = pl.multiple_of(end * ROW_TILES, SUBLANES * ROW_TILES)
        buf[pl.ds(end, FFN_BM * ROW_TILES), :] = jnp.zeros((FFN_BM * ROW_TILES, LANES), F32)
        for cp in routing_copies():
            cp.wait()

        def scatter(t0, carry):
            for u in range(ROW_LOOP_UNROLL):
                tok = t0 * ROW_LOOP_UNROLL + u
                row = xt_ref[tile_of(tok * ROW_TILES), :]
                for kk in range(TOP_K):
                    buf[tile_of(dest_s[kk * gt + tok]), :] = row
            return carry

        lax.fori_loop(0, gt // ROW_LOOP_UNROLL, scatter, 0)

    experts = list(range(MOE_EPS))
    seg = [(start_ref[g * ne + e * MOE_EPS + k], count_ref[g * ne + e * MOE_EPS + k])
           for k in experts]
    n_blocks = [(cnt + FFN_BM - 1) // FFN_BM for _, cnt in seg]
    last_block_row = GROUP_ROWS - FFN_BM

    def ffn_block(i, carry):
        r0 = [pl.multiple_of(jnp.minimum(s + i * FFN_BM, last_block_row), SUBLANES) for s, _ in seg]
        base = [pl.multiple_of(r * ROW_TILES, SUBLANES * ROW_TILES) for r in r0]
        xin = [jnp.concatenate([buf[pl.ds(base[k] + j, FFN_BM, stride=ROW_TILES), :]
                                for j in range(ROW_TILES)], axis=1) for k in experts]
        gu = [jnp.dot(xin[k].astype(BF16), wgu_ref[k], preferred_element_type=F32)
              for k in experts]
        hidden = [(_silu(gu[k][:, :EXPERT_DIM]) * gu[k][:, EXPERT_DIM:]).astype(BF16)
                  for k in experts]
        y = [jnp.dot(hidden[k], wd_ref[k], preferred_element_type=F32) for k in experts]
        for k in experts:
            s, cnt = seg[k]
            keep = (r0[k] + lax.broadcasted_iota(I32, (FFN_BM, 1), 0)) < s + cnt
            for j in range(ROW_TILES):
                buf[pl.ds(base[k] + j, FFN_BM, stride=ROW_TILES), :] = jnp.where(
                    keep, y[k][:, j * LANES:(j + 1) * LANES], xin[k][:, j * LANES:(j + 1) * LANES])
        return carry

    total_blocks = n_blocks[0]
    for nb in n_blocks[1:]:
        total_blocks = jnp.maximum(total_blocks, nb)
    lax.fori_loop(0, total_blocks, ffn_block, 0)

    @pl.when(e == pl.num_programs(1) - 1)
    def _():
        def combine(t0, carry):
            for u in range(ROW_LOOP_UNROLL):
                tok = t0 * ROW_LOOP_UNROLL + u
                acc = buf[tile_of(dest_s[tok]), :] * wt_s[tok]
                for kk in range(1, TOP_K):
                    acc = acc + buf[tile_of(dest_s[kk * gt + tok]), :] * wt_s[kk * gt + tok]
                out_ref[tile_of(tok * ROW_TILES), :] = acc
            return carry

        lax.fori_loop(0, gt // ROW_LOOP_UNROLL, combine, 0)


def _moe(x1t, dest, wts, starts, counts, wgu_all, wd_all, layer):
    n = x1t.shape[0] // ROW_TILES
    ng = n // GROUP
    first = layer * (N_EXPERTS // MOE_EPS)
    grid_spec = pltpu.PrefetchScalarGridSpec(
        num_scalar_prefetch=2,
        grid=(ng, N_EXPERTS // MOE_EPS),
        in_specs=[pl.BlockSpec((GROUP * ROW_TILES, LANES), lambda g, e, *_: (g, 0),
                               pipeline_mode=pl.Buffered(1)),
                  pl.BlockSpec(memory_space=pl.ANY),
                  pl.BlockSpec(memory_space=pl.ANY),
                  pl.BlockSpec((MOE_EPS, D_MODEL, 2 * EXPERT_DIM),
                               lambda g, e, *_: (first + e, 0, 0)),
                  pl.BlockSpec((MOE_EPS, EXPERT_DIM, D_MODEL),
                               lambda g, e, *_: (first + e, 0, 0))],
        out_specs=pl.BlockSpec((GROUP * ROW_TILES, LANES), lambda g, e, *_: (g, 0),
                               pipeline_mode=pl.Buffered(1)),
        scratch_shapes=[pltpu.VMEM((GROUP_ROWS * ROW_TILES, LANES), F32),
                        pltpu.SMEM((SUBLANES * GROUP,), I32),
                        pltpu.SMEM((SUBLANES * GROUP,), F32),
                        pltpu.SemaphoreType.DMA((2,))],
    )
    return pl.pallas_call(
        _moe_kernel,
        grid_spec=grid_spec,
        out_shape=jax.ShapeDtypeStruct((n * ROW_TILES, LANES), F32),
        compiler_params=_cparams(("arbitrary", "arbitrary")),
        name="routed_experts",
    )(starts, counts, x1t, dest, wts, wgu_all, wd_all)


def _final_kernel(x1t_ref, rt_ref, wsg_ref, wsd_ref, g_ref, b_ref, o_ref):
    x1 = _load_row_tiles(x1t_ref)
    gus = jnp.dot(x1.astype(BF16), wsg_ref[...], preferred_element_type=F32)
    hidden = _silu(gus[:, :SHARED_DIM]) * gus[:, SHARED_DIM:]
    shared = jnp.dot(hidden.astype(BF16), wsd_ref[...], preferred_element_type=F32)
    y = ALPHA * x1 + (_load_row_tiles(rt_ref) + shared)
    o_ref[...] = _layer_norm(y, g_ref[...], b_ref[...])


def _final(x1t, routed_t, wsg, wsd, g, b, tm=512):
    n = x1t.shape[0] // ROW_TILES
    return pl.pallas_call(
        _final_kernel,
        grid=(n // tm,),
        in_specs=[pl.BlockSpec((tm * ROW_TILES, LANES), lambda i: (i, 0)),
                  pl.BlockSpec((tm * ROW_TILES, LANES), lambda i: (i, 0)),
                  pl.BlockSpec((D_MODEL, 2 * SHARED_DIM), lambda i: (0, 0)),
                  pl.BlockSpec((SHARED_DIM, D_MODEL), lambda i: (0, 0)),
                  pl.BlockSpec((1, D_MODEL), lambda i: (0, 0)),
                  pl.BlockSpec((1, D_MODEL), lambda i: (0, 0))],
        out_specs=pl.BlockSpec((tm, D_MODEL), lambda i: (i, 0)),
        out_shape=jax.ShapeDtypeStruct((n, D_MODEL), F32),
        compiler_params=_cparams(("parallel",)),
        name="shared_ln2",
    )(x1t, routed_t, wsg, wsd, g, b)


C_SMALL = 4 * DN_WIDTH
C_ATT = C_SMALL + 2 * DN_HEADS
C_GATE = C_ATT + 3 * AT_WIDTH


def _layer(l, x, batch, seq, w_big_all, wgu_all, wd_all, w_in, conv_w, a_log, dt_bias,
           dn_norm_w, rel_table, w_branch_a, w_branch_b, w_out, ln1_g, ln1_b, router_w,
           router_bias, ws_gate_up, ws_down, ln2_g, ln2_b):
    w_small = w_in[:, C_SMALL:C_ATT]
    w_col = jnp.pad(w_small, ((0, 0), (0, LANES - 2 * DN_HEADS))).astype(BF16)
    w_row = w_small.T.astype(BF16)
    decay_rate = jnp.exp(a_log.astype(F32))
    zeros8 = jnp.zeros((DN_HEADS,), F32)
    p_col = jnp.pad(jnp.stack([jnp.concatenate([zeros8, decay_rate]),
                               jnp.concatenate([zeros8, dt_bias.astype(F32)])]),
                    ((0, 0), (0, LANES - 2 * DN_HEADS)))
    p_row = jnp.pad(p_col[:, :2 * DN_HEADS].T, ((0, 0), (0, LANES - 2)))

    proj = _matmul(x, w_big_all, l, BF16, tm=1024, tn=1280)
    cg, rg = _gates(x, w_col, w_row, p_col, p_row)
    dn = _deltanet(proj, cg, rg, conv_w.astype(F32), dn_norm_w.astype(F32).reshape(1, DN_HEAD_DIM),
                   batch, seq)
    at = _attention(proj, _attention_bias(rel_table), batch, seq)
    x1t = _merge(dn, at, proj, x, w_branch_a.astype(BF16), w_branch_b.astype(BF16),
                 w_out.astype(BF16), ln1_g.reshape(1, -1), ln1_b.reshape(1, -1))
    dest, wts, seg = _router(x1t, router_w.T.astype(F32),
                             jnp.broadcast_to(router_bias.astype(F32)[:, None], (N_EXPERTS, LANES)))
    ng = dest.shape[0]
    routed_t = _moe(x1t, dest.reshape(ng, SUBLANES * GROUP), wts.reshape(ng, SUBLANES * GROUP),
                    seg[:, :, 0].reshape(-1), seg[:, :, 1].reshape(-1), wgu_all, wd_all, l)
    return _final(x1t, routed_t, ws_gate_up.astype(BF16), ws_down.astype(BF16),
                  ln2_g.reshape(1, -1), ln2_b.reshape(1, -1))


def kernel(x, w_in, conv_w, a_log, dt_bias, dn_norm_w, rel_table, w_branch_a, w_branch_b, w_out,
           ln1_g, ln1_b, router_w, router_bias, expert_w_gate_up, expert_w_down,
           shared_w_gate_up, shared_w_down, ln2_g, ln2_b):
    batch, seq, d = x.shape
    assert d == D_MODEL and seq % DN_TB == 0 and seq % AT_QB == 0 and (batch * seq) % GROUP == 0
    depth = w_in.shape[0]
    xf = x.reshape(batch * seq, d).astype(F32)
    w_big_all = jnp.concatenate([w_in[:, :, :C_SMALL], w_in[:, :, C_GATE:], w_in[:, :, C_ATT:C_GATE]],
                                axis=2).astype(BF16).reshape(depth * D_MODEL, BIG_COLS)
    wgu_all = expert_w_gate_up.astype(BF16).reshape(depth * N_EXPERTS, D_MODEL, 2 * EXPERT_DIM)
    wd_all = expert_w_down.astype(BF16).reshape(depth * N_EXPERTS, EXPERT_DIM, D_MODEL)
    for l in range(depth):
        xf = _layer(l, xf, batch, seq, w_big_all, wgu_all, wd_all, w_in[l], conv_w[l],
                        a_log[l], dt_bias[l], dn_norm_w[l], rel_table[l], w_branch_a[l],
                        w_branch_b[l], w_out[l], ln1_g[l], ln1_b[l], router_w[l], router_bias[l],
                        shared_w_gate_up[l], shared_w_down[l], ln2_g[l], ln2_b[l])
    return xf.reshape(batch, seq, d).astype(x.dtype)
```

```python
import functools

import jax
import jax.numpy as jnp
from jax import lax
from jax.experimental import pallas as pl
from jax.experimental.pallas import tpu as pltpu

F32 = jnp.float32
BF16 = jnp.bfloat16
I32 = jnp.int32

D_MODEL = 1024
DEPTH = 2
CHUNK = 64
DN_HEADS = 8
DN_HEAD_DIM = 128
DN_WIDTH = DN_HEADS * DN_HEAD_DIM
CONV_WIDTH = 4
AT_HEADS = 8
AT_HEAD_DIM = 64
AT_WIDTH = AT_HEADS * AT_HEAD_DIM
LEFT_CHUNKS = 8
REL_CLIP = 256
N_EXPERTS = 64
TOP_K = 6
EXPERT_DIM = 256
SHARED_DIM = 256
ROUTED_SCALE = 2.5
ALPHA = (2 * DEPTH) ** 0.25
LN_EPS = 1e-5
RMS_EPS = 1e-6
L2_EPS = 1e-6
NEG_INF = -1e30

LANES = 128
SUBLANES = 8
ROW_TILES = D_MODEL // LANES

DN_TB = 256
DN_SB = 128
DN_HEADS_PER_PASS = 8
AT_QB = 256
AT_KB = 3 * AT_QB
GROUP = 1024
FFN_BM = 128
MOE_EPS = 4
MOE_WEIGHT_SLOTS = 3
ROW_LOOP_UNROLL = 4
GROUP_ROWS = GROUP * TOP_K + N_EXPERTS * (SUBLANES - 1) + FFN_BM
GROUP_ROWS = -(-GROUP_ROWS // SUBLANES) * SUBLANES
VMEM_LIMIT = 56 * 1024 * 1024

COL_Q, COL_K, COL_V, COL_Z, COL_GA, COL_GB = 0, 1, 2, 3, 4, 5
COL_AQ, COL_AK, COL_AV = 12, 13, 14
BIG_COLS = 4 * DN_WIDTH + 2 * D_MODEL + 3 * AT_WIDTH


def _sigmoid(x):
    return 1.0 / (1.0 + jnp.exp(-x))


def _silu(x):
    return x * _sigmoid(x)


def _softplus(x):
    return jnp.maximum(x, 0.0) + jnp.log1p(jnp.exp(-jnp.abs(x)))


def _bdot(a, b):
    return jnp.dot(a.astype(BF16), b.astype(BF16), preferred_element_type=F32)


def _bdot_nt(a, b):
    return lax.dot_general(a.astype(BF16), b.astype(BF16), (((1,), (1,)), ((), ())),
                           preferred_element_type=F32)


def _cparams(sem):
    return pltpu.CompilerParams(dimension_semantics=sem, vmem_limit_bytes=VMEM_LIMIT)


def _mm_kernel(x_ref, w_ref, o_ref):
    o_ref[...] = jnp.dot(x_ref[...].astype(BF16), w_ref[...],
                         preferred_element_type=F32).astype(o_ref.dtype)


def _matmul(x, w_all, layer, out_dtype, tm, tn):
    n, k = x.shape
    m = w_all.shape[1]
    return pl.pallas_call(
        _mm_kernel,
        grid=(n // tm, m // tn),
        in_specs=[pl.BlockSpec((tm, k), lambda i, j: (i, 0)),
                  pl.BlockSpec((k, tn), lambda i, j: (layer, j))],
        out_specs=pl.BlockSpec((tm, tn), lambda i, j: (i, j)),
        out_shape=jax.ShapeDtypeStruct((n, m), out_dtype),
        compiler_params=_cparams(("parallel", "parallel")),
        name="inproj",
    )(x, w_all)


def _gates_kernel(x_ref, wc_ref, wr_ref, pc_ref, pr_ref, cg_ref, rg_ref, *, tm):
    x = x_ref[...].astype(BF16)
    r = lax.broadcasted_iota(I32, (tm, tm), 0)
    c = lax.broadcasted_iota(I32, (tm, tm), 1)
    same = (r // CHUNK) == (c // CHUNK)

    pc = jnp.dot(x, wc_ref[...], preferred_element_type=F32)
    lane = lax.broadcasted_iota(I32, (tm, LANES), 1)
    col = jnp.where(lane < DN_HEADS, _sigmoid(pc),
                    -pc_ref[0:1, :] * _softplus(pc + pc_ref[1:2, :]))
    lower = (same & (r >= c)).astype(F32)
    gcc = jnp.dot(lower, col, preferred_element_type=F32, precision=lax.Precision.HIGHEST)
    cg_ref[...] = jnp.where(lane < DN_HEADS, col, gcc)

    pr = lax.dot_general(wr_ref[...], x, (((1,), (1,)), ((), ())),
                         preferred_element_type=F32)
    sub = lax.broadcasted_iota(I32, (2 * DN_HEADS, tm), 0)
    row = jnp.where(sub < DN_HEADS, _sigmoid(pr),
                    -pr_ref[:, 0:1] * _softplus(pr + pr_ref[:, 1:2]))
    upper = (same & (r <= c)).astype(F32)
    gcr = jnp.dot(row, upper, preferred_element_type=F32, precision=lax.Precision.HIGHEST)
    rg_ref[...] = jnp.where(sub < DN_HEADS, row, gcr)


def _gates(xb, w_col, w_row, p_col, p_row, tm=256):
    n = xb.shape[0]
    return pl.pallas_call(
        functools.partial(_gates_kernel, tm=tm),
        grid=(n // tm,),
        in_specs=[pl.BlockSpec((tm, D_MODEL), lambda i: (i, 0)),
                  pl.BlockSpec((D_MODEL, LANES), lambda i: (0, 0)),
                  pl.BlockSpec((2 * DN_HEADS, D_MODEL), lambda i: (0, 0)),
                  pl.BlockSpec((2, LANES), lambda i: (0, 0)),
                  pl.BlockSpec((2 * DN_HEADS, LANES), lambda i: (0, 0))],
        out_specs=[pl.BlockSpec((tm, LANES), lambda i: (i, 0)),
                   pl.BlockSpec((2 * DN_HEADS, tm), lambda i: (0, i))],
        out_shape=[jax.ShapeDtypeStruct((n, LANES), F32),
                   jax.ShapeDtypeStruct((2 * DN_HEADS, n), F32)],
        compiler_params=_cparams(("parallel",)),
        name="dn_gates",
    )(xb, w_col, w_row, p_col, p_row)


def _dn_kernel(q_ref, k_ref, v_ref, z_ref, cg_ref, rg_ref, cw_ref, nw_ref, o_ref,
               xpad, s_ref, u_s, wq_s, qkk_s, vn_s, o_s, mask_s):
    tb, nch, hd = DN_TB, DN_TB // CHUNK, DN_HEAD_DIM
    t = pl.program_id(1)

    sb = DN_SB
    halves = tb // sb

    @pl.when(t == 0)
    def _():
        xpad[:, 0:SUBLANES, :] = jnp.zeros((3, SUBLANES, DN_WIDTH), F32)
        s_ref[...] = jnp.zeros_like(s_ref)
        r = lax.broadcasted_iota(I32, (sb, sb), 0)
        c = lax.broadcasted_iota(I32, (sb, sb), 1)
        same = (r // CHUNK) == (c // CHUNK)
        mask_s[0] = jnp.where(same & (r >= c), 0.0, NEG_INF)
        mask_s[1] = (same & (r > c)).astype(F32)
        mask_s[2] = ((r // 16) == (c // 16)).astype(F32)
        mask_s[3] = (r == c).astype(F32)

    @pl.when(t > 0)
    def _():
        xpad[:, 0:SUBLANES, :] = xpad[:, tb:tb + SUBLANES, :]

    xpad[0, SUBLANES:SUBLANES + tb, :] = q_ref[...].astype(F32)
    xpad[1, SUBLANES:SUBLANES + tb, :] = k_ref[...].astype(F32)
    xpad[2, SUBLANES:SUBLANES + tb, :] = v_ref[...].astype(F32)
    vn_s[...] = jnp.zeros_like(vn_s)

    lane_chunk = lax.broadcasted_iota(I32, (hd, sb), 1) // CHUNK
    ones = jnp.ones((hd, hd), BF16)

    def row_sum_sq(x):
        return jnp.dot((x * x).astype(BF16), ones, preferred_element_type=F32)

    def each(fn, *lists):
        return [fn(*args) for args in zip(*lists)]

    for h0 in range(0, DN_HEADS, DN_HEADS_PER_PASS):
        inst = [(h, half) for h in range(h0, h0 + DN_HEADS_PER_PASS) for half in range(halves)]

        def conv(idx, h, half):
            block = xpad[idx, half * sb:half * sb + sb + SUBLANES, h * hd:(h + 1) * hd]
            acc = None
            for i in range(CONV_WIDTH):
                col0 = idx * DN_WIDTH + h * hd
                back = CONV_WIDTH - 1 - i
                shifted = block if back == 0 else pltpu.roll(block, back, 0)
                term = shifted[SUBLANES:SUBLANES + sb, :] * cw_ref[i:i + 1, col0:col0 + hd]
                acc = term if acc is None else acc + term
            return _silu(acc)

        qh = [conv(0, h, half) for h, half in inst]
        kh = [conv(1, h, half) for h, half in inst]
        vh = [conv(2, h, half) for h, half in inst]
        qh = each(lambda x: x * lax.rsqrt(row_sum_sq(x) + L2_EPS) * (hd ** -0.5), qh)
        kh = each(lambda x: x * lax.rsqrt(row_sum_sq(x) + L2_EPS), kh)
        beta = [cg_ref[half * sb:(half + 1) * sb, h:h + 1] for h, half in inst]
        gc = [cg_ref[half * sb:(half + 1) * sb, DN_HEADS + h:DN_HEADS + h + 1] for h, half in inst]
        gcr = [rg_ref[DN_HEADS + h:DN_HEADS + h + 1, half * sb:(half + 1) * sb] for h, half in inst]
        dec = each(lambda a, b: jnp.exp((a - b) + mask_s[0]), gc, gcr)
        kb = each(lambda a, b: a * b, kh, beta)
        khb = each(lambda a: a.astype(BF16), kh)
        lmat = each(lambda a, b, d: _bdot_nt(a, b) * d * mask_s[1], kb, khb, dec)
        qk = each(lambda a, b, d: _bdot_nt(a, b) * d, qh, khb, dec)
        eye = mask_s[3]
        dmat = each(lambda a: a * mask_s[2], lmat)
        omat = each(lambda a, b: a - b, lmat, dmat)
        s1 = each(lambda a: eye - a, dmat)
        d2 = each(lambda a: _bdot(a, a), dmat)
        s2 = each(lambda a, b: a + _bdot(a, b), s1, d2)
        d4 = each(lambda a: _bdot(a, a), d2)
        s3 = each(lambda a, b: a + _bdot(a, b), s2, d4)
        d8 = each(lambda a: _bdot(a, a), d4)
        dinv = each(lambda a, b: a + _bdot(a, b), s3, d8)
        m1 = each(_bdot, dinv, omat)
        m2 = each(lambda a: _bdot(a, a), m1)
        im = each(lambda a: eye - a, m1)
        pm = each(lambda a, b: a + _bdot(a, b), im, m2)
        tinv = each(_bdot, pm, dinv)
        eg = each(jnp.exp, gc)
        rhs = each(lambda v, b, k, e: jnp.concatenate([v * b, k * e], axis=1), vh, beta, kb, eg)
        uw = each(_bdot, tinv, rhs)
        for i, (h, half) in enumerate(inst):
            rs = slice(half * sb, (half + 1) * sb)
            u_s[h, rs, :] = uw[i][:, :hd]
            w_b = uw[i][:, hd:].astype(BF16)
            qd_b = (qh[i] * eg[i]).astype(BF16)
            qk_b = qk[i].astype(BF16)
            gl = jnp.concatenate(
                [jnp.broadcast_to(gc[i][cc * CHUNK + CHUNK - 1:cc * CHUNK + CHUNK, :], (CHUNK, 1))
                 for cc in range(sb // CHUNK)], axis=0)
            kdt = (kh[i] * jnp.exp(gl - gc[i])).T
            for cc in range(sb // CHUNK):
                chunk = half * (sb // CHUNK) + cc
                cr = slice(cc * CHUNK, (cc + 1) * CHUNK)
                wq_s[h, chunk, 0:CHUNK, :] = w_b[cr, :]
                wq_s[h, chunk, CHUNK:2 * CHUNK, :] = qd_b[cr, :]
                qkk_s[h, chunk, 0:CHUNK, :] = qk_b[cr, :]
                qkk_s[h, chunk, CHUNK:CHUNK + hd, :] = jnp.where(lane_chunk == cc, kdt,
                                                                  0.0).astype(BF16)

    heads = list(range(DN_HEADS))

    def chunk_step(cc, carry):
        r0 = pl.multiple_of(cc * CHUNK, CHUNK)
        rows = pl.ds(r0, CHUNK)
        sub_rows = pl.ds(pl.multiple_of((cc // (sb // CHUNK)) * sb, sb), sb)
        s_old = [s_ref[h] for h in heads]
        s_b = each(lambda a: a.astype(BF16), s_old)
        by_state = [jnp.dot(wq_s[h, cc], s_b[h], preferred_element_type=F32) for h in heads]
        for h in heads:
            vn_s[h, rows, :] = (u_s[h, rows, :] - by_state[h][:CHUNK, :]).astype(BF16)
        vn_all = [vn_s[h, sub_rows, :] for h in heads]
        by_vnew = [jnp.dot(qkk_s[h, cc], vn_all[h], preferred_element_type=F32) for h in heads]
        for h in heads:
            g_last = cg_ref[pl.ds(r0 + CHUNK - 1, 1), DN_HEADS + h:DN_HEADS + h + 1]
            s_ref[h] = s_old[h] * jnp.exp(g_last) + by_vnew[h][CHUNK:, :]
            o_s[rows, h * hd:(h + 1) * hd] = by_state[h][CHUNK:, :] + by_vnew[h][:CHUNK, :]
        return carry

    lax.fori_loop(0, nch, chunk_step, 0)

    for h in range(DN_HEADS):
        sl = slice(h * hd, (h + 1) * hd)
        o = o_s[:, sl]
        o = o * lax.rsqrt(row_sum_sq(o) * (1.0 / hd) + RMS_EPS)
        z = z_ref[:, sl].astype(F32)
        o_ref[:, sl] = (o * nw_ref[...] * _silu(z)).astype(o_ref.dtype)


def _deltanet(proj, cg, rg, conv_w, norm_w, batch, seq):
    n = batch * seq
    nt = seq // DN_TB
    tb, hd, nch = DN_TB, DN_HEAD_DIM, DN_TB // CHUNK

    def col(j):
        return pl.BlockSpec((tb, DN_WIDTH), lambda b, t, j=j: (b * nt + t, j))

    return pl.pallas_call(
        _dn_kernel,
        grid=(batch, nt),
        in_specs=[col(COL_Q), col(COL_K), col(COL_V), col(COL_Z),
                  pl.BlockSpec((tb, LANES), lambda b, t: (b * nt + t, 0)),
                  pl.BlockSpec((2 * DN_HEADS, tb), lambda b, t: (0, b * nt + t)),
                  pl.BlockSpec((CONV_WIDTH, 3 * DN_WIDTH), lambda b, t: (0, 0)),
                  pl.BlockSpec((1, hd), lambda b, t: (0, 0))],
        out_specs=pl.BlockSpec((tb, DN_WIDTH), lambda b, t: (b * nt + t, 0)),
        out_shape=jax.ShapeDtypeStruct((n, DN_WIDTH), BF16),
        scratch_shapes=[
            pltpu.VMEM((3, tb + SUBLANES, DN_WIDTH), F32),
            pltpu.VMEM((DN_HEADS, hd, hd), F32),
            pltpu.VMEM((DN_HEADS, tb, hd), F32),
            pltpu.VMEM((DN_HEADS, nch, 2 * CHUNK, hd), BF16),
            pltpu.VMEM((DN_HEADS, nch, CHUNK + hd, DN_SB), BF16),
            pltpu.VMEM((DN_HEADS, tb, hd), BF16),
            pltpu.VMEM((tb, DN_WIDTH), F32),
            pltpu.VMEM((4, DN_SB, DN_SB), F32),
        ],
        compiler_params=_cparams(("parallel", "arbitrary")),
        name="deltanet",
    )(proj, proj, proj, proj, cg, rg, conv_w, norm_w)


def _attn_kernel(q_ref, k0_ref, k1_ref, k2_ref, v0_ref, v1_ref, v2_ref, bias_ref, o_ref):
    t = pl.program_id(1)
    q = q_ref[...]
    k = jnp.concatenate([k0_ref[...], k1_ref[...], k2_ref[...]], axis=0)
    v = jnp.concatenate([v0_ref[...], v1_ref[...], v2_ref[...]], axis=0)
    kidx = lax.broadcasted_iota(I32, (1, AT_KB), 1)
    key_valid = kidx >= (2 - t) * AT_QB
    lane_head = lax.broadcasted_iota(I32, (1, LANES), 1) // AT_HEAD_DIM
    scale = AT_HEAD_DIM ** -0.5
    def scores(head):
        p, mine = head // 2, lane_head == head % 2
        sl = slice(p * LANES, (p + 1) * LANES)
        qm = jnp.where(mine, q[:, sl] * scale, jnp.zeros((AT_QB, LANES), BF16))
        s = lax.dot_general(qm, k[:, sl], (((1,), (1,)), ((), ())), preferred_element_type=F32)
        return jnp.where(key_valid, s + bias_ref[head], NEG_INF)

    def weighted_values(head, s, out):
        p, mine = head // 2, lane_head == head % 2
        vp = v[:, p * LANES:(p + 1) * LANES]
        e = jnp.exp(s - jnp.max(s, axis=-1, keepdims=True))
        v_aug = jnp.where(mine, vp, jnp.ones_like(vp))
        pv = jnp.dot(e.astype(BF16), v_aug, preferred_element_type=F32)
        row_sum = pltpu.roll(pv, AT_HEAD_DIM, 1)
        return jnp.where(mine, pv / row_sum, out)

    outs = [jnp.zeros((AT_QB, LANES), F32)] * (AT_HEADS // 2)
    s_prev = scores(0)
    for head in range(AT_HEADS):
        s_next = scores(head + 1) if head + 1 < AT_HEADS else None
        outs[head // 2] = weighted_values(head, s_prev, outs[head // 2])
        s_prev = s_next
    for p in range(AT_HEADS // 2):
        o_ref[:, p * LANES:(p + 1) * LANES] = outs[p].astype(o_ref.dtype)


def _attention(proj, bias, batch, seq):
    n = batch * seq
    nt = seq // AT_QB

    def kv(col, back):
        return pl.BlockSpec((AT_QB, AT_WIDTH),
                            lambda b, t, col=col, back=back: (b * nt + jnp.maximum(t - back, 0), col))

    return pl.pallas_call(
        _attn_kernel,
        grid=(batch, nt),
        in_specs=[pl.BlockSpec((AT_QB, AT_WIDTH), lambda b, t: (b * nt + t, COL_AQ)),
                  kv(COL_AK, 2), kv(COL_AK, 1), kv(COL_AK, 0),
                  kv(COL_AV, 2), kv(COL_AV, 1), kv(COL_AV, 0),
                  pl.BlockSpec((AT_HEADS, AT_QB, AT_KB), lambda b, t: (0, 0, 0))],
        out_specs=pl.BlockSpec((AT_QB, AT_WIDTH), lambda b, t: (b * nt + t, 0)),
        out_shape=jax.ShapeDtypeStruct((n, AT_WIDTH), BF16),
        compiler_params=_cparams(("parallel", "parallel")),
        name="band_attention",
    )(proj, proj, proj, proj, proj, proj, proj, bias)


def _bias_kernel(f_ref, o_ref):
    width = f_ref.shape[1]
    qi = lax.broadcasted_iota(I32, (AT_QB, AT_KB), 0) // CHUNK
    ki = lax.broadcasted_iota(I32, (AT_QB, AT_KB), 1) // CHUNK
    band = (ki >= qi) & (ki <= qi + LEFT_CHUNKS)
    for h in range(AT_HEADS):
        base = jnp.broadcast_to(f_ref[h:h + 1, :], (AT_QB, width))
        rolled = pltpu.roll(base, width - (AT_QB - 1), 1, stride=1, stride_axis=0)
        o_ref[h] = jnp.where(band, rolled[:, :AT_KB], NEG_INF)


def _attention_bias(rel_table):
    table = rel_table.astype(F32)
    n_clipped = (AT_KB - 1) - REL_CLIP
    f = jnp.concatenate([jnp.broadcast_to(table[:, :1], (AT_HEADS, n_clipped)),
                         table[:, :AT_QB + REL_CLIP + 1]], axis=1)
    return pl.pallas_call(
        _bias_kernel,
        out_shape=jax.ShapeDtypeStruct((AT_HEADS, AT_QB, AT_KB), F32),
        name="attn_bias",
    )(f)


def _layer_norm(y, g, b):
    mu = jnp.mean(y, axis=-1, keepdims=True)
    d = y - mu
    var = jnp.mean(d * d, axis=-1, keepdims=True)
    return d * lax.rsqrt(var + LN_EPS) * g + b


def _load_row_tiles(ref):
    m = ref.shape[0] // ROW_TILES
    return jnp.concatenate([ref[pl.ds(j, m, stride=ROW_TILES), :] for j in range(ROW_TILES)],
                           axis=1)


def _store_row_tiles(ref, x):
    m = x.shape[0]
    for j in range(ROW_TILES):
        ref[pl.ds(j, m, stride=ROW_TILES), :] = x[:, j * LANES:(j + 1) * LANES]


def _merge_kernel(dn_ref, at_ref, ga_ref, gb_ref, x_ref, wa_ref, wb_ref, wo_ref, g_ref, b_ref,
                  x1t_ref):
    ya = jnp.dot(dn_ref[...], wa_ref[...], preferred_element_type=F32)
    yb = jnp.dot(at_ref[...], wb_ref[...], preferred_element_type=F32)
    mix = _sigmoid(ga_ref[...].astype(F32)) * ya + _sigmoid(gb_ref[...].astype(F32)) * yb
    y = ALPHA * x_ref[...] + jnp.dot(mix.astype(BF16), wo_ref[...], preferred_element_type=F32)
    _store_row_tiles(x1t_ref, _layer_norm(y, g_ref[...], b_ref[...]))


def _merge(dn, at, proj, x, wa, wb, wo, g, b, tm=512):
    n = x.shape[0]
    return pl.pallas_call(
        _merge_kernel,
        grid=(n // tm,),
        in_specs=[pl.BlockSpec((tm, DN_WIDTH), lambda i: (i, 0)),
                  pl.BlockSpec((tm, AT_WIDTH), lambda i: (i, 0)),
                  pl.BlockSpec((tm, D_MODEL), lambda i: (i, COL_GA)),
                  pl.BlockSpec((tm, D_MODEL), lambda i: (i, COL_GB)),
                  pl.BlockSpec((tm, D_MODEL), lambda i: (i, 0)),
                  pl.BlockSpec((DN_WIDTH, D_MODEL), lambda i: (0, 0)),
                  pl.BlockSpec((AT_WIDTH, D_MODEL), lambda i: (0, 0)),
                  pl.BlockSpec((D_MODEL, D_MODEL), lambda i: (0, 0)),
                  pl.BlockSpec((1, D_MODEL), lambda i: (0, 0)),
                  pl.BlockSpec((1, D_MODEL), lambda i: (0, 0))],
        out_specs=pl.BlockSpec((tm * ROW_TILES, LANES), lambda i: (i, 0)),
        out_shape=jax.ShapeDtypeStruct((n * ROW_TILES, LANES), F32),
        compiler_params=_cparams(("parallel",)),
        name="merge_ln1",
    )(dn, at, proj, proj, x, wa, wb, wo, g, b)


def _router_kernel(xt_ref, rw_ref, rb_ref, dest_ref, wt_ref, seg_ref):
    gt, ne = GROUP, N_EXPERTS
    logits = lax.dot_general(rw_ref[...], _load_row_tiles(xt_ref), (((1,), (1,)), ((), ())),
                             preferred_element_type=F32, precision=lax.Precision.HIGHEST)
    scores = _sigmoid(logits)
    sel = scores + rb_ref[:, 0:1]
    eidx = lax.broadcasted_iota(I32, (ne, gt), 0)
    picked = jnp.zeros((ne, gt), jnp.bool_)
    hits, vals = [], []
    for _ in range(TOP_K):
        m = jnp.max(sel, axis=0, keepdims=True)
        idx = jnp.min(jnp.where(sel == m, eidx, ne), axis=0, keepdims=True)
        hit = eidx == idx
        hits.append(hit)
        vals.append(jnp.sum(jnp.where(hit, scores, 0.0), axis=0, keepdims=True))
        sel = jnp.where(hit, -jnp.inf, sel)
        picked = picked | hit
    total = vals[0]
    for v in vals[1:]:
        total = total + v

    onehot = picked.astype(BF16)
    tr = lax.broadcasted_iota(I32, (gt, gt), 0)
    tc = lax.broadcasted_iota(I32, (gt, gt), 1)
    rank = jnp.dot(onehot, (tr < tc).astype(BF16), preferred_element_type=F32)
    count = jnp.sum(picked.astype(F32), axis=1, keepdims=True)
    tiles = jnp.floor((count + (SUBLANES - 1)) * (1.0 / SUBLANES))
    er = lax.broadcasted_iota(I32, (ne, ne), 0)
    ec = lax.broadcasted_iota(I32, (ne, ne), 1)
    start = SUBLANES * jnp.dot((ec < er).astype(BF16),
                               jnp.broadcast_to(tiles, (ne, LANES)).astype(BF16),
                               preferred_element_type=F32)
    slot = start[:, 0:1] + rank

    sub = lax.broadcasted_iota(I32, (SUBLANES, gt), 0)
    dest = jnp.zeros((SUBLANES, gt), F32)
    wts = jnp.zeros((SUBLANES, gt), F32)
    for kk in range(TOP_K):
        d = jnp.sum(jnp.where(hits[kk], slot, 0.0), axis=0, keepdims=True)
        dest = jnp.where(sub == kk, d, dest)
        wts = jnp.where(sub == kk, vals[kk] / total * ROUTED_SCALE, wts)
    dest_ref[0] = dest.astype(I32) * ROW_TILES
    wt_ref[0] = wts
    lane = lax.broadcasted_iota(I32, (ne, LANES), 1)
    seg_ref[0] = jnp.where(lane == 0, start, jnp.broadcast_to(count, (ne, LANES))).astype(I32)


def _router(x1t, rw_t, rb):
    n = x1t.shape[0] // ROW_TILES
    ng = n // GROUP
    return pl.pallas_call(
        _router_kernel,
        grid=(ng,),
        in_specs=[pl.BlockSpec((GROUP * ROW_TILES, LANES), lambda g: (g, 0)),
                  pl.BlockSpec((N_EXPERTS, D_MODEL), lambda g: (0, 0)),
                  pl.BlockSpec((N_EXPERTS, LANES), lambda g: (0, 0))],
        out_specs=[pl.BlockSpec((1, SUBLANES, GROUP), lambda g: (g, 0, 0)),
                   pl.BlockSpec((1, SUBLANES, GROUP), lambda g: (g, 0, 0)),
                   pl.BlockSpec((1, N_EXPERTS, LANES), lambda g: (g, 0, 0))],
        out_shape=[jax.ShapeDtypeStruct((ng, SUBLANES, GROUP), I32),
                   jax.ShapeDtypeStruct((ng, SUBLANES, GROUP), F32),
                   jax.ShapeDtypeStruct((ng, N_EXPERTS, LANES), I32)],
        compiler_params=_cparams(("parallel",)),
        name="router",
    )(x1t, rw_t, rb)


def _moe_kernel(start_ref, count_ref, xt_ref, dest_hbm, wt_hbm, wgu_hbm, wd_hbm, out_ref,
                buf, dest_s, wt_s, sem, wgu_buf, wd_buf, wsem, *, first_expert, n_groups):
    g = pl.program_id(0)
    e = pl.program_id(1)
    ne, gt = N_EXPERTS, GROUP
    n_steps = N_EXPERTS // MOE_EPS
    n_blocks_total = n_groups * n_steps
    blk = g * n_steps + e
    slot = blk % MOE_WEIGHT_SLOTS

    def routing_copies():
        return (pltpu.make_async_copy(dest_hbm.at[g], dest_s, sem.at[0]),
                pltpu.make_async_copy(wt_hbm.at[g], wt_s, sem.at[1]))

    def weight_copies(b, s):
        first = first_expert + (b % n_steps) * MOE_EPS
        return (pltpu.make_async_copy(wgu_hbm.at[pl.ds(first, MOE_EPS)], wgu_buf.at[s],
                                      wsem.at[s, 0]),
                pltpu.make_async_copy(wd_hbm.at[pl.ds(first, MOE_EPS)], wd_buf.at[s],
                                      wsem.at[s, 1]))

    @pl.when(blk == 0)
    def _():
        for b in range(min(MOE_WEIGHT_SLOTS, n_blocks_total)):
            for cp in weight_copies(b, b):
                cp.start()

    def tile_of(row8):
        return pl.ds(pl.multiple_of(row8, ROW_TILES), ROW_TILES)

    @pl.when(e == 0)
    def _():
        for cp in routing_copies():
            cp.start()
        zeros = jnp.zeros((SUBLANES * ROW_TILES, LANES), F32)

        def zero_pad(ee, carry):
            s = start_ref[g * ne + ee]
            cnt = count_ref[g * ne + ee]
            z0 = pl.multiple_of((s + (cnt // SUBLANES) * SUBLANES) * ROW_TILES, SUBLANES * ROW_TILES)
            buf[pl.ds(z0, SUBLANES * ROW_TILES), :] = zeros
            return carry

        lax.fori_loop(0, ne, zero_pad, 0)
        last = g * ne + ne - 1
        end = start_ref[last] + (count_ref[last] + SUBLANES - 1) // SUBLANES * SUBLANES
        end = pl.multiple_of(end * ROW_TILES, SUBLANES * ROW_TILES)
        buf[pl.ds(end, FFN_BM * ROW_TILES), :] = jnp.zeros((FFN_BM * ROW_TILES, LANES), F32)
        for cp in routing_copies():
            cp.wait()

        def scatter(t0, carry):
            for u in range(ROW_LOOP_UNROLL):
                tok = t0 * ROW_LOOP_UNROLL + u
                row = xt_ref[tile_of(tok * ROW_TILES), :]
                for kk in range(TOP_K):
                    buf[tile_of(dest_s[kk * gt + tok]), :] = row
            return carry

        lax.fori_loop(0, gt // ROW_LOOP_UNROLL, scatter, 0)

    experts = list(range(MOE_EPS))
    seg = [(start_ref[g * ne + e * MOE_EPS + k], count_ref[g * ne + e * MOE_EPS + k])
           for k in experts]
    n_blocks = [(cnt + FFN_BM - 1) // FFN_BM for _, cnt in seg]
    last_block_row = GROUP_ROWS - FFN_BM

    def ffn_block(i, carry):
        r0 = [pl.multiple_of(jnp.minimum(s + i * FFN_BM, last_block_row), SUBLANES) for s, _ in seg]
        base = [pl.multiple_of(r * ROW_TILES, SUBLANES * ROW_TILES) for r in r0]
        xin = [jnp.concatenate([buf[pl.ds(base[k] + j, FFN_BM, stride=ROW_TILES), :]
                                for j in range(ROW_TILES)], axis=1) for k in experts]
        gu = [jnp.dot(xin[k].astype(BF16), wgu_buf[slot, k], preferred_element_type=F32)
              for k in experts]
        hidden = [(_silu(gu[k][:, :EXPERT_DIM]) * gu[k][:, EXPERT_DIM:]).astype(BF16)
                  for k in experts]
        y = [jnp.dot(hidden[k], wd_buf[slot, k], preferred_element_type=F32) for k in experts]
        for k in experts:
            s, cnt = seg[k]
            keep = (r0[k] + lax.broadcasted_iota(I32, (FFN_BM, 1), 0)) < s + cnt
            for j in range(ROW_TILES):
                buf[pl.ds(base[k] + j, FFN_BM, stride=ROW_TILES), :] = jnp.where(
                    keep, y[k][:, j * LANES:(j + 1) * LANES], xin[k][:, j * LANES:(j + 1) * LANES])
        return carry

    total_blocks = n_blocks[0]
    for nb in n_blocks[1:]:
        total_blocks = jnp.maximum(total_blocks, nb)
    for cp in weight_copies(blk, slot):
        cp.wait()
    lax.fori_loop(0, total_blocks, ffn_block, 0)

    @pl.when(blk + MOE_WEIGHT_SLOTS < n_blocks_total)
    def _():
        for cp in weight_copies(blk + MOE_WEIGHT_SLOTS, slot):
            cp.start()

    @pl.when(e == pl.num_programs(1) - 1)
    def _():
        def combine(t0, carry):
            for u in range(ROW_LOOP_UNROLL):
                tok = t0 * ROW_LOOP_UNROLL + u
                acc = buf[tile_of(dest_s[tok]), :] * wt_s[tok]
                for kk in range(1, TOP_K):
                    acc = acc + buf[tile_of(dest_s[kk * gt + tok]), :] * wt_s[kk * gt + tok]
                out_ref[tile_of(tok * ROW_TILES), :] = acc
            return carry

        lax.fori_loop(0, gt // ROW_LOOP_UNROLL, combine, 0)


def _moe(x1t, dest, wts, starts, counts, wgu_all, wd_all, layer):
    n = x1t.shape[0] // ROW_TILES
    ng = n // GROUP
    grid_spec = pltpu.PrefetchScalarGridSpec(
        num_scalar_prefetch=2,
        grid=(ng, N_EXPERTS // MOE_EPS),
        in_specs=[pl.BlockSpec((GROUP * ROW_TILES, LANES), lambda g, e, *_: (g, 0),
                               pipeline_mode=pl.Buffered(1)),
                  pl.BlockSpec(memory_space=pl.ANY),
                  pl.BlockSpec(memory_space=pl.ANY),
                  pl.BlockSpec(memory_space=pl.ANY),
                  pl.BlockSpec(memory_space=pl.ANY)],
        out_specs=pl.BlockSpec((GROUP * ROW_TILES, LANES), lambda g, e, *_: (g, 0),
                               pipeline_mode=pl.Buffered(1)),
        scratch_shapes=[pltpu.VMEM((GROUP_ROWS * ROW_TILES, LANES), F32),
                        pltpu.SMEM((SUBLANES * GROUP,), I32),
                        pltpu.SMEM((SUBLANES * GROUP,), F32),
                        pltpu.SemaphoreType.DMA((2,)),
                        pltpu.VMEM((MOE_WEIGHT_SLOTS, MOE_EPS, D_MODEL, 2 * EXPERT_DIM), BF16),
                        pltpu.VMEM((MOE_WEIGHT_SLOTS, MOE_EPS, EXPERT_DIM, D_MODEL), BF16),
                        pltpu.SemaphoreType.DMA((MOE_WEIGHT_SLOTS, 2))],
    )
    return pl.pallas_call(
        functools.partial(_moe_kernel, first_expert=layer * N_EXPERTS, n_groups=ng),
        grid_spec=grid_spec,
        out_shape=jax.ShapeDtypeStruct((n * ROW_TILES, LANES), F32),
        compiler_params=_cparams(("arbitrary", "arbitrary")),
        name="routed_experts",
    )(starts, counts, x1t, dest, wts, wgu_all, wd_all)


def _final_kernel(x1t_ref, rt_ref, wsg_ref, wsd_ref, g_ref, b_ref, o_ref):
    x1 = _load_row_tiles(x1t_ref)
    gus = jnp.dot(x1.astype(BF16), wsg_ref[...], preferred_element_type=F32)
    hidden = _silu(gus[:, :SHARED_DIM]) * gus[:, SHARED_DIM:]
    shared = jnp.dot(hidden.astype(BF16), wsd_ref[...], preferred_element_type=F32)
    y = ALPHA * x1 + (_load_row_tiles(rt_ref) + shared)
    o_ref[...] = _layer_norm(y, g_ref[...], b_ref[...])


def _final(x1t, routed_t, wsg, wsd, g, b, tm=512):
    n = x1t.shape[0] // ROW_TILES
    return pl.pallas_call(
        _final_kernel,
        grid=(n // tm,),
        in_specs=[pl.BlockSpec((tm * ROW_TILES, LANES), lambda i: (i, 0)),
                  pl.BlockSpec((tm * ROW_TILES, LANES), lambda i: (i, 0)),
                  pl.BlockSpec((D_MODEL, 2 * SHARED_DIM), lambda i: (0, 0)),
                  pl.BlockSpec((SHARED_DIM, D_MODEL), lambda i: (0, 0)),
                  pl.BlockSpec((1, D_MODEL), lambda i: (0, 0)),
                  pl.BlockSpec((1, D_MODEL), lambda i: (0, 0))],
        out_specs=pl.BlockSpec((tm, D_MODEL), lambda i: (i, 0)),
        out_shape=jax.ShapeDtypeStruct((n, D_MODEL), F32),
        compiler_params=_cparams(("parallel",)),
        name="shared_ln2",
    )(x1t, routed_t, wsg, wsd, g, b)


C_SMALL = 4 * DN_WIDTH
C_ATT = C_SMALL + 2 * DN_HEADS
C_GATE = C_ATT + 3 * AT_WIDTH


def _layer(l, x, batch, seq, w_big_all, wgu_all, wd_all, w_in, conv_w, a_log, dt_bias,
           dn_norm_w, rel_table, w_branch_a, w_branch_b, w_out, ln1_g, ln1_b, router_w,
           router_bias, ws_gate_up, ws_down, ln2_g, ln2_b):
    w_small = w_in[:, C_SMALL:C_ATT]
    w_col = jnp.pad(w_small, ((0, 0), (0, LANES - 2 * DN_HEADS))).astype(BF16)
    w_row = w_small.T.astype(BF16)
    decay_rate = jnp.exp(a_log.astype(F32))
    zeros8 = jnp.zeros((DN_HEADS,), F32)
    p_col = jnp.pad(jnp.stack([jnp.concatenate([zeros8, decay_rate]),
                               jnp.concatenate([zeros8, dt_bias.astype(F32)])]),
                    ((0, 0), (0, LANES - 2 * DN_HEADS)))
    p_row = jnp.pad(p_col[:, :2 * DN_HEADS].T, ((0, 0), (0, LANES - 2)))

    proj = _matmul(x, w_big_all, l, BF16, tm=1024, tn=1280)
    cg, rg = _gates(x, w_col, w_row, p_col, p_row)
    dn = _deltanet(proj, cg, rg, conv_w.astype(F32), dn_norm_w.astype(F32).reshape(1, DN_HEAD_DIM),
                   batch, seq)
    at = _attention(proj, _attention_bias(rel_table), batch, seq)
    x1t = _merge(dn, at, proj, x, w_branch_a.astype(BF16), w_branch_b.astype(BF16),
                 w_out.astype(BF16), ln1_g.reshape(1, -1), ln1_b.reshape(1, -1))
    dest, wts, seg = _router(x1t, router_w.T.astype(F32),
                             jnp.broadcast_to(router_bias.astype(F32)[:, None], (N_EXPERTS, LANES)))
    ng = dest.shape[0]
    routed_t = _moe(x1t, dest.reshape(ng, SUBLANES * GROUP), wts.reshape(ng, SUBLANES * GROUP),
                    seg[:, :, 0].reshape(-1), seg[:, :, 1].reshape(-1), wgu_all, wd_all, l)
    return _final(x1t, routed_t, ws_gate_up.astype(BF16), ws_down.astype(BF16),
                  ln2_g.reshape(1, -1), ln2_b.reshape(1, -1))


def kernel(x, w_in, conv_w, a_log, dt_bias, dn_norm_w, rel_table, w_branch_a, w_branch_b, w_out,
           ln1_g, ln1_b, router_w, router_bias, expert_w_gate_up, expert_w_down,
           shared_w_gate_up, shared_w_down, ln2_g, ln2_b):
    batch, seq, d = x.shape
    assert d == D_MODEL and seq % DN_TB == 0 and seq % AT_QB == 0 and (batch * seq) % GROUP == 0
    depth = w_in.shape[0]
    xf = x.reshape(batch * seq, d).astype(F32)
    w_big_all = jnp.concatenate([w_in[:, :, :C_SMALL], w_in[:, :, C_GATE:], w_in[:, :, C_ATT:C_GATE]],
                                axis=2).astype(BF16).reshape(depth * D_MODEL, BIG_COLS)
    wgu_all = expert_w_gate_up.astype(BF16).reshape(depth * N_EXPERTS, D_MODEL, 2 * EXPERT_DIM)
    wd_all = expert_w_down.astype(BF16).reshape(depth * N_EXPERTS, EXPERT_DIM, D_MODEL)
    for l in range(depth):
        xf = _layer(l, xf, batch, seq, w_big_all, wgu_all, wd_all, w_in[l], conv_w[l],
                        a_log[l], dt_bias[l], dn_norm_w[l], rel_table[l], w_branch_a[l],
                        w_branch_b[l], w_out[l], ln1_g[l], ln1_b[l], router_w[l], router_bias[l],
                        shared_w_gate_up[l], shared_w_down[l], ln2_g[l], ln2_b[l])
    return xf.reshape(batch, seq, d).astype(x.dtype)
```

```python
import functools

import jax
import jax.numpy as jnp
from jax import lax
from jax.experimental import pallas as pl
from jax.experimental.pallas import tpu as pltpu

F32 = jnp.float32
BF16 = jnp.bfloat16
I32 = jnp.int32

D_MODEL = 1024
DEPTH = 2
CHUNK = 64
DN_HEADS = 8
DN_HEAD_DIM = 128
DN_WIDTH = DN_HEADS * DN_HEAD_DIM
CONV_WIDTH = 4
AT_HEADS = 8
AT_HEAD_DIM = 64
AT_WIDTH = AT_HEADS * AT_HEAD_DIM
LEFT_CHUNKS = 8
REL_CLIP = 256
N_EXPERTS = 64
TOP_K = 6
EXPERT_DIM = 256
SHARED_DIM = 256
ROUTED_SCALE = 2.5
ALPHA = (2 * DEPTH) ** 0.25
LN_EPS = 1e-5
RMS_EPS = 1e-6
L2_EPS = 1e-6
NEG_INF = -1e30

LANES = 128
SUBLANES = 8
ROW_TILES = D_MODEL // LANES

DN_TB = 256
DN_SB = 128
DN_HEADS_PER_PASS = 8
AT_QB = 256
AT_KB = 3 * AT_QB
GROUP = 1024
FFN_BM = 128
MOE_EPS = 4
MOE_WEIGHT_SLOTS = 4
ROW_LOOP_UNROLL = 4
GROUP_ROWS = GROUP * TOP_K + N_EXPERTS * (SUBLANES - 1) + FFN_BM
GROUP_ROWS = -(-GROUP_ROWS // SUBLANES) * SUBLANES
VMEM_LIMIT = 58 * 1024 * 1024

COL_Q, COL_K, COL_V, COL_Z = 0, 1, 2, 3
COL_GA, COL_GB = 0, 1
COL_AQ, COL_AK, COL_AV = 4, 5, 6
HEAD_COLS = 4 * DN_WIDTH
TAIL_COLS = 2 * D_MODEL + 3 * AT_WIDTH


def _sigmoid(x):
    return 1.0 / (1.0 + jnp.exp(-x))


def _silu(x):
    return x * _sigmoid(x)


def _softplus(x):
    return jnp.maximum(x, 0.0) + jnp.log1p(jnp.exp(-jnp.abs(x)))


def _bdot(a, b):
    return jnp.dot(a.astype(BF16), b.astype(BF16), preferred_element_type=F32)


def _bdot_nt(a, b):
    return lax.dot_general(a.astype(BF16), b.astype(BF16), (((1,), (1,)), ((), ())),
                           preferred_element_type=F32)


def _cparams(sem):
    return pltpu.CompilerParams(dimension_semantics=sem, vmem_limit_bytes=VMEM_LIMIT)


def _mm_kernel(x_ref, w_ref, o_ref):
    o_ref[...] = jnp.dot(x_ref[...].astype(BF16), w_ref[...].astype(BF16),
                         preferred_element_type=F32).astype(o_ref.dtype)


def _matmul(x, w_all, layer, m, out_dtype, tm, tn):
    n, k = x.shape
    return pl.pallas_call(
        _mm_kernel,
        grid=(n // tm, m // tn),
        in_specs=[pl.BlockSpec((tm, k), lambda i, j: (i, 0)),
                  pl.BlockSpec((k, tn), lambda i, j: (layer, j))],
        out_specs=pl.BlockSpec((tm, tn), lambda i, j: (i, j)),
        out_shape=jax.ShapeDtypeStruct((n, m), out_dtype),
        compiler_params=_cparams(("parallel", "parallel")),
        name="inproj",
    )(x, w_all)


def _gates_kernel(x_ref, wc_ref, wr_ref, pc_ref, pr_ref, cg_ref, rg_ref, *, tm):
    x = x_ref[...].astype(BF16)
    r = lax.broadcasted_iota(I32, (tm, tm), 0)
    c = lax.broadcasted_iota(I32, (tm, tm), 1)
    same = (r // CHUNK) == (c // CHUNK)

    pc = jnp.dot(x, wc_ref[...], preferred_element_type=F32)
    lane = lax.broadcasted_iota(I32, (tm, LANES), 1)
    col = jnp.where(lane < DN_HEADS, _sigmoid(pc),
                    -pc_ref[0:1, :] * _softplus(pc + pc_ref[1:2, :]))
    lower = (same & (r >= c)).astype(F32)
    gcc = jnp.dot(lower, col, preferred_element_type=F32, precision=lax.Precision.HIGHEST)
    cg_ref[...] = jnp.where(lane < DN_HEADS, col, gcc)

    pr = lax.dot_general(wr_ref[...], x, (((1,), (1,)), ((), ())),
                         preferred_element_type=F32)
    sub = lax.broadcasted_iota(I32, (2 * DN_HEADS, tm), 0)
    row = jnp.where(sub < DN_HEADS, _sigmoid(pr),
                    -pr_ref[:, 0:1] * _softplus(pr + pr_ref[:, 1:2]))
    upper = (same & (r <= c)).astype(F32)
    gcr = jnp.dot(row, upper, preferred_element_type=F32, precision=lax.Precision.HIGHEST)
    rg_ref[...] = jnp.where(sub < DN_HEADS, row, gcr)


def _gates(xb, w_col, w_row, p_col, p_row, tm=256):
    n = xb.shape[0]
    return pl.pallas_call(
        functools.partial(_gates_kernel, tm=tm),
        grid=(n // tm,),
        in_specs=[pl.BlockSpec((tm, D_MODEL), lambda i: (i, 0)),
                  pl.BlockSpec((D_MODEL, LANES), lambda i: (0, 0)),
                  pl.BlockSpec((2 * DN_HEADS, D_MODEL), lambda i: (0, 0)),
                  pl.BlockSpec((2, LANES), lambda i: (0, 0)),
                  pl.BlockSpec((2 * DN_HEADS, LANES), lambda i: (0, 0))],
        out_specs=[pl.BlockSpec((tm, LANES), lambda i: (i, 0)),
                   pl.BlockSpec((2 * DN_HEADS, tm), lambda i: (0, i))],
        out_shape=[jax.ShapeDtypeStruct((n, LANES), F32),
                   jax.ShapeDtypeStruct((2 * DN_HEADS, n), F32)],
        compiler_params=_cparams(("parallel",)),
        name="dn_gates",
    )(xb, w_col, w_row, p_col, p_row)


def _dn_kernel(q_ref, k_ref, v_ref, z_ref, cg_ref, rg_ref, cw_ref, nw_ref, o_ref,
               xpad, s_ref, u_s, wq_s, qkk_s, vn_s, o_s, mask_s):
    tb, nch, hd = DN_TB, DN_TB // CHUNK, DN_HEAD_DIM
    t = pl.program_id(1)

    sb = DN_SB
    halves = tb // sb

    @pl.when(t == 0)
    def _():
        xpad[:, 0:SUBLANES, :] = jnp.zeros((3, SUBLANES, DN_WIDTH), F32)
        s_ref[...] = jnp.zeros_like(s_ref)
        r = lax.broadcasted_iota(I32, (sb, sb), 0)
        c = lax.broadcasted_iota(I32, (sb, sb), 1)
        same = (r // CHUNK) == (c // CHUNK)
        mask_s[0] = jnp.where(same & (r >= c), 0.0, NEG_INF)
        mask_s[1] = (same & (r > c)).astype(F32)
        mask_s[2] = ((r // 16) == (c // 16)).astype(F32)
        mask_s[3] = (r == c).astype(F32)

    @pl.when(t > 0)
    def _():
        xpad[:, 0:SUBLANES, :] = xpad[:, tb:tb + SUBLANES, :]

    xpad[0, SUBLANES:SUBLANES + tb, :] = q_ref[...].astype(F32)
    xpad[1, SUBLANES:SUBLANES + tb, :] = k_ref[...].astype(F32)
    xpad[2, SUBLANES:SUBLANES + tb, :] = v_ref[...].astype(F32)
    vn_s[...] = jnp.zeros_like(vn_s)

    lane_chunk = lax.broadcasted_iota(I32, (hd, sb), 1) // CHUNK
    ones = jnp.ones((hd, hd), BF16)

    def row_sum_sq(x):
        return jnp.dot((x * x).astype(BF16), ones, preferred_element_type=F32)

    def each(fn, *lists):
        return [fn(*args) for args in zip(*lists)]

    for h0 in range(0, DN_HEADS, DN_HEADS_PER_PASS):
        inst = [(h, half) for h in range(h0, h0 + DN_HEADS_PER_PASS) for half in range(halves)]

        def conv(idx, h, half):
            block = xpad[idx, half * sb:half * sb + sb + SUBLANES, h * hd:(h + 1) * hd]
            acc = None
            for i in range(CONV_WIDTH):
                col0 = idx * DN_WIDTH + h * hd
                back = CONV_WIDTH - 1 - i
                shifted = block if back == 0 else pltpu.roll(block, back, 0)
                term = shifted[SUBLANES:SUBLANES + sb, :] * cw_ref[i:i + 1, col0:col0 + hd]
                acc = term if acc is None else acc + term
            return _silu(acc)

        qh = [conv(0, h, half) for h, half in inst]
        kh = [conv(1, h, half) for h, half in inst]
        vh = [conv(2, h, half) for h, half in inst]
        qh = each(lambda x: x * lax.rsqrt(row_sum_sq(x) + L2_EPS) * (hd ** -0.5), qh)
        kh = each(lambda x: x * lax.rsqrt(row_sum_sq(x) + L2_EPS), kh)
        beta = [cg_ref[half * sb:(half + 1) * sb, h:h + 1] for h, half in inst]
        gc = [cg_ref[half * sb:(half + 1) * sb, DN_HEADS + h:DN_HEADS + h + 1] for h, half in inst]
        gcr = [rg_ref[DN_HEADS + h:DN_HEADS + h + 1, half * sb:(half + 1) * sb] for h, half in inst]
        dec = each(lambda a, b: jnp.exp((a - b) + mask_s[0]), gc, gcr)
        kb = each(lambda a, b: a * b, kh, beta)
        khb = each(lambda a: a.astype(BF16), kh)
        lmat = each(lambda a, b, d: _bdot_nt(a, b) * d * mask_s[1], kb, khb, dec)
        qk = each(lambda a, b, d: _bdot_nt(a, b) * d, qh, khb, dec)
        eye = mask_s[3]
        dmat = each(lambda a: a * mask_s[2], lmat)
        omat = each(lambda a, b: a - b, lmat, dmat)
        s1 = each(lambda a: eye - a, dmat)
        d2 = each(lambda a: _bdot(a, a), dmat)
        s2 = each(lambda a, b: a + _bdot(a, b), s1, d2)
        d4 = each(lambda a: _bdot(a, a), d2)
        s3 = each(lambda a, b: a + _bdot(a, b), s2, d4)
        d8 = each(lambda a: _bdot(a, a), d4)
        dinv = each(lambda a, b: a + _bdot(a, b), s3, d8)
        m1 = each(_bdot, dinv, omat)
        m2 = each(lambda a: _bdot(a, a), m1)
        im = each(lambda a: eye - a, m1)
        pm = each(lambda a, b: a + _bdot(a, b), im, m2)
        tinv = each(_bdot, pm, dinv)
        eg = each(jnp.exp, gc)
        rhs = each(lambda v, b, k, e: jnp.concatenate([v * b, k * e], axis=1), vh, beta, kb, eg)
        uw = each(_bdot, tinv, rhs)
        for i, (h, half) in enumerate(inst):
            rs = slice(half * sb, (half + 1) * sb)
            u_s[h, rs, :] = uw[i][:, :hd]
            w_b = uw[i][:, hd:].astype(BF16)
            qd_b = (qh[i] * eg[i]).astype(BF16)
            qk_b = qk[i].astype(BF16)
            gl = jnp.concatenate(
                [jnp.broadcast_to(gc[i][cc * CHUNK + CHUNK - 1:cc * CHUNK + CHUNK, :], (CHUNK, 1))
                 for cc in range(sb // CHUNK)], axis=0)
            kdt = (kh[i] * jnp.exp(gl - gc[i])).T
            for cc in range(sb // CHUNK):
                chunk = half * (sb // CHUNK) + cc
                cr = slice(cc * CHUNK, (cc + 1) * CHUNK)
                wq_s[h, chunk, 0:CHUNK, :] = w_b[cr, :]
                wq_s[h, chunk, CHUNK:2 * CHUNK, :] = qd_b[cr, :]
                qkk_s[h, chunk, 0:CHUNK, :] = qk_b[cr, :]
                qkk_s[h, chunk, CHUNK:CHUNK + hd, :] = jnp.where(lane_chunk == cc, kdt,
                                                                  0.0).astype(BF16)

    heads = list(range(DN_HEADS))

    def chunk_step(cc, carry):
        r0 = pl.multiple_of(cc * CHUNK, CHUNK)
        rows = pl.ds(r0, CHUNK)
        sub_rows = pl.ds(pl.multiple_of((cc // (sb // CHUNK)) * sb, sb), sb)
        s_old = [s_ref[h] for h in heads]
        s_b = each(lambda a: a.astype(BF16), s_old)
        by_state = [jnp.dot(wq_s[h, cc], s_b[h], preferred_element_type=F32) for h in heads]
        for h in heads:
            vn_s[h, rows, :] = (u_s[h, rows, :] - by_state[h][:CHUNK, :]).astype(BF16)
        vn_all = [vn_s[h, sub_rows, :] for h in heads]
        by_vnew = [jnp.dot(qkk_s[h, cc], vn_all[h], preferred_element_type=F32) for h in heads]
        for h in heads:
            g_last = cg_ref[pl.ds(r0 + CHUNK - 1, 1), DN_HEADS + h:DN_HEADS + h + 1]
            s_ref[h] = s_old[h] * jnp.exp(g_last) + by_vnew[h][CHUNK:, :]
            o_s[rows, h * hd:(h + 1) * hd] = by_state[h][CHUNK:, :] + by_vnew[h][:CHUNK, :]
        return carry

    lax.fori_loop(0, nch, chunk_step, 0)

    for h in range(DN_HEADS):
        sl = slice(h * hd, (h + 1) * hd)
        o = o_s[:, sl]
        o = o * lax.rsqrt(row_sum_sq(o) * (1.0 / hd) + RMS_EPS)
        z = z_ref[:, sl].astype(F32)
        o_ref[:, sl] = (o * nw_ref[...] * _silu(z)).astype(o_ref.dtype)


def _deltanet(proj, cg, rg, conv_w, norm_w, batch, seq):
    n = batch * seq
    nt = seq // DN_TB
    tb, hd, nch = DN_TB, DN_HEAD_DIM, DN_TB // CHUNK

    def col(j):
        return pl.BlockSpec((tb, DN_WIDTH), lambda b, t, j=j: (b * nt + t, j))

    return pl.pallas_call(
        _dn_kernel,
        grid=(batch, nt),
        in_specs=[col(COL_Q), col(COL_K), col(COL_V), col(COL_Z),
                  pl.BlockSpec((tb, LANES), lambda b, t: (b * nt + t, 0)),
                  pl.BlockSpec((2 * DN_HEADS, tb), lambda b, t: (0, b * nt + t)),
                  pl.BlockSpec((CONV_WIDTH, 3 * DN_WIDTH), lambda b, t: (0, 0)),
                  pl.BlockSpec((1, hd), lambda b, t: (0, 0))],
        out_specs=pl.BlockSpec((tb, DN_WIDTH), lambda b, t: (b * nt + t, 0)),
        out_shape=jax.ShapeDtypeStruct((n, DN_WIDTH), BF16),
        scratch_shapes=[
            pltpu.VMEM((3, tb + SUBLANES, DN_WIDTH), F32),
            pltpu.VMEM((DN_HEADS, hd, hd), F32),
            pltpu.VMEM((DN_HEADS, tb, hd), F32),
            pltpu.VMEM((DN_HEADS, nch, 2 * CHUNK, hd), BF16),
            pltpu.VMEM((DN_HEADS, nch, CHUNK + hd, DN_SB), BF16),
            pltpu.VMEM((DN_HEADS, tb, hd), BF16),
            pltpu.VMEM((tb, DN_WIDTH), F32),
            pltpu.VMEM((4, DN_SB, DN_SB), F32),
        ],
        compiler_params=_cparams(("parallel", "arbitrary")),
        name="deltanet",
    )(proj, proj, proj, proj, cg, rg, conv_w, norm_w)


def _attn_kernel(q_ref, k0_ref, k1_ref, k2_ref, v0_ref, v1_ref, v2_ref, bias_ref, o_ref):
    t = pl.program_id(1)
    q = q_ref[...]
    k = jnp.concatenate([k0_ref[...], k1_ref[...], k2_ref[...]], axis=0)
    v = jnp.concatenate([v0_ref[...], v1_ref[...], v2_ref[...]], axis=0)
    kidx = lax.broadcasted_iota(I32, (1, AT_KB), 1)
    key_valid = kidx >= (2 - t) * AT_QB
    lane_head = lax.broadcasted_iota(I32, (1, LANES), 1) // AT_HEAD_DIM
    scale = AT_HEAD_DIM ** -0.5
    def scores(head):
        p, mine = head // 2, lane_head == head % 2
        sl = slice(p * LANES, (p + 1) * LANES)
        qm = jnp.where(mine, q[:, sl] * scale, jnp.zeros((AT_QB, LANES), BF16))
        s = lax.dot_general(qm, k[:, sl], (((1,), (1,)), ((), ())), preferred_element_type=F32)
        return jnp.where(key_valid, s + bias_ref[head], NEG_INF)

    def weighted_values(head, s, out):
        p, mine = head // 2, lane_head == head % 2
        vp = v[:, p * LANES:(p + 1) * LANES]
        e = jnp.exp(s - jnp.max(s, axis=-1, keepdims=True))
        v_aug = jnp.where(mine, vp, jnp.ones_like(vp))
        pv = jnp.dot(e.astype(BF16), v_aug, preferred_element_type=F32)
        row_sum = pltpu.roll(pv, AT_HEAD_DIM, 1)
        return jnp.where(mine, pv / row_sum, out)

    outs = [jnp.zeros((AT_QB, LANES), F32)] * (AT_HEADS // 2)
    s_prev = scores(0)
    for head in range(AT_HEADS):
        s_next = scores(head + 1) if head + 1 < AT_HEADS else None
        outs[head // 2] = weighted_values(head, s_prev, outs[head // 2])
        s_prev = s_next
    for p in range(AT_HEADS // 2):
        o_ref[:, p * LANES:(p + 1) * LANES] = outs[p].astype(o_ref.dtype)


def _attention(proj, bias, batch, seq):
    n = batch * seq
    nt = seq // AT_QB

    def kv(col, back):
        return pl.BlockSpec((AT_QB, AT_WIDTH),
                            lambda b, t, col=col, back=back: (b * nt + jnp.maximum(t - back, 0), col))

    return pl.pallas_call(
        _attn_kernel,
        grid=(batch, nt),
        in_specs=[pl.BlockSpec((AT_QB, AT_WIDTH), lambda b, t: (b * nt + t, COL_AQ)),
                  kv(COL_AK, 2), kv(COL_AK, 1), kv(COL_AK, 0),
                  kv(COL_AV, 2), kv(COL_AV, 1), kv(COL_AV, 0),
                  pl.BlockSpec((AT_HEADS, AT_QB, AT_KB), lambda b, t: (0, 0, 0))],
        out_specs=pl.BlockSpec((AT_QB, AT_WIDTH), lambda b, t: (b * nt + t, 0)),
        out_shape=jax.ShapeDtypeStruct((n, AT_WIDTH), BF16),
        compiler_params=_cparams(("parallel", "parallel")),
        name="band_attention",
    )(proj, proj, proj, proj, proj, proj, proj, bias)


def _bias_kernel(f_ref, o_ref):
    width = f_ref.shape[1]
    qi = lax.broadcasted_iota(I32, (AT_QB, AT_KB), 0) // CHUNK
    ki = lax.broadcasted_iota(I32, (AT_QB, AT_KB), 1) // CHUNK
    band = (ki >= qi) & (ki <= qi + LEFT_CHUNKS)
    for h in range(AT_HEADS):
        base = jnp.broadcast_to(f_ref[h:h + 1, :], (AT_QB, width))
        rolled = pltpu.roll(base, width - (AT_QB - 1), 1, stride=1, stride_axis=0)
        o_ref[h] = jnp.where(band, rolled[:, :AT_KB], NEG_INF)


def _attention_bias(rel_table):
    table = rel_table.astype(F32)
    n_clipped = (AT_KB - 1) - REL_CLIP
    f = jnp.concatenate([jnp.broadcast_to(table[:, :1], (AT_HEADS, n_clipped)),
                         table[:, :AT_QB + REL_CLIP + 1]], axis=1)
    return pl.pallas_call(
        _bias_kernel,
        out_shape=jax.ShapeDtypeStruct((AT_HEADS, AT_QB, AT_KB), F32),
        name="attn_bias",
    )(f)


def _layer_norm(y, g, b):
    mu = jnp.mean(y, axis=-1, keepdims=True)
    d = y - mu
    var = jnp.mean(d * d, axis=-1, keepdims=True)
    return d * lax.rsqrt(var + LN_EPS) * g + b


def _load_row_tiles(ref):
    m = ref.shape[0] // ROW_TILES
    return jnp.concatenate([ref[pl.ds(j, m, stride=ROW_TILES), :] for j in range(ROW_TILES)],
                           axis=1)


def _store_row_tiles(ref, x):
    m = x.shape[0]
    for j in range(ROW_TILES):
        ref[pl.ds(j, m, stride=ROW_TILES), :] = x[:, j * LANES:(j + 1) * LANES]


def _merge_kernel(dn_ref, at_ref, ga_ref, gb_ref, x_ref, wa_ref, wb_ref, wo_ref, g_ref, b_ref,
                  x1t_ref):
    ya = jnp.dot(dn_ref[...], wa_ref[...], preferred_element_type=F32)
    yb = jnp.dot(at_ref[...], wb_ref[...], preferred_element_type=F32)
    mix = _sigmoid(ga_ref[...].astype(F32)) * ya + _sigmoid(gb_ref[...].astype(F32)) * yb
    y = ALPHA * x_ref[...] + jnp.dot(mix.astype(BF16), wo_ref[...], preferred_element_type=F32)
    _store_row_tiles(x1t_ref, _layer_norm(y, g_ref[...], b_ref[...]))


def _merge(dn, at, proj, x, wa, wb, wo, g, b, tm=512):
    n = x.shape[0]
    return pl.pallas_call(
        _merge_kernel,
        grid=(n // tm,),
        in_specs=[pl.BlockSpec((tm, DN_WIDTH), lambda i: (i, 0)),
                  pl.BlockSpec((tm, AT_WIDTH), lambda i: (i, 0)),
                  pl.BlockSpec((tm, D_MODEL), lambda i: (i, COL_GA)),
                  pl.BlockSpec((tm, D_MODEL), lambda i: (i, COL_GB)),
                  pl.BlockSpec((tm, D_MODEL), lambda i: (i, 0)),
                  pl.BlockSpec((DN_WIDTH, D_MODEL), lambda i: (0, 0)),
                  pl.BlockSpec((AT_WIDTH, D_MODEL), lambda i: (0, 0)),
                  pl.BlockSpec((D_MODEL, D_MODEL), lambda i: (0, 0)),
                  pl.BlockSpec((1, D_MODEL), lambda i: (0, 0)),
                  pl.BlockSpec((1, D_MODEL), lambda i: (0, 0))],
        out_specs=pl.BlockSpec((tm * ROW_TILES, LANES), lambda i: (i, 0)),
        out_shape=jax.ShapeDtypeStruct((n * ROW_TILES, LANES), F32),
        compiler_params=_cparams(("parallel",)),
        name="merge_ln1",
    )(dn, at, proj, proj, x, wa, wb, wo, g, b)


def _router_kernel(xt_ref, rw_ref, rb_ref, dest_ref, wt_ref, seg_ref):
    gt, ne = GROUP, N_EXPERTS
    logits = lax.dot_general(rw_ref[...], _load_row_tiles(xt_ref), (((1,), (1,)), ((), ())),
                             preferred_element_type=F32, precision=lax.Precision.HIGHEST)
    scores = _sigmoid(logits)
    sel = scores + rb_ref[:, 0:1]
    eidx = lax.broadcasted_iota(I32, (ne, gt), 0)
    picked = jnp.zeros((ne, gt), jnp.bool_)
    hits, vals = [], []
    for _ in range(TOP_K):
        m = jnp.max(sel, axis=0, keepdims=True)
        idx = jnp.min(jnp.where(sel == m, eidx, ne), axis=0, keepdims=True)
        hit = eidx == idx
        hits.append(hit)
        vals.append(jnp.sum(jnp.where(hit, scores, 0.0), axis=0, keepdims=True))
        sel = jnp.where(hit, -jnp.inf, sel)
        picked = picked | hit
    total = vals[0]
    for v in vals[1:]:
        total = total + v

    onehot = picked.astype(BF16)
    tr = lax.broadcasted_iota(I32, (gt, gt), 0)
    tc = lax.broadcasted_iota(I32, (gt, gt), 1)
    rank = jnp.dot(onehot, (tr < tc).astype(BF16), preferred_element_type=F32)
    count = jnp.sum(picked.astype(F32), axis=1, keepdims=True)
    tiles = jnp.floor((count + (SUBLANES - 1)) * (1.0 / SUBLANES))
    er = lax.broadcasted_iota(I32, (ne, ne), 0)
    ec = lax.broadcasted_iota(I32, (ne, ne), 1)
    start = SUBLANES * jnp.dot((ec < er).astype(BF16),
                               jnp.broadcast_to(tiles, (ne, LANES)).astype(BF16),
                               preferred_element_type=F32)
    slot = start[:, 0:1] + rank

    sub = lax.broadcasted_iota(I32, (SUBLANES, gt), 0)
    dest = jnp.zeros((SUBLANES, gt), F32)
    wts = jnp.zeros((SUBLANES, gt), F32)
    for kk in range(TOP_K):
        d = jnp.sum(jnp.where(hits[kk], slot, 0.0), axis=0, keepdims=True)
        dest = jnp.where(sub == kk, d, dest)
        wts = jnp.where(sub == kk, vals[kk] / total * ROUTED_SCALE, wts)
    dest_ref[0] = dest.astype(I32) * ROW_TILES
    wt_ref[0] = wts
    lane = lax.broadcasted_iota(I32, (ne, LANES), 1)
    seg_ref[0] = jnp.where(lane == 0, start, jnp.broadcast_to(count, (ne, LANES))).astype(I32)


def _router(x1t, rw_t, rb):
    n = x1t.shape[0] // ROW_TILES
    ng = n // GROUP
    return pl.pallas_call(
        _router_kernel,
        grid=(ng,),
        in_specs=[pl.BlockSpec((GROUP * ROW_TILES, LANES), lambda g: (g, 0)),
                  pl.BlockSpec((N_EXPERTS, D_MODEL), lambda g: (0, 0)),
                  pl.BlockSpec((N_EXPERTS, LANES), lambda g: (0, 0))],
        out_specs=[pl.BlockSpec((1, SUBLANES, GROUP), lambda g: (g, 0, 0)),
                   pl.BlockSpec((1, SUBLANES, GROUP), lambda g: (g, 0, 0)),
                   pl.BlockSpec((1, N_EXPERTS, LANES), lambda g: (g, 0, 0))],
        out_shape=[jax.ShapeDtypeStruct((ng, SUBLANES, GROUP), I32),
                   jax.ShapeDtypeStruct((ng, SUBLANES, GROUP), F32),
                   jax.ShapeDtypeStruct((ng, N_EXPERTS, LANES), I32)],
        compiler_params=_cparams(("parallel",)),
        name="router",
    )(x1t, rw_t, rb)


def _moe_kernel(start_ref, count_ref, xt_hbm, dest_hbm, wt_hbm, wgu_hbm, wd_hbm, out_ref,
                buf, dest_s, wt_s, sem, wgu_buf, wd_buf, wsem, *, first_expert, n_groups):
    g = pl.program_id(0)
    e = pl.program_id(1)
    ne, gt = N_EXPERTS, GROUP
    n_steps = N_EXPERTS // MOE_EPS
    n_blocks_total = n_groups * n_steps
    blk = g * n_steps + e
    slot = blk % MOE_WEIGHT_SLOTS

    def routing_copies():
        rows = pl.ds(pl.multiple_of(g * (gt * ROW_TILES), gt * ROW_TILES), gt * ROW_TILES)
        return (pltpu.make_async_copy(dest_hbm.at[g], dest_s, sem.at[0]),
                pltpu.make_async_copy(wt_hbm.at[g], wt_s, sem.at[1]),
                pltpu.make_async_copy(xt_hbm.at[rows], out_ref, sem.at[2]))

    def weight_copies(b, s):
        first = first_expert + (b % n_steps) * MOE_EPS
        return (pltpu.make_async_copy(wgu_hbm.at[pl.ds(first, MOE_EPS)], wgu_buf.at[s],
                                      wsem.at[s, 0]),
                pltpu.make_async_copy(wd_hbm.at[pl.ds(first, MOE_EPS)], wd_buf.at[s],
                                      wsem.at[s, 1]))

    @pl.when(blk == 0)
    def _():
        for b in range(min(MOE_WEIGHT_SLOTS, n_blocks_total)):
            for cp in weight_copies(b, b):
                cp.start()

    def tile_of(row8):
        return pl.ds(pl.multiple_of(row8, ROW_TILES), ROW_TILES)

    @pl.when(e == 0)
    def _():
        for cp in routing_copies():
            cp.start()
        zeros = jnp.zeros((SUBLANES * ROW_TILES, LANES), F32)

        def zero_pad(ee, carry):
            s = start_ref[g * ne + ee]
            cnt = count_ref[g * ne + ee]
            z0 = pl.multiple_of((s + (cnt // SUBLANES) * SUBLANES) * ROW_TILES, SUBLANES * ROW_TILES)
            buf[pl.ds(z0, SUBLANES * ROW_TILES), :] = zeros
            return carry

        lax.fori_loop(0, ne, zero_pad, 0)
        last = g * ne + ne - 1
        end = start_ref[last] + (count_ref[last] + SUBLANES - 1) // SUBLANES * SUBLANES
        end = pl.multiple_of(end * ROW_TILES, SUBLANES * ROW_TILES)
        buf[pl.ds(end, FFN_BM * ROW_TILES), :] = jnp.zeros((FFN_BM * ROW_TILES, LANES), F32)
        for cp in routing_copies():
            cp.wait()

        def scatter(t0, carry):
            for u in range(ROW_LOOP_UNROLL):
                tok = t0 * ROW_LOOP_UNROLL + u
                row = out_ref[tile_of(tok * ROW_TILES), :]
                for kk in range(TOP_K):
                    buf[tile_of(dest_s[kk * gt + tok]), :] = row
            return carry

        lax.fori_loop(0, gt // ROW_LOOP_UNROLL, scatter, 0)

    experts = list(range(MOE_EPS))
    seg = [(start_ref[g * ne + e * MOE_EPS + k], count_ref[g * ne + e * MOE_EPS + k])
           for k in experts]
    n_blocks = [(cnt + FFN_BM - 1) // FFN_BM for _, cnt in seg]
    last_block_row = GROUP_ROWS - FFN_BM

    def ffn_block(i, carry):
        r0 = [pl.multiple_of(jnp.minimum(s + i * FFN_BM, last_block_row), SUBLANES) for s, _ in seg]
        base = [pl.multiple_of(r * ROW_TILES, SUBLANES * ROW_TILES) for r in r0]
        xin = [jnp.concatenate([buf[pl.ds(base[k] + j, FFN_BM, stride=ROW_TILES), :]
                                for j in range(ROW_TILES)], axis=1) for k in experts]
        gu = [jnp.dot(xin[k].astype(BF16), wgu_buf[slot, k], preferred_element_type=F32)
              for k in experts]
        hidden = [(_silu(gu[k][:, :EXPERT_DIM]) * gu[k][:, EXPERT_DIM:]).astype(BF16)
                  for k in experts]
        y = [jnp.dot(hidden[k], wd_buf[slot, k], preferred_element_type=F32) for k in experts]
        for k in experts:
            s, cnt = seg[k]
            keep = (r0[k] + lax.broadcasted_iota(I32, (FFN_BM, 1), 0)) < s + cnt
            for j in range(ROW_TILES):
                buf[pl.ds(base[k] + j, FFN_BM, stride=ROW_TILES), :] = jnp.where(
                    keep, y[k][:, j * LANES:(j + 1) * LANES], xin[k][:, j * LANES:(j + 1) * LANES])
        return carry

    total_blocks = n_blocks[0]
    for nb in n_blocks[1:]:
        total_blocks = jnp.maximum(total_blocks, nb)
    for cp in weight_copies(blk, slot):
        cp.wait()
    lax.fori_loop(0, total_blocks, ffn_block, 0)

    @pl.when(blk + MOE_WEIGHT_SLOTS < n_blocks_total)
    def _():
        for cp in weight_copies(blk + MOE_WEIGHT_SLOTS, slot):
            cp.start()

    @pl.when(e == pl.num_programs(1) - 1)
    def _():
        def combine(t0, carry):
            for u in range(ROW_LOOP_UNROLL):
                tok = t0 * ROW_LOOP_UNROLL + u
                acc = buf[tile_of(dest_s[tok]), :] * wt_s[tok]
                for kk in range(1, TOP_K):
                    acc = acc + buf[tile_of(dest_s[kk * gt + tok]), :] * wt_s[kk * gt + tok]
                out_ref[tile_of(tok * ROW_TILES), :] = acc
            return carry

        lax.fori_loop(0, gt // ROW_LOOP_UNROLL, combine, 0)


def _moe(x1t, dest, wts, starts, counts, wgu_all, wd_all, layer):
    n = x1t.shape[0] // ROW_TILES
    ng = n // GROUP
    grid_spec = pltpu.PrefetchScalarGridSpec(
        num_scalar_prefetch=2,
        grid=(ng, N_EXPERTS // MOE_EPS),
        in_specs=[pl.BlockSpec(memory_space=pl.ANY)] * 5,
        out_specs=pl.BlockSpec((GROUP * ROW_TILES, LANES), lambda g, e, *_: (g, 0),
                               pipeline_mode=pl.Buffered(1)),
        scratch_shapes=[pltpu.VMEM((GROUP_ROWS * ROW_TILES, LANES), F32),
                        pltpu.SMEM((SUBLANES * GROUP,), I32),
                        pltpu.SMEM((SUBLANES * GROUP,), F32),
                        pltpu.SemaphoreType.DMA((3,)),
                        pltpu.VMEM((MOE_WEIGHT_SLOTS, MOE_EPS, D_MODEL, 2 * EXPERT_DIM), BF16),
                        pltpu.VMEM((MOE_WEIGHT_SLOTS, MOE_EPS, EXPERT_DIM, D_MODEL), BF16),
                        pltpu.SemaphoreType.DMA((MOE_WEIGHT_SLOTS, 2))],
    )
    return pl.pallas_call(
        functools.partial(_moe_kernel, first_expert=layer * N_EXPERTS, n_groups=ng),
        grid_spec=grid_spec,
        out_shape=jax.ShapeDtypeStruct((n * ROW_TILES, LANES), F32),
        compiler_params=_cparams(("arbitrary", "arbitrary")),
        name="routed_experts",
    )(starts, counts, x1t, dest, wts, wgu_all, wd_all)


def _final_kernel(x1t_ref, rt_ref, wsg_ref, wsd_ref, g_ref, b_ref, o_ref):
    x1 = _load_row_tiles(x1t_ref)
    gus = jnp.dot(x1.astype(BF16), wsg_ref[...], preferred_element_type=F32)
    hidden = _silu(gus[:, :SHARED_DIM]) * gus[:, SHARED_DIM:]
    shared = jnp.dot(hidden.astype(BF16), wsd_ref[...], preferred_element_type=F32)
    y = ALPHA * x1 + (_load_row_tiles(rt_ref) + shared)
    o_ref[...] = _layer_norm(y, g_ref[...], b_ref[...])


def _final(x1t, routed_t, wsg, wsd, g, b, tm=512):
    n = x1t.shape[0] // ROW_TILES
    return pl.pallas_call(
        _final_kernel,
        grid=(n // tm,),
        in_specs=[pl.BlockSpec((tm * ROW_TILES, LANES), lambda i: (i, 0)),
                  pl.BlockSpec((tm * ROW_TILES, LANES), lambda i: (i, 0)),
                  pl.BlockSpec((D_MODEL, 2 * SHARED_DIM), lambda i: (0, 0)),
                  pl.BlockSpec((SHARED_DIM, D_MODEL), lambda i: (0, 0)),
                  pl.BlockSpec((1, D_MODEL), lambda i: (0, 0)),
                  pl.BlockSpec((1, D_MODEL), lambda i: (0, 0))],
        out_specs=pl.BlockSpec((tm, D_MODEL), lambda i: (i, 0)),
        out_shape=jax.ShapeDtypeStruct((n, D_MODEL), F32),
        compiler_params=_cparams(("parallel",)),
        name="shared_ln2",
    )(x1t, routed_t, wsg, wsd, g, b)


C_SMALL = 4 * DN_WIDTH
C_ATT = C_SMALL + 2 * DN_HEADS
C_GATE = C_ATT + 3 * AT_WIDTH


def _layer(l, x, batch, seq, w_in_all, w_tail_all, wgu_all, wd_all, w_in, conv_w, a_log, dt_bias,
           dn_norm_w, rel_table, w_branch_a, w_branch_b, w_out, ln1_g, ln1_b, router_w,
           router_bias, ws_gate_up, ws_down, ln2_g, ln2_b):
    w_small = w_in[:, C_SMALL:C_ATT]
    w_col = jnp.pad(w_small, ((0, 0), (0, LANES - 2 * DN_HEADS))).astype(BF16)
    w_row = w_small.T.astype(BF16)
    decay_rate = jnp.exp(a_log.astype(F32))
    zeros8 = jnp.zeros((DN_HEADS,), F32)
    p_col = jnp.pad(jnp.stack([jnp.concatenate([zeros8, decay_rate]),
                               jnp.concatenate([zeros8, dt_bias.astype(F32)])]),
                    ((0, 0), (0, LANES - 2 * DN_HEADS)))
    p_row = jnp.pad(p_col[:, :2 * DN_HEADS].T, ((0, 0), (0, LANES - 2)))

    proj_dn = _matmul(x, w_in_all, l, HEAD_COLS, BF16, tm=1024, tn=1024)
    proj = _matmul(x, w_tail_all, l, TAIL_COLS, BF16, tm=1024, tn=TAIL_COLS // 2)
    cg, rg = _gates(x, w_col, w_row, p_col, p_row)
    dn = _deltanet(proj_dn, cg, rg, conv_w.astype(F32),
                   dn_norm_w.astype(F32).reshape(1, DN_HEAD_DIM), batch, seq)
    at = _attention(proj, _attention_bias(rel_table), batch, seq)
    x1t = _merge(dn, at, proj, x, w_branch_a.astype(BF16), w_branch_b.astype(BF16),
                 w_out.astype(BF16), ln1_g.reshape(1, -1), ln1_b.reshape(1, -1))
    dest, wts, seg = _router(x1t, router_w.T.astype(F32),
                             jnp.broadcast_to(router_bias.astype(F32)[:, None], (N_EXPERTS, LANES)))
    ng = dest.shape[0]
    routed_t = _moe(x1t, dest.reshape(ng, SUBLANES * GROUP), wts.reshape(ng, SUBLANES * GROUP),
                    seg[:, :, 0].reshape(-1), seg[:, :, 1].reshape(-1), wgu_all, wd_all, l)
    return _final(x1t, routed_t, ws_gate_up.astype(BF16), ws_down.astype(BF16),
                  ln2_g.reshape(1, -1), ln2_b.reshape(1, -1))


def kernel(x, w_in, conv_w, a_log, dt_bias, dn_norm_w, rel_table, w_branch_a, w_branch_b, w_out,
           ln1_g, ln1_b, router_w, router_bias, expert_w_gate_up, expert_w_down,
           shared_w_gate_up, shared_w_down, ln2_g, ln2_b):
    batch, seq, d = x.shape
    assert d == D_MODEL and seq % DN_TB == 0 and seq % AT_QB == 0 and (batch * seq) % GROUP == 0
    depth = w_in.shape[0]
    xf = x.reshape(batch * seq, d).astype(F32)
    w_in_all = w_in.reshape(depth * D_MODEL, w_in.shape[2])
    w_tail_all = jnp.concatenate([w_in[:, :, C_GATE:], w_in[:, :, C_ATT:C_GATE]],
                                 axis=2).astype(BF16).reshape(depth * D_MODEL, TAIL_COLS)
    wgu_all = expert_w_gate_up.astype(BF16).reshape(depth * N_EXPERTS, D_MODEL, 2 * EXPERT_DIM)
    wd_all = expert_w_down.astype(BF16).reshape(depth * N_EXPERTS, EXPERT_DIM, D_MODEL)
    for l in range(depth):
        xf = _layer(l, xf, batch, seq, w_in_all, w_tail_all, wgu_all, wd_all, w_in[l], conv_w[l],
                        a_log[l], dt_bias[l], dn_norm_w[l], rel_table[l], w_branch_a[l],
                        w_branch_b[l], w_out[l], ln1_g[l], ln1_b[l], router_w[l], router_bias[l],
                        shared_w_gate_up[l], shared_w_down[l], ln2_g[l], ln2_b[l])
    return xf.reshape(batch, seq, d).astype(x.dtype)
```

```python
import functools

import jax
import jax.numpy as jnp
from jax import lax
from jax.experimental import pallas as pl
from jax.experimental.pallas import tpu as pltpu

F32 = jnp.float32
BF16 = jnp.bfloat16
I32 = jnp.int32

D_MODEL = 1024
DEPTH = 2
CHUNK = 64
DN_HEADS = 8
DN_HEAD_DIM = 128
DN_WIDTH = DN_HEADS * DN_HEAD_DIM
CONV_WIDTH = 4
AT_HEADS = 8
AT_HEAD_DIM = 64
AT_WIDTH = AT_HEADS * AT_HEAD_DIM
LEFT_CHUNKS = 8
REL_CLIP = 256
N_EXPERTS = 64
TOP_K = 6
EXPERT_DIM = 256
SHARED_DIM = 256
ROUTED_SCALE = 2.5
ALPHA = (2 * DEPTH) ** 0.25
LN_EPS = 1e-5
RMS_EPS = 1e-6
L2_EPS = 1e-6
NEG_INF = -1e30

LANES = 128
SUBLANES = 8
ROW_TILES = D_MODEL // LANES

DN_TB = 256
DN_SB = 128
DN_HEADS_PER_PASS = 8
AT_QB = 256
AT_KB = 3 * AT_QB
GROUP = 1024
FFN_BM = 128
MOE_EPS = 4
MOE_WEIGHT_SLOTS = 4
ROW_LOOP_UNROLL = 4
GROUP_ROWS = GROUP * TOP_K + N_EXPERTS * (SUBLANES - 1) + FFN_BM
GROUP_ROWS = -(-GROUP_ROWS // SUBLANES) * SUBLANES
VMEM_LIMIT = 58 * 1024 * 1024

COL_Q, COL_K, COL_V, COL_Z, COL_GA, COL_GB = 0, 1, 2, 3, 4, 5
COL_AQ, COL_AK, COL_AV = 12, 13, 14
BIG_COLS = 4 * DN_WIDTH + 2 * D_MODEL + 3 * AT_WIDTH


def _sigmoid(x):
    return 1.0 / (1.0 + jnp.exp(-x))


def _silu(x):
    return x * _sigmoid(x)


def _softplus(x):
    return jnp.maximum(x, 0.0) + jnp.log1p(jnp.exp(-jnp.abs(x)))


def _bdot(a, b):
    return jnp.dot(a.astype(BF16), b.astype(BF16), preferred_element_type=F32)


def _bdot_nt(a, b):
    return lax.dot_general(a.astype(BF16), b.astype(BF16), (((1,), (1,)), ((), ())),
                           preferred_element_type=F32)


def _cparams(sem):
    return pltpu.CompilerParams(dimension_semantics=sem, vmem_limit_bytes=VMEM_LIMIT)


def _mm_kernel(x_ref, w_ref, o_ref):
    o_ref[...] = jnp.dot(x_ref[...].astype(BF16), w_ref[...],
                         preferred_element_type=F32).astype(o_ref.dtype)


def _matmul(x, w_all, layer, out_dtype, tm, tn):
    n, k = x.shape
    m = w_all.shape[1]
    return pl.pallas_call(
        _mm_kernel,
        grid=(n // tm, m // tn),
        in_specs=[pl.BlockSpec((tm, k), lambda i, j: (i, 0)),
                  pl.BlockSpec((k, tn), lambda i, j: (layer, j))],
        out_specs=pl.BlockSpec((tm, tn), lambda i, j: (i, j)),
        out_shape=jax.ShapeDtypeStruct((n, m), out_dtype),
        compiler_params=_cparams(("parallel", "parallel")),
        name="inproj",
    )(x, w_all)


def _gates_kernel(x_ref, wc_ref, wr_ref, pc_ref, pr_ref, cg_ref, rg_ref, *, tm):
    x = x_ref[...].astype(BF16)
    r = lax.broadcasted_iota(I32, (tm, tm), 0)
    c = lax.broadcasted_iota(I32, (tm, tm), 1)
    same = (r // CHUNK) == (c // CHUNK)

    pc = jnp.dot(x, wc_ref[...], preferred_element_type=F32)
    lane = lax.broadcasted_iota(I32, (tm, LANES), 1)
    col = jnp.where(lane < DN_HEADS, _sigmoid(pc),
                    -pc_ref[0:1, :] * _softplus(pc + pc_ref[1:2, :]))
    lower = (same & (r >= c)).astype(F32)
    gcc = jnp.dot(lower, col, preferred_element_type=F32, precision=lax.Precision.HIGHEST)
    cg_ref[...] = jnp.where(lane < DN_HEADS, col, gcc)

    pr = lax.dot_general(wr_ref[...], x, (((1,), (1,)), ((), ())),
                         preferred_element_type=F32)
    sub = lax.broadcasted_iota(I32, (2 * DN_HEADS, tm), 0)
    row = jnp.where(sub < DN_HEADS, _sigmoid(pr),
                    -pr_ref[:, 0:1] * _softplus(pr + pr_ref[:, 1:2]))
    upper = (same & (r <= c)).astype(F32)
    gcr = jnp.dot(row, upper, preferred_element_type=F32, precision=lax.Precision.HIGHEST)
    rg_ref[...] = jnp.where(sub < DN_HEADS, row, gcr)


def _gates(xb, w_col, w_row, p_col, p_row, tm=256):
    n = xb.shape[0]
    return pl.pallas_call(
        functools.partial(_gates_kernel, tm=tm),
        grid=(n // tm,),
        in_specs=[pl.BlockSpec((tm, D_MODEL), lambda i: (i, 0)),
                  pl.BlockSpec((D_MODEL, LANES), lambda i: (0, 0)),
                  pl.BlockSpec((2 * DN_HEADS, D_MODEL), lambda i: (0, 0)),
                  pl.BlockSpec((2, LANES), lambda i: (0, 0)),
                  pl.BlockSpec((2 * DN_HEADS, LANES), lambda i: (0, 0))],
        out_specs=[pl.BlockSpec((tm, LANES), lambda i: (i, 0)),
                   pl.BlockSpec((2 * DN_HEADS, tm), lambda i: (0, i))],
        out_shape=[jax.ShapeDtypeStruct((n, LANES), F32),
                   jax.ShapeDtypeStruct((2 * DN_HEADS, n), F32)],
        compiler_params=_cparams(("parallel",)),
        name="dn_gates",
    )(xb, w_col, w_row, p_col, p_row)


def _dn_kernel(q_ref, k_ref, v_ref, z_ref, cg_ref, rg_ref, cw_ref, nw_ref, o_ref,
               xpad, s_ref, u_s, wq_s, qkk_s, vn_s, o_s, mask_s):
    tb, nch, hd = DN_TB, DN_TB // CHUNK, DN_HEAD_DIM
    t = pl.program_id(1)

    sb = DN_SB
    halves = tb // sb

    @pl.when(t == 0)
    def _():
        xpad[:, 0:SUBLANES, :] = jnp.zeros((3, SUBLANES, DN_WIDTH), F32)
        s_ref[...] = jnp.zeros_like(s_ref)
        r = lax.broadcasted_iota(I32, (sb, sb), 0)
        c = lax.broadcasted_iota(I32, (sb, sb), 1)
        same = (r // CHUNK) == (c // CHUNK)
        mask_s[0] = jnp.where(same & (r >= c), 0.0, NEG_INF)
        mask_s[1] = (same & (r > c)).astype(F32)
        mask_s[2] = ((r // 16) == (c // 16)).astype(F32)
        mask_s[3] = (r == c).astype(F32)

    @pl.when(t > 0)
    def _():
        xpad[:, 0:SUBLANES, :] = xpad[:, tb:tb + SUBLANES, :]

    xpad[0, SUBLANES:SUBLANES + tb, :] = q_ref[...].astype(F32)
    xpad[1, SUBLANES:SUBLANES + tb, :] = k_ref[...].astype(F32)
    xpad[2, SUBLANES:SUBLANES + tb, :] = v_ref[...].astype(F32)
    vn_s[...] = jnp.zeros_like(vn_s)

    lane_chunk = lax.broadcasted_iota(I32, (hd, sb), 1) // CHUNK
    ones = jnp.ones((hd, hd), BF16)

    def row_sum_sq(x):
        return jnp.dot((x * x).astype(BF16), ones, preferred_element_type=F32)

    def each(fn, *lists):
        return [fn(*args) for args in zip(*lists)]

    for h0 in range(0, DN_HEADS, DN_HEADS_PER_PASS):
        inst = [(h, half) for h in range(h0, h0 + DN_HEADS_PER_PASS) for half in range(halves)]

        def conv(idx, h, half):
            block = xpad[idx, half * sb:half * sb + sb + SUBLANES, h * hd:(h + 1) * hd]
            acc = None
            for i in range(CONV_WIDTH):
                col0 = idx * DN_WIDTH + h * hd
                back = CONV_WIDTH - 1 - i
                shifted = block if back == 0 else pltpu.roll(block, back, 0)
                term = shifted[SUBLANES:SUBLANES + sb, :] * cw_ref[i:i + 1, col0:col0 + hd]
                acc = term if acc is None else acc + term
            return _silu(acc)

        qh = [conv(0, h, half) for h, half in inst]
        kh = [conv(1, h, half) for h, half in inst]
        vh = [conv(2, h, half) for h, half in inst]
        qh = each(lambda x: x * lax.rsqrt(row_sum_sq(x) + L2_EPS) * (hd ** -0.5), qh)
        kh = each(lambda x: x * lax.rsqrt(row_sum_sq(x) + L2_EPS), kh)
        beta = [cg_ref[half * sb:(half + 1) * sb, h:h + 1] for h, half in inst]
        gc = [cg_ref[half * sb:(half + 1) * sb, DN_HEADS + h:DN_HEADS + h + 1] for h, half in inst]
        gcr = [rg_ref[DN_HEADS + h:DN_HEADS + h + 1, half * sb:(half + 1) * sb] for h, half in inst]
        dec = each(lambda a, b: jnp.exp((a - b) + mask_s[0]), gc, gcr)
        kb = each(lambda a, b: a * b, kh, beta)
        khb = each(lambda a: a.astype(BF16), kh)
        lmat = each(lambda a, b, d: _bdot_nt(a, b) * d * mask_s[1], kb, khb, dec)
        qk = each(lambda a, b, d: _bdot_nt(a, b) * d, qh, khb, dec)
        eye = mask_s[3]
        dmat = each(lambda a: a * mask_s[2], lmat)
        omat = each(lambda a, b: a - b, lmat, dmat)
        s1 = each(lambda a: eye - a, dmat)
        d2 = each(lambda a: _bdot(a, a), dmat)
        s2 = each(lambda a, b: a + _bdot(a, b), s1, d2)
        d4 = each(lambda a: _bdot(a, a), d2)
        s3 = each(lambda a, b: a + _bdot(a, b), s2, d4)
        d8 = each(lambda a: _bdot(a, a), d4)
        dinv = each(lambda a, b: a + _bdot(a, b), s3, d8)
        m1 = each(_bdot, dinv, omat)
        m2 = each(lambda a: _bdot(a, a), m1)
        im = each(lambda a: eye - a, m1)
        pm = each(lambda a, b: a + _bdot(a, b), im, m2)
        tinv = each(_bdot, pm, dinv)
        eg = each(jnp.exp, gc)
        rhs = each(lambda v, b, k, e: jnp.concatenate([v * b, k * e], axis=1), vh, beta, kb, eg)
        uw = each(_bdot, tinv, rhs)
        for i, (h, half) in enumerate(inst):
            rs = slice(half * sb, (half + 1) * sb)
            u_s[h, rs, :] = uw[i][:, :hd]
            w_b = uw[i][:, hd:].astype(BF16)
            qd_b = (qh[i] * eg[i]).astype(BF16)
            qk_b = qk[i].astype(BF16)
            gl = jnp.concatenate(
                [jnp.broadcast_to(gc[i][cc * CHUNK + CHUNK - 1:cc * CHUNK + CHUNK, :], (CHUNK, 1))
                 for cc in range(sb // CHUNK)], axis=0)
            kdt = (kh[i] * jnp.exp(gl - gc[i])).T
            for cc in range(sb // CHUNK):
                chunk = half * (sb // CHUNK) + cc
                cr = slice(cc * CHUNK, (cc + 1) * CHUNK)
                wq_s[h, chunk, 0:CHUNK, :] = w_b[cr, :]
                wq_s[h, chunk, CHUNK:2 * CHUNK, :] = qd_b[cr, :]
                qkk_s[h, chunk, 0:CHUNK, :] = qk_b[cr, :]
                qkk_s[h, chunk, CHUNK:CHUNK + hd, :] = jnp.where(lane_chunk == cc, kdt,
                                                                  0.0).astype(BF16)

    heads = list(range(DN_HEADS))

    def chunk_step(cc, carry):
        r0 = pl.multiple_of(cc * CHUNK, CHUNK)
        rows = pl.ds(r0, CHUNK)
        sub_rows = pl.ds(pl.multiple_of((cc // (sb // CHUNK)) * sb, sb), sb)
        s_old = [s_ref[h] for h in heads]
        s_b = each(lambda a: a.astype(BF16), s_old)
        by_state = [jnp.dot(wq_s[h, cc], s_b[h], preferred_element_type=F32) for h in heads]
        for h in heads:
            vn_s[h, rows, :] = (u_s[h, rows, :] - by_state[h][:CHUNK, :]).astype(BF16)
        vn_all = [vn_s[h, sub_rows, :] for h in heads]
        by_vnew = [jnp.dot(qkk_s[h, cc], vn_all[h], preferred_element_type=F32) for h in heads]
        for h in heads:
            g_last = cg_ref[pl.ds(r0 + CHUNK - 1, 1), DN_HEADS + h:DN_HEADS + h + 1]
            s_ref[h] = s_old[h] * jnp.exp(g_last) + by_vnew[h][CHUNK:, :]
            o_s[rows, h * hd:(h + 1) * hd] = by_state[h][CHUNK:, :] + by_vnew[h][:CHUNK, :]
        return carry

    lax.fori_loop(0, nch, chunk_step, 0)

    for h in range(DN_HEADS):
        sl = slice(h * hd, (h + 1) * hd)
        o = o_s[:, sl]
        o = o * lax.rsqrt(row_sum_sq(o) * (1.0 / hd) + RMS_EPS)
        z = z_ref[:, sl].astype(F32)
        o_ref[:, sl] = (o * nw_ref[...] * _silu(z)).astype(o_ref.dtype)


def _deltanet(proj, cg, rg, conv_w, norm_w, batch, seq):
    n = batch * seq
    nt = seq // DN_TB
    tb, hd, nch = DN_TB, DN_HEAD_DIM, DN_TB // CHUNK

    def col(j):
        return pl.BlockSpec((tb, DN_WIDTH), lambda b, t, j=j: (b * nt + t, j))

    return pl.pallas_call(
        _dn_kernel,
        grid=(batch, nt),
        in_specs=[col(COL_Q), col(COL_K), col(COL_V), col(COL_Z),
                  pl.BlockSpec((tb, LANES), lambda b, t: (b * nt + t, 0)),
                  pl.BlockSpec((2 * DN_HEADS, tb), lambda b, t: (0, b * nt + t)),
                  pl.BlockSpec((CONV_WIDTH, 3 * DN_WIDTH), lambda b, t: (0, 0)),
                  pl.BlockSpec((1, hd), lambda b, t: (0, 0))],
        out_specs=pl.BlockSpec((tb, DN_WIDTH), lambda b, t: (b * nt + t, 0)),
        out_shape=jax.ShapeDtypeStruct((n, DN_WIDTH), BF16),
        scratch_shapes=[
            pltpu.VMEM((3, tb + SUBLANES, DN_WIDTH), F32),
            pltpu.VMEM((DN_HEADS, hd, hd), F32),
            pltpu.VMEM((DN_HEADS, tb, hd), F32),
            pltpu.VMEM((DN_HEADS, nch, 2 * CHUNK, hd), BF16),
            pltpu.VMEM((DN_HEADS, nch, CHUNK + hd, DN_SB), BF16),
            pltpu.VMEM((DN_HEADS, tb, hd), BF16),
            pltpu.VMEM((tb, DN_WIDTH), F32),
            pltpu.VMEM((4, DN_SB, DN_SB), F32),
        ],
        compiler_params=_cparams(("parallel", "arbitrary")),
        name="deltanet",
    )(proj, proj, proj, proj, cg, rg, conv_w, norm_w)


def _attn_kernel(q_ref, k0_ref, k1_ref, k2_ref, v0_ref, v1_ref, v2_ref, bias_ref, o_ref):
    t = pl.program_id(1)
    q = q_ref[...]
    k = jnp.concatenate([k0_ref[...], k1_ref[...], k2_ref[...]], axis=0)
    v = jnp.concatenate([v0_ref[...], v1_ref[...], v2_ref[...]], axis=0)
    kidx = lax.broadcasted_iota(I32, (1, AT_KB), 1)
    key_valid = kidx >= (2 - t) * AT_QB
    lane_head = lax.broadcasted_iota(I32, (1, LANES), 1) // AT_HEAD_DIM
    scale = AT_HEAD_DIM ** -0.5
    def scores(head):
        p, mine = head // 2, lane_head == head % 2
        sl = slice(p * LANES, (p + 1) * LANES)
        qm = jnp.where(mine, q[:, sl] * scale, jnp.zeros((AT_QB, LANES), BF16))
        s = lax.dot_general(qm, k[:, sl], (((1,), (1,)), ((), ())), preferred_element_type=F32)
        return jnp.where(key_valid, s + bias_ref[head], NEG_INF)

    def weighted_values(head, s, out):
        p, mine = head // 2, lane_head == head % 2
        vp = v[:, p * LANES:(p + 1) * LANES]
        e = jnp.exp(s - jnp.max(s, axis=-1, keepdims=True))
        v_aug = jnp.where(mine, vp, jnp.ones_like(vp))
        pv = jnp.dot(e.astype(BF16), v_aug, preferred_element_type=F32)
        row_sum = pltpu.roll(pv, AT_HEAD_DIM, 1)
        return jnp.where(mine, pv / row_sum, out)

    outs = [jnp.zeros((AT_QB, LANES), F32)] * (AT_HEADS // 2)
    s_prev = scores(0)
    for head in range(AT_HEADS):
        s_next = scores(head + 1) if head + 1 < AT_HEADS else None
        outs[head // 2] = weighted_values(head, s_prev, outs[head // 2])
        s_prev = s_next
    for p in range(AT_HEADS // 2):
        o_ref[:, p * LANES:(p + 1) * LANES] = outs[p].astype(o_ref.dtype)


def _attention(proj, bias, batch, seq):
    n = batch * seq
    nt = seq // AT_QB

    def kv(col, back):
        return pl.BlockSpec((AT_QB, AT_WIDTH),
                            lambda b, t, col=col, back=back: (b * nt + jnp.maximum(t - back, 0), col))

    return pl.pallas_call(
        _attn_kernel,
        grid=(batch, nt),
        in_specs=[pl.BlockSpec((AT_QB, AT_WIDTH), lambda b, t: (b * nt + t, COL_AQ)),
                  kv(COL_AK, 2), kv(COL_AK, 1), kv(COL_AK, 0),
                  kv(COL_AV, 2), kv(COL_AV, 1), kv(COL_AV, 0),
                  pl.BlockSpec((AT_HEADS, AT_QB, AT_KB), lambda b, t: (0, 0, 0))],
        out_specs=pl.BlockSpec((AT_QB, AT_WIDTH), lambda b, t: (b * nt + t, 0)),
        out_shape=jax.ShapeDtypeStruct((n, AT_WIDTH), BF16),
        compiler_params=_cparams(("parallel", "parallel")),
        name="band_attention",
    )(proj, proj, proj, proj, proj, proj, proj, bias)


def _bias_kernel(f_ref, o_ref):
    width = f_ref.shape[1]
    qi = lax.broadcasted_iota(I32, (AT_QB, AT_KB), 0) // CHUNK
    ki = lax.broadcasted_iota(I32, (AT_QB, AT_KB), 1) // CHUNK
    band = (ki >= qi) & (ki <= qi + LEFT_CHUNKS)
    for h in range(AT_HEADS):
        base = jnp.broadcast_to(f_ref[h:h + 1, :], (AT_QB, width))
        rolled = pltpu.roll(base, width - (AT_QB - 1), 1, stride=1, stride_axis=0)
        o_ref[h] = jnp.where(band, rolled[:, :AT_KB], NEG_INF)


def _attention_bias(rel_table):
    table = rel_table.astype(F32)
    n_clipped = (AT_KB - 1) - REL_CLIP
    f = jnp.concatenate([jnp.broadcast_to(table[:, :1], (AT_HEADS, n_clipped)),
                         table[:, :AT_QB + REL_CLIP + 1]], axis=1)
    return pl.pallas_call(
        _bias_kernel,
        out_shape=jax.ShapeDtypeStruct((AT_HEADS, AT_QB, AT_KB), F32),
        name="attn_bias",
    )(f)


def _layer_norm(y, g, b):
    mu = jnp.mean(y, axis=-1, keepdims=True)
    d = y - mu
    var = jnp.mean(d * d, axis=-1, keepdims=True)
    return d * lax.rsqrt(var + LN_EPS) * g + b


def _load_row_tiles(ref):
    m = ref.shape[0] // ROW_TILES
    return jnp.concatenate([ref[pl.ds(j, m, stride=ROW_TILES), :] for j in range(ROW_TILES)],
                           axis=1)


def _store_row_tiles(ref, x):
    m = x.shape[0]
    for j in range(ROW_TILES):
        ref[pl.ds(j, m, stride=ROW_TILES), :] = x[:, j * LANES:(j + 1) * LANES]


def _merge_kernel(dn_ref, at_ref, ga_ref, gb_ref, x_ref, wa_ref, wb_ref, wo_ref, g_ref, b_ref,
                  x1t_ref):
    ya = jnp.dot(dn_ref[...], wa_ref[...], preferred_element_type=F32)
    yb = jnp.dot(at_ref[...], wb_ref[...], preferred_element_type=F32)
    mix = _sigmoid(ga_ref[...].astype(F32)) * ya + _sigmoid(gb_ref[...].astype(F32)) * yb
    y = ALPHA * x_ref[...] + jnp.dot(mix.astype(BF16), wo_ref[...], preferred_element_type=F32)
    _store_row_tiles(x1t_ref, _layer_norm(y, g_ref[...], b_ref[...]))


def _merge(dn, at, proj, x, wa, wb, wo, g, b, tm=512):
    n = x.shape[0]
    return pl.pallas_call(
        _merge_kernel,
        grid=(n // tm,),
        in_specs=[pl.BlockSpec((tm, DN_WIDTH), lambda i: (i, 0)),
                  pl.BlockSpec((tm, AT_WIDTH), lambda i: (i, 0)),
                  pl.BlockSpec((tm, D_MODEL), lambda i: (i, COL_GA)),
                  pl.BlockSpec((tm, D_MODEL), lambda i: (i, COL_GB)),
                  pl.BlockSpec((tm, D_MODEL), lambda i: (i, 0)),
                  pl.BlockSpec((DN_WIDTH, D_MODEL), lambda i: (0, 0)),
                  pl.BlockSpec((AT_WIDTH, D_MODEL), lambda i: (0, 0)),
                  pl.BlockSpec((D_MODEL, D_MODEL), lambda i: (0, 0)),
                  pl.BlockSpec((1, D_MODEL), lambda i: (0, 0)),
                  pl.BlockSpec((1, D_MODEL), lambda i: (0, 0))],
        out_specs=pl.BlockSpec((tm * ROW_TILES, LANES), lambda i: (i, 0)),
        out_shape=jax.ShapeDtypeStruct((n * ROW_TILES, LANES), F32),
        compiler_params=_cparams(("parallel",)),
        name="merge_ln1",
    )(dn, at, proj, proj, x, wa, wb, wo, g, b)


def _router_kernel(xt_ref, rw_ref, rb_ref, dest_ref, wt_ref, seg_ref):
    gt, ne = GROUP, N_EXPERTS
    logits = lax.dot_general(rw_ref[...], _load_row_tiles(xt_ref), (((1,), (1,)), ((), ())),
                             preferred_element_type=F32, precision=lax.Precision.HIGHEST)
    scores = _sigmoid(logits)
    sel = scores + rb_ref[:, 0:1]
    eidx = lax.broadcasted_iota(I32, (ne, gt), 0)
    picked = jnp.zeros((ne, gt), jnp.bool_)
    hits, vals = [], []
    for _ in range(TOP_K):
        m = jnp.max(sel, axis=0, keepdims=True)
        idx = jnp.min(jnp.where(sel == m, eidx, ne), axis=0, keepdims=True)
        hit = eidx == idx
        hits.append(hit)
        vals.append(jnp.sum(jnp.where(hit, scores, 0.0), axis=0, keepdims=True))
        sel = jnp.where(hit, -jnp.inf, sel)
        picked = picked | hit
    total = vals[0]
    for v in vals[1:]:
        total = total + v

    onehot = picked.astype(BF16)
    tr = lax.broadcasted_iota(I32, (gt, gt), 0)
    tc = lax.broadcasted_iota(I32, (gt, gt), 1)
    rank = jnp.dot(onehot, (tr < tc).astype(BF16), preferred_element_type=F32)
    count = jnp.sum(picked.astype(F32), axis=1, keepdims=True)
    tiles = jnp.floor((count + (SUBLANES - 1)) * (1.0 / SUBLANES))
    er = lax.broadcasted_iota(I32, (ne, ne), 0)
    ec = lax.broadcasted_iota(I32, (ne, ne), 1)
    start = SUBLANES * jnp.dot((ec < er).astype(BF16),
                               jnp.broadcast_to(tiles, (ne, LANES)).astype(BF16),
                               preferred_element_type=F32)
    slot = start[:, 0:1] + rank

    sub = lax.broadcasted_iota(I32, (SUBLANES, gt), 0)
    dest = jnp.zeros((SUBLANES, gt), F32)
    wts = jnp.zeros((SUBLANES, gt), F32)
    for kk in range(TOP_K):
        d = jnp.sum(jnp.where(hits[kk], slot, 0.0), axis=0, keepdims=True)
        dest = jnp.where(sub == kk, d, dest)
        wts = jnp.where(sub == kk, vals[kk] / total * ROUTED_SCALE, wts)
    dest_ref[0] = dest.astype(I32) * ROW_TILES
    wt_ref[0] = wts
    lane = lax.broadcasted_iota(I32, (ne, LANES), 1)
    seg_ref[0] = jnp.where(lane == 0, start, jnp.broadcast_to(count, (ne, LANES))).astype(I32)


def _router(x1t, rw_t, rb):
    n = x1t.shape[0] // ROW_TILES
    ng = n // GROUP
    return pl.pallas_call(
        _router_kernel,
        grid=(ng,),
        in_specs=[pl.BlockSpec((GROUP * ROW_TILES, LANES), lambda g: (g, 0)),
                  pl.BlockSpec((N_EXPERTS, D_MODEL), lambda g: (0, 0)),
                  pl.BlockSpec((N_EXPERTS, LANES), lambda g: (0, 0))],
        out_specs=[pl.BlockSpec((1, SUBLANES, GROUP), lambda g: (g, 0, 0)),
                   pl.BlockSpec((1, SUBLANES, GROUP), lambda g: (g, 0, 0)),
                   pl.BlockSpec((1, N_EXPERTS, LANES), lambda g: (g, 0, 0))],
        out_shape=[jax.ShapeDtypeStruct((ng, SUBLANES, GROUP), I32),
                   jax.ShapeDtypeStruct((ng, SUBLANES, GROUP), F32),
                   jax.ShapeDtypeStruct((ng, N_EXPERTS, LANES), I32)],
        compiler_params=_cparams(("parallel",)),
        name="router",
    )(x1t, rw_t, rb)


def _moe_kernel(start_ref, count_ref, xt_hbm, dest_hbm, wt_hbm, wgu_hbm, wd_hbm, out_ref,
                buf, dest_s, wt_s, sem, wgu_buf, wd_buf, wsem, *, first_expert, n_groups):
    g = pl.program_id(0)
    e = pl.program_id(1)
    ne, gt = N_EXPERTS, GROUP
    n_steps = N_EXPERTS // MOE_EPS
    n_blocks_total = n_groups * n_steps
    blk = g * n_steps + e
    slot = blk % MOE_WEIGHT_SLOTS

    def routing_copies():
        rows = pl.ds(pl.multiple_of(g * (gt * ROW_TILES), gt * ROW_TILES), gt * ROW_TILES)
        return (pltpu.make_async_copy(dest_hbm.at[g], dest_s, sem.at[0]),
                pltpu.make_async_copy(wt_hbm.at[g], wt_s, sem.at[1]),
                pltpu.make_async_copy(xt_hbm.at[rows], out_ref, sem.at[2]))

    def weight_copies(b, s):
        first = first_expert + (b % n_steps) * MOE_EPS
        return (pltpu.make_async_copy(wgu_hbm.at[pl.ds(first, MOE_EPS)], wgu_buf.at[s],
                                      wsem.at[s, 0]),
                pltpu.make_async_copy(wd_hbm.at[pl.ds(first, MOE_EPS)], wd_buf.at[s],
                                      wsem.at[s, 1]))

    @pl.when(blk == 0)
    def _():
        for b in range(min(MOE_WEIGHT_SLOTS, n_blocks_total)):
            for cp in weight_copies(b, b):
                cp.start()

    def tile_of(row8):
        return pl.ds(pl.multiple_of(row8, ROW_TILES), ROW_TILES)

    @pl.when(e == 0)
    def _():
        for cp in routing_copies():
            cp.start()
        zeros = jnp.zeros((SUBLANES * ROW_TILES, LANES), F32)

        def zero_pad(ee, carry):
            s = start_ref[g * ne + ee]
            cnt = count_ref[g * ne + ee]
            z0 = pl.multiple_of((s + (cnt // SUBLANES) * SUBLANES) * ROW_TILES, SUBLANES * ROW_TILES)
            buf[pl.ds(z0, SUBLANES * ROW_TILES), :] = zeros
            return carry

        lax.fori_loop(0, ne, zero_pad, 0)
        last = g * ne + ne - 1
        end = start_ref[last] + (count_ref[last] + SUBLANES - 1) // SUBLANES * SUBLANES
        end = pl.multiple_of(end * ROW_TILES, SUBLANES * ROW_TILES)
        buf[pl.ds(end, FFN_BM * ROW_TILES), :] = jnp.zeros((FFN_BM * ROW_TILES, LANES), F32)
        for cp in routing_copies():
            cp.wait()

        def scatter(t0, carry):
            for u in range(ROW_LOOP_UNROLL):
                tok = t0 * ROW_LOOP_UNROLL + u
                row = out_ref[tile_of(tok * ROW_TILES), :]
                for kk in range(TOP_K):
                    buf[tile_of(dest_s[kk * gt + tok]), :] = row
            return carry

        lax.fori_loop(0, gt // ROW_LOOP_UNROLL, scatter, 0)

    experts = list(range(MOE_EPS))
    seg = [(start_ref[g * ne + e * MOE_EPS + k], count_ref[g * ne + e * MOE_EPS + k])
           for k in experts]
    n_blocks = [(cnt + FFN_BM - 1) // FFN_BM for _, cnt in seg]
    last_block_row = GROUP_ROWS - FFN_BM

    def ffn_block(i, carry):
        r0 = [pl.multiple_of(jnp.minimum(s + i * FFN_BM, last_block_row), SUBLANES) for s, _ in seg]
        base = [pl.multiple_of(r * ROW_TILES, SUBLANES * ROW_TILES) for r in r0]
        xin = [jnp.concatenate([buf[pl.ds(base[k] + j, FFN_BM, stride=ROW_TILES), :]
                                for j in range(ROW_TILES)], axis=1) for k in experts]
        gu = [jnp.dot(xin[k].astype(BF16), wgu_buf[slot, k], preferred_element_type=F32)
              for k in experts]
        hidden = [(_silu(gu[k][:, :EXPERT_DIM]) * gu[k][:, EXPERT_DIM:]).astype(BF16)
                  for k in experts]
        y = [jnp.dot(hidden[k], wd_buf[slot, k], preferred_element_type=F32) for k in experts]
        for k in experts:
            s, cnt = seg[k]
            keep = (r0[k] + lax.broadcasted_iota(I32, (FFN_BM, 1), 0)) < s + cnt
            for j in range(ROW_TILES):
                buf[pl.ds(base[k] + j, FFN_BM, stride=ROW_TILES), :] = jnp.where(
                    keep, y[k][:, j * LANES:(j + 1) * LANES], xin[k][:, j * LANES:(j + 1) * LANES])
        return carry

    total_blocks = n_blocks[0]
    for nb in n_blocks[1:]:
        total_blocks = jnp.maximum(total_blocks, nb)
    for cp in weight_copies(blk, slot):
        cp.wait()
    lax.fori_loop(0, total_blocks, ffn_block, 0)

    @pl.when(blk + MOE_WEIGHT_SLOTS < n_blocks_total)
    def _():
        for cp in weight_copies(blk + MOE_WEIGHT_SLOTS, slot):
            cp.start()

    @pl.when(e == pl.num_programs(1) - 1)
    def _():
        def combine(t0, carry):
            for u in range(ROW_LOOP_UNROLL):
                tok = t0 * ROW_LOOP_UNROLL + u
                acc = buf[tile_of(dest_s[tok]), :] * wt_s[tok]
                for kk in range(1, TOP_K):
                    acc = acc + buf[tile_of(dest_s[kk * gt + tok]), :] * wt_s[kk * gt + tok]
                out_ref[tile_of(tok * ROW_TILES), :] = acc
            return carry

        lax.fori_loop(0, gt // ROW_LOOP_UNROLL, combine, 0)


def _moe(x1t, dest, wts, starts, counts, wgu_all, wd_all, layer):
    n = x1t.shape[0] // ROW_TILES
    ng = n // GROUP
    grid_spec = pltpu.PrefetchScalarGridSpec(
        num_scalar_prefetch=2,
        grid=(ng, N_EXPERTS // MOE_EPS),
        in_specs=[pl.BlockSpec(memory_space=pl.ANY)] * 5,
        out_specs=pl.BlockSpec((GROUP * ROW_TILES, LANES), lambda g, e, *_: (g, 0),
                               pipeline_mode=pl.Buffered(1)),
        scratch_shapes=[pltpu.VMEM((GROUP_ROWS * ROW_TILES, LANES), F32),
                        pltpu.SMEM((SUBLANES * GROUP,), I32),
                        pltpu.SMEM((SUBLANES * GROUP,), F32),
                        pltpu.SemaphoreType.DMA((3,)),
                        pltpu.VMEM((MOE_WEIGHT_SLOTS, MOE_EPS, D_MODEL, 2 * EXPERT_DIM), BF16),
                        pltpu.VMEM((MOE_WEIGHT_SLOTS, MOE_EPS, EXPERT_DIM, D_MODEL), BF16),
                        pltpu.SemaphoreType.DMA((MOE_WEIGHT_SLOTS, 2))],
    )
    return pl.pallas_call(
        functools.partial(_moe_kernel, first_expert=layer * N_EXPERTS, n_groups=ng),
        grid_spec=grid_spec,
        out_shape=jax.ShapeDtypeStruct((n * ROW_TILES, LANES), F32),
        compiler_params=_cparams(("arbitrary", "arbitrary")),
        name="routed_experts",
    )(starts, counts, x1t, dest, wts, wgu_all, wd_all)


def _final_kernel(x1t_ref, rt_ref, wsg_ref, wsd_ref, g_ref, b_ref, o_ref):
    x1 = _load_row_tiles(x1t_ref)
    gus = jnp.dot(x1.astype(BF16), wsg_ref[...], preferred_element_type=F32)
    hidden = _silu(gus[:, :SHARED_DIM]) * gus[:, SHARED_DIM:]
    shared = jnp.dot(hidden.astype(BF16), wsd_ref[...], preferred_element_type=F32)
    y = ALPHA * x1 + (_load_row_tiles(rt_ref) + shared)
    o_ref[...] = _layer_norm(y, g_ref[...], b_ref[...])


def _final(x1t, routed_t, wsg, wsd, g, b, tm=512):
    n = x1t.shape[0] // ROW_TILES
    return pl.pallas_call(
        _final_kernel,
        grid=(n // tm,),
        in_specs=[pl.BlockSpec((tm * ROW_TILES, LANES), lambda i: (i, 0)),
                  pl.BlockSpec((tm * ROW_TILES, LANES), lambda i: (i, 0)),
                  pl.BlockSpec((D_MODEL, 2 * SHARED_DIM), lambda i: (0, 0)),
                  pl.BlockSpec((SHARED_DIM, D_MODEL), lambda i: (0, 0)),
                  pl.BlockSpec((1, D_MODEL), lambda i: (0, 0)),
                  pl.BlockSpec((1, D_MODEL), lambda i: (0, 0))],
        out_specs=pl.BlockSpec((tm, D_MODEL), lambda i: (i, 0)),
        out_shape=jax.ShapeDtypeStruct((n, D_MODEL), F32),
        compiler_params=_cparams(("parallel",)),
        name="shared_ln2",
    )(x1t, routed_t, wsg, wsd, g, b)


C_SMALL = 4 * DN_WIDTH
C_ATT = C_SMALL + 2 * DN_HEADS
C_GATE = C_ATT + 3 * AT_WIDTH


def _layer(l, x, batch, seq, w_big_all, wgu_all, wd_all, w_in, conv_w, a_log, dt_bias,
           dn_norm_w, rel_table, w_branch_a, w_branch_b, w_out, ln1_g, ln1_b, router_w,
           router_bias, ws_gate_up, ws_down, ln2_g, ln2_b):
    w_small = w_in[:, C_SMALL:C_ATT]
    w_col = jnp.pad(w_small, ((0, 0), (0, LANES - 2 * DN_HEADS))).astype(BF16)
    w_row = w_small.T.astype(BF16)
    decay_rate = jnp.exp(a_log.astype(F32))
    zeros8 = jnp.zeros((DN_HEADS,), F32)
    p_col = jnp.pad(jnp.stack([jnp.concatenate([zeros8, decay_rate]),
                               jnp.concatenate([zeros8, dt_bias.astype(F32)])]),
                    ((0, 0), (0, LANES - 2 * DN_HEADS)))
    p_row = jnp.pad(p_col[:, :2 * DN_HEADS].T, ((0, 0), (0, LANES - 2)))

    proj = _matmul(x, w_big_all, l, BF16, tm=1024, tn=1280)
    cg, rg = _gates(x, w_col, w_row, p_col, p_row)
    dn = _deltanet(proj, cg, rg, conv_w.astype(F32), dn_norm_w.astype(F32).reshape(1, DN_HEAD_DIM),
                   batch, seq)
    at = _attention(proj, _attention_bias(rel_table), batch, seq)
    x1t = _merge(dn, at, proj, x, w_branch_a.astype(BF16), w_branch_b.astype(BF16),
                 w_out.astype(BF16), ln1_g.reshape(1, -1), ln1_b.reshape(1, -1))
    dest, wts, seg = _router(x1t, router_w.T.astype(F32),
                             jnp.broadcast_to(router_bias.astype(F32)[:, None], (N_EXPERTS, LANES)))
    ng = dest.shape[0]
    routed_t = _moe(x1t, dest.reshape(ng, SUBLANES * GROUP), wts.reshape(ng, SUBLANES * GROUP),
                    seg[:, :, 0].reshape(-1), seg[:, :, 1].reshape(-1), wgu_all, wd_all, l)
    return _final(x1t, routed_t, ws_gate_up.astype(BF16), ws_down.astype(BF16),
                  ln2_g.reshape(1, -1), ln2_b.reshape(1, -1))


def kernel(x, w_in, conv_w, a_log, dt_bias, dn_norm_w, rel_table, w_branch_a, w_branch_b, w_out,
           ln1_g, ln1_b, router_w, router_bias, expert_w_gate_up, expert_w_down,
           shared_w_gate_up, shared_w_down, ln2_g, ln2_b):
    batch, seq, d = x.shape
    assert d == D_MODEL and seq % DN_TB == 0 and seq % AT_QB == 0 and (batch * seq) % GROUP == 0
    depth = w_in.shape[0]
    xf = x.reshape(batch * seq, d).astype(F32)
    w_big_all = jnp.concatenate([w_in[:, :, :C_SMALL], w_in[:, :, C_GATE:], w_in[:, :, C_ATT:C_GATE]],
                                axis=2).astype(BF16).reshape(depth * D_MODEL, BIG_COLS)
    wgu_all = expert_w_gate_up.astype(BF16).reshape(depth * N_EXPERTS, D_MODEL, 2 * EXPERT_DIM)
    wd_all = expert_w_down.astype(BF16).reshape(depth * N_EXPERTS, EXPERT_DIM, D_MODEL)
    for l in range(depth):
        xf = _layer(l, xf, batch, seq, w_big_all, wgu_all, wd_all, w_in[l], conv_w[l],
                        a_log[l], dt_bias[l], dn_norm_w[l], rel_table[l], w_branch_a[l],
                        w_branch_b[l], w_out[l], ln1_g[l], ln1_b[l], router_w[l], router_bias[l],
                        shared_w_gate_up[l], shared_w_down[l], ln2_g[l], ln2_b[l])
    return xf.reshape(batch, seq, d).astype(x.dtype)
```

```python
import functools

import jax
import jax.numpy as jnp
from jax import lax
from jax.experimental import pallas as pl
from jax.experimental.pallas import tpu as pltpu

F32 = jnp.float32
BF16 = jnp.bfloat16
I32 = jnp.int32

D_MODEL = 1024
DEPTH = 2
CHUNK = 64
DN_HEADS = 8
DN_HEAD_DIM = 128
DN_WIDTH = DN_HEADS * DN_HEAD_DIM
CONV_WIDTH = 4
AT_HEADS = 8
AT_HEAD_DIM = 64
AT_WIDTH = AT_HEADS * AT_HEAD_DIM
LEFT_CHUNKS = 8
REL_CLIP = 256
N_EXPERTS = 64
TOP_K = 6
EXPERT_DIM = 256
SHARED_DIM = 256
ROUTED_SCALE = 2.5
ALPHA = (2 * DEPTH) ** 0.25
LN_EPS = 1e-5
RMS_EPS = 1e-6
L2_EPS = 1e-6
NEG_INF = -1e30

LANES = 128
SUBLANES = 8
ROW_TILES = D_MODEL // LANES

DN_TB = 256
DN_SB = 128
DN_HEADS_PER_PASS = 8
AT_QB = 256
AT_KB = 3 * AT_QB
GROUP = 1024
FFN_BM = 128
MOE_EPS = 4
MOE_WEIGHT_SLOTS = 4
ROW_LOOP_UNROLL = 8
GROUP_ROWS = GROUP * TOP_K + N_EXPERTS * (SUBLANES - 1) + FFN_BM
GROUP_ROWS = -(-GROUP_ROWS // SUBLANES) * SUBLANES
VMEM_LIMIT = 58 * 1024 * 1024

COL_Q, COL_K, COL_V, COL_Z, COL_GA, COL_GB = 0, 1, 2, 3, 4, 5
COL_AQ, COL_AK, COL_AV = 12, 13, 14
BIG_COLS = 4 * DN_WIDTH + 2 * D_MODEL + 3 * AT_WIDTH


def _sigmoid(x):
    return 1.0 / (1.0 + jnp.exp(-x))


def _silu(x):
    return x * _sigmoid(x)


def _softplus(x):
    return jnp.maximum(x, 0.0) + jnp.log1p(jnp.exp(-jnp.abs(x)))


def _bdot(a, b):
    return jnp.dot(a.astype(BF16), b.astype(BF16), preferred_element_type=F32)


def _bdot_nt(a, b):
    return lax.dot_general(a.astype(BF16), b.astype(BF16), (((1,), (1,)), ((), ())),
                           preferred_element_type=F32)


def _cparams(sem):
    return pltpu.CompilerParams(dimension_semantics=sem, vmem_limit_bytes=VMEM_LIMIT)


def _mm_kernel(x_ref, w_ref, o_ref):
    o_ref[...] = jnp.dot(x_ref[...].astype(BF16), w_ref[...],
                         preferred_element_type=F32).astype(o_ref.dtype)


def _matmul(x, w_all, layer, out_dtype, tm, tn):
    n, k = x.shape
    m = w_all.shape[1]
    return pl.pallas_call(
        _mm_kernel,
        grid=(n // tm, m // tn),
        in_specs=[pl.BlockSpec((tm, k), lambda i, j: (i, 0)),
                  pl.BlockSpec((k, tn), lambda i, j: (layer, j))],
        out_specs=pl.BlockSpec((tm, tn), lambda i, j: (i, j)),
        out_shape=jax.ShapeDtypeStruct((n, m), out_dtype),
        compiler_params=_cparams(("parallel", "parallel")),
        name="inproj",
    )(x, w_all)


def _gates_kernel(x_ref, wc_ref, wr_ref, pc_ref, pr_ref, cg_ref, rg_ref, *, tm):
    x = x_ref[...].astype(BF16)
    r = lax.broadcasted_iota(I32, (tm, tm), 0)
    c = lax.broadcasted_iota(I32, (tm, tm), 1)
    same = (r // CHUNK) == (c // CHUNK)

    pc = jnp.dot(x, wc_ref[...], preferred_element_type=F32)
    lane = lax.broadcasted_iota(I32, (tm, LANES), 1)
    col = jnp.where(lane < DN_HEADS, _sigmoid(pc),
                    -pc_ref[0:1, :] * _softplus(pc + pc_ref[1:2, :]))
    lower = (same & (r >= c)).astype(F32)
    gcc = jnp.dot(lower, col, preferred_element_type=F32, precision=lax.Precision.HIGHEST)
    cg_ref[...] = jnp.where(lane < DN_HEADS, col, gcc)

    pr = lax.dot_general(wr_ref[...], x, (((1,), (1,)), ((), ())),
                         preferred_element_type=F32)
    sub = lax.broadcasted_iota(I32, (2 * DN_HEADS, tm), 0)
    row = jnp.where(sub < DN_HEADS, _sigmoid(pr),
                    -pr_ref[:, 0:1] * _softplus(pr + pr_ref[:, 1:2]))
    upper = (same & (r <= c)).astype(F32)
    gcr = jnp.dot(row, upper, preferred_element_type=F32, precision=lax.Precision.HIGHEST)
    rg_ref[...] = jnp.where(sub < DN_HEADS, row, gcr)


def _gates(xb, w_col, w_row, p_col, p_row, tm=256):
    n = xb.shape[0]
    return pl.pallas_call(
        functools.partial(_gates_kernel, tm=tm),
        grid=(n // tm,),
        in_specs=[pl.BlockSpec((tm, D_MODEL), lambda i: (i, 0)),
                  pl.BlockSpec((D_MODEL, LANES), lambda i: (0, 0)),
                  pl.BlockSpec((2 * DN_HEADS, D_MODEL), lambda i: (0, 0)),
                  pl.BlockSpec((2, LANES), lambda i: (0, 0)),
                  pl.BlockSpec((2 * DN_HEADS, LANES), lambda i: (0, 0))],
        out_specs=[pl.BlockSpec((tm, LANES), lambda i: (i, 0)),
                   pl.BlockSpec((2 * DN_HEADS, tm), lambda i: (0, i))],
        out_shape=[jax.ShapeDtypeStruct((n, LANES), F32),
                   jax.ShapeDtypeStruct((2 * DN_HEADS, n), F32)],
        compiler_params=_cparams(("parallel",)),
        name="dn_gates",
    )(xb, w_col, w_row, p_col, p_row)


def _dn_kernel(q_ref, k_ref, v_ref, z_ref, cg_ref, rg_ref, cw_ref, nw_ref, o_ref,
               xpad, s_ref, u_s, wq_s, qkk_s, vn_s, o_s, mask_s):
    tb, nch, hd = DN_TB, DN_TB // CHUNK, DN_HEAD_DIM
    t = pl.program_id(1)

    sb = DN_SB
    halves = tb // sb

    @pl.when(t == 0)
    def _():
        xpad[:, 0:SUBLANES, :] = jnp.zeros((3, SUBLANES, DN_WIDTH), F32)
        s_ref[...] = jnp.zeros_like(s_ref)
        r = lax.broadcasted_iota(I32, (sb, sb), 0)
        c = lax.broadcasted_iota(I32, (sb, sb), 1)
        same = (r // CHUNK) == (c // CHUNK)
        mask_s[0] = jnp.where(same & (r >= c), 0.0, NEG_INF)
        mask_s[1] = (same & (r > c)).astype(F32)
        mask_s[2] = ((r // 16) == (c // 16)).astype(F32)
        mask_s[3] = (r == c).astype(F32)

    @pl.when(t > 0)
    def _():
        xpad[:, 0:SUBLANES, :] = xpad[:, tb:tb + SUBLANES, :]

    xpad[0, SUBLANES:SUBLANES + tb, :] = q_ref[...].astype(F32)
    xpad[1, SUBLANES:SUBLANES + tb, :] = k_ref[...].astype(F32)
    xpad[2, SUBLANES:SUBLANES + tb, :] = v_ref[...].astype(F32)
    vn_s[...] = jnp.zeros_like(vn_s)

    lane_chunk = lax.broadcasted_iota(I32, (hd, sb), 1) // CHUNK
    ones = jnp.ones((hd, hd), BF16)

    def row_sum_sq(x):
        return jnp.dot((x * x).astype(BF16), ones, preferred_element_type=F32)

    def each(fn, *lists):
        return [fn(*args) for args in zip(*lists)]

    for h0 in range(0, DN_HEADS, DN_HEADS_PER_PASS):
        inst = [(h, half) for h in range(h0, h0 + DN_HEADS_PER_PASS) for half in range(halves)]

        def conv(idx, h, half):
            block = xpad[idx, half * sb:half * sb + sb + SUBLANES, h * hd:(h + 1) * hd]
            acc = None
            for i in range(CONV_WIDTH):
                col0 = idx * DN_WIDTH + h * hd
                back = CONV_WIDTH - 1 - i
                shifted = block if back == 0 else pltpu.roll(block, back, 0)
                term = shifted[SUBLANES:SUBLANES + sb, :] * cw_ref[i:i + 1, col0:col0 + hd]
                acc = term if acc is None else acc + term
            return _silu(acc)

        qh = [conv(0, h, half) for h, half in inst]
        kh = [conv(1, h, half) for h, half in inst]
        vh = [conv(2, h, half) for h, half in inst]
        qh = each(lambda x: x * lax.rsqrt(row_sum_sq(x) + L2_EPS) * (hd ** -0.5), qh)
        kh = each(lambda x: x * lax.rsqrt(row_sum_sq(x) + L2_EPS), kh)
        beta = [cg_ref[half * sb:(half + 1) * sb, h:h + 1] for h, half in inst]
        gc = [cg_ref[half * sb:(half + 1) * sb, DN_HEADS + h:DN_HEADS + h + 1] for h, half in inst]
        gcr = [rg_ref[DN_HEADS + h:DN_HEADS + h + 1, half * sb:(half + 1) * sb] for h, half in inst]
        dec = each(lambda a, b: jnp.exp((a - b) + mask_s[0]), gc, gcr)
        kb = each(lambda a, b: a * b, kh, beta)
        khb = each(lambda a: a.astype(BF16), kh)
        lmat = each(lambda a, b, d: _bdot_nt(a, b) * d * mask_s[1], kb, khb, dec)
        qk = each(lambda a, b, d: _bdot_nt(a, b) * d, qh, khb, dec)
        eye = mask_s[3]
        dmat = each(lambda a: a * mask_s[2], lmat)
        omat = each(lambda a, b: a - b, lmat, dmat)
        s1 = each(lambda a: eye - a, dmat)
        d2 = each(lambda a: _bdot(a, a), dmat)
        s2 = each(lambda a, b: a + _bdot(a, b), s1, d2)
        d4 = each(lambda a: _bdot(a, a), d2)
        s3 = each(lambda a, b: a + _bdot(a, b), s2, d4)
        d8 = each(lambda a: _bdot(a, a), d4)
        dinv = each(lambda a, b: a + _bdot(a, b), s3, d8)
        m1 = each(_bdot, dinv, omat)
        m2 = each(lambda a: _bdot(a, a), m1)
        im = each(lambda a: eye - a, m1)
        pm = each(lambda a, b: a + _bdot(a, b), im, m2)
        tinv = each(_bdot, pm, dinv)
        eg = each(jnp.exp, gc)
        rhs = each(lambda v, b, k, e: jnp.concatenate([v * b, k * e], axis=1), vh, beta, kb, eg)
        uw = each(_bdot, tinv, rhs)
        for i, (h, half) in enumerate(inst):
            rs = slice(half * sb, (half + 1) * sb)
            u_s[h, rs, :] = uw[i][:, :hd]
            w_b = uw[i][:, hd:].astype(BF16)
            qd_b = (qh[i] * eg[i]).astype(BF16)
            qk_b = qk[i].astype(BF16)
            gl = jnp.concatenate(
                [jnp.broadcast_to(gc[i][cc * CHUNK + CHUNK - 1:cc * CHUNK + CHUNK, :], (CHUNK, 1))
                 for cc in range(sb // CHUNK)], axis=0)
            kdt = (kh[i] * jnp.exp(gl - gc[i])).T
            for cc in range(sb // CHUNK):
                chunk = half * (sb // CHUNK) + cc
                cr = slice(cc * CHUNK, (cc + 1) * CHUNK)
                wq_s[h, chunk, 0:CHUNK, :] = w_b[cr, :]
                wq_s[h, chunk, CHUNK:2 * CHUNK, :] = qd_b[cr, :]
                qkk_s[h, chunk, 0:CHUNK, :] = qk_b[cr, :]
                qkk_s[h, chunk, CHUNK:CHUNK + hd, :] = jnp.where(lane_chunk == cc, kdt,
                                                                  0.0).astype(BF16)

    heads = list(range(DN_HEADS))

    def chunk_step(cc, carry):
        r0 = pl.multiple_of(cc * CHUNK, CHUNK)
        rows = pl.ds(r0, CHUNK)
        sub_rows = pl.ds(pl.multiple_of((cc // (sb // CHUNK)) * sb, sb), sb)
        s_old = [s_ref[h] for h in heads]
        s_b = each(lambda a: a.astype(BF16), s_old)
        by_state = [jnp.dot(wq_s[h, cc], s_b[h], preferred_element_type=F32) for h in heads]
        for h in heads:
            vn_s[h, rows, :] = (u_s[h, rows, :] - by_state[h][:CHUNK, :]).astype(BF16)
        vn_all = [vn_s[h, sub_rows, :] for h in heads]
        by_vnew = [jnp.dot(qkk_s[h, cc], vn_all[h], preferred_element_type=F32) for h in heads]
        for h in heads:
            g_last = cg_ref[pl.ds(r0 + CHUNK - 1, 1), DN_HEADS + h:DN_HEADS + h + 1]
            s_ref[h] = s_old[h] * jnp.exp(g_last) + by_vnew[h][CHUNK:, :]
            o_s[rows, h * hd:(h + 1) * hd] = by_state[h][CHUNK:, :] + by_vnew[h][:CHUNK, :]
        return carry

    lax.fori_loop(0, nch, chunk_step, 0)

    for h in range(DN_HEADS):
        sl = slice(h * hd, (h + 1) * hd)
        o = o_s[:, sl]
        o = o * lax.rsqrt(row_sum_sq(o) * (1.0 / hd) + RMS_EPS)
        z = z_ref[:, sl].astype(F32)
        o_ref[:, sl] = (o * nw_ref[...] * _silu(z)).astype(o_ref.dtype)


def _deltanet(proj, cg, rg, conv_w, norm_w, batch, seq):
    n = batch * seq
    nt = seq // DN_TB
    tb, hd, nch = DN_TB, DN_HEAD_DIM, DN_TB // CHUNK

    def col(j):
        return pl.BlockSpec((tb, DN_WIDTH), lambda b, t, j=j: (b * nt + t, j))

    return pl.pallas_call(
        _dn_kernel,
        grid=(batch, nt),
        in_specs=[col(COL_Q), col(COL_K), col(COL_V), col(COL_Z),
                  pl.BlockSpec((tb, LANES), lambda b, t: (b * nt + t, 0)),
                  pl.BlockSpec((2 * DN_HEADS, tb), lambda b, t: (0, b * nt + t)),
                  pl.BlockSpec((CONV_WIDTH, 3 * DN_WIDTH), lambda b, t: (0, 0)),
                  pl.BlockSpec((1, hd), lambda b, t: (0, 0))],
        out_specs=pl.BlockSpec((tb, DN_WIDTH), lambda b, t: (b * nt + t, 0)),
        out_shape=jax.ShapeDtypeStruct((n, DN_WIDTH), BF16),
        scratch_shapes=[
            pltpu.VMEM((3, tb + SUBLANES, DN_WIDTH), F32),
            pltpu.VMEM((DN_HEADS, hd, hd), F32),
            pltpu.VMEM((DN_HEADS, tb, hd), F32),
            pltpu.VMEM((DN_HEADS, nch, 2 * CHUNK, hd), BF16),
            pltpu.VMEM((DN_HEADS, nch, CHUNK + hd, DN_SB), BF16),
            pltpu.VMEM((DN_HEADS, tb, hd), BF16),
            pltpu.VMEM((tb, DN_WIDTH), F32),
            pltpu.VMEM((4, DN_SB, DN_SB), F32),
        ],
        compiler_params=_cparams(("parallel", "arbitrary")),
        name="deltanet",
    )(proj, proj, proj, proj, cg, rg, conv_w, norm_w)


def _attn_kernel(q_ref, k0_ref, k1_ref, k2_ref, v0_ref, v1_ref, v2_ref, bias_ref, o_ref):
    t = pl.program_id(1)
    q = q_ref[...]
    k = jnp.concatenate([k0_ref[...], k1_ref[...], k2_ref[...]], axis=0)
    v = jnp.concatenate([v0_ref[...], v1_ref[...], v2_ref[...]], axis=0)
    kidx = lax.broadcasted_iota(I32, (1, AT_KB), 1)
    key_valid = kidx >= (2 - t) * AT_QB
    lane_head = lax.broadcasted_iota(I32, (1, LANES), 1) // AT_HEAD_DIM
    scale = AT_HEAD_DIM ** -0.5
    def scores(head):
        p, mine = head // 2, lane_head == head % 2
        sl = slice(p * LANES, (p + 1) * LANES)
        qm = jnp.where(mine, q[:, sl] * scale, jnp.zeros((AT_QB, LANES), BF16))
        s = lax.dot_general(qm, k[:, sl], (((1,), (1,)), ((), ())), preferred_element_type=F32)
        return jnp.where(key_valid, s + bias_ref[head], NEG_INF)

    def weighted_values(head, s, out):
        p, mine = head // 2, lane_head == head % 2
        vp = v[:, p * LANES:(p + 1) * LANES]
        e = jnp.exp(s - jnp.max(s, axis=-1, keepdims=True))
        v_aug = jnp.where(mine, vp, jnp.ones_like(vp))
        pv = jnp.dot(e.astype(BF16), v_aug, preferred_element_type=F32)
        row_sum = pltpu.roll(pv, AT_HEAD_DIM, 1)
        return jnp.where(mine, pv / row_sum, out)

    outs = [jnp.zeros((AT_QB, LANES), F32)] * (AT_HEADS // 2)
    s_prev = scores(0)
    for head in range(AT_HEADS):
        s_next = scores(head + 1) if head + 1 < AT_HEADS else None
        outs[head // 2] = weighted_values(head, s_prev, outs[head // 2])
        s_prev = s_next
    for p in range(AT_HEADS // 2):
        o_ref[:, p * LANES:(p + 1) * LANES] = outs[p].astype(o_ref.dtype)


def _attention(proj, bias, batch, seq):
    n = batch * seq
    nt = seq // AT_QB

    def kv(col, back):
        return pl.BlockSpec((AT_QB, AT_WIDTH),
                            lambda b, t, col=col, back=back: (b * nt + jnp.maximum(t - back, 0), col))

    return pl.pallas_call(
        _attn_kernel,
        grid=(batch, nt),
        in_specs=[pl.BlockSpec((AT_QB, AT_WIDTH), lambda b, t: (b * nt + t, COL_AQ)),
                  kv(COL_AK, 2), kv(COL_AK, 1), kv(COL_AK, 0),
                  kv(COL_AV, 2), kv(COL_AV, 1), kv(COL_AV, 0),
                  pl.BlockSpec((AT_HEADS, AT_QB, AT_KB), lambda b, t: (0, 0, 0))],
        out_specs=pl.BlockSpec((AT_QB, AT_WIDTH), lambda b, t: (b * nt + t, 0)),
        out_shape=jax.ShapeDtypeStruct((n, AT_WIDTH), BF16),
        compiler_params=_cparams(("parallel", "parallel")),
        name="band_attention",
    )(proj, proj, proj, proj, proj, proj, proj, bias)


def _bias_kernel(f_ref, o_ref):
    width = f_ref.shape[1]
    qi = lax.broadcasted_iota(I32, (AT_QB, AT_KB), 0) // CHUNK
    ki = lax.broadcasted_iota(I32, (AT_QB, AT_KB), 1) // CHUNK
    band = (ki >= qi) & (ki <= qi + LEFT_CHUNKS)
    for h in range(AT_HEADS):
        base = jnp.broadcast_to(f_ref[h:h + 1, :], (AT_QB, width))
        rolled = pltpu.roll(base, width - (AT_QB - 1), 1, stride=1, stride_axis=0)
        o_ref[h] = jnp.where(band, rolled[:, :AT_KB], NEG_INF)


def _attention_bias(rel_table):
    table = rel_table.astype(F32)
    n_clipped = (AT_KB - 1) - REL_CLIP
    f = jnp.concatenate([jnp.broadcast_to(table[:, :1], (AT_HEADS, n_clipped)),
                         table[:, :AT_QB + REL_CLIP + 1]], axis=1)
    return pl.pallas_call(
        _bias_kernel,
        out_shape=jax.ShapeDtypeStruct((AT_HEADS, AT_QB, AT_KB), F32),
        name="attn_bias",
    )(f)


def _layer_norm(y, g, b):
    mu = jnp.mean(y, axis=-1, keepdims=True)
    d = y - mu
    var = jnp.mean(d * d, axis=-1, keepdims=True)
    return d * lax.rsqrt(var + LN_EPS) * g + b


def _load_row_tiles(ref):
    m = ref.shape[0] // ROW_TILES
    return jnp.concatenate([ref[pl.ds(j, m, stride=ROW_TILES), :] for j in range(ROW_TILES)],
                           axis=1)


def _store_row_tiles(ref, x):
    m = x.shape[0]
    for j in range(ROW_TILES):
        ref[pl.ds(j, m, stride=ROW_TILES), :] = x[:, j * LANES:(j + 1) * LANES]


def _merge_kernel(dn_ref, at_ref, ga_ref, gb_ref, x_ref, wa_ref, wb_ref, wo_ref, g_ref, b_ref,
                  x1t_ref):
    ya = jnp.dot(dn_ref[...], wa_ref[...], preferred_element_type=F32)
    yb = jnp.dot(at_ref[...], wb_ref[...], preferred_element_type=F32)
    mix = _sigmoid(ga_ref[...].astype(F32)) * ya + _sigmoid(gb_ref[...].astype(F32)) * yb
    y = ALPHA * x_ref[...] + jnp.dot(mix.astype(BF16), wo_ref[...], preferred_element_type=F32)
    _store_row_tiles(x1t_ref, _layer_norm(y, g_ref[...], b_ref[...]))


def _merge(dn, at, proj, x, wa, wb, wo, g, b, tm=512):
    n = x.shape[0]
    return pl.pallas_call(
        _merge_kernel,
        grid=(n // tm,),
        in_specs=[pl.BlockSpec((tm, DN_WIDTH), lambda i: (i, 0)),
                  pl.BlockSpec((tm, AT_WIDTH), lambda i: (i, 0)),
                  pl.BlockSpec((tm, D_MODEL), lambda i: (i, COL_GA)),
                  pl.BlockSpec((tm, D_MODEL), lambda i: (i, COL_GB)),
                  pl.BlockSpec((tm, D_MODEL), lambda i: (i, 0)),
                  pl.BlockSpec((DN_WIDTH, D_MODEL), lambda i: (0, 0)),
                  pl.BlockSpec((AT_WIDTH, D_MODEL), lambda i: (0, 0)),
                  pl.BlockSpec((D_MODEL, D_MODEL), lambda i: (0, 0)),
                  pl.BlockSpec((1, D_MODEL), lambda i: (0, 0)),
                  pl.BlockSpec((1, D_MODEL), lambda i: (0, 0))],
        out_specs=pl.BlockSpec((tm * ROW_TILES, LANES), lambda i: (i, 0)),
        out_shape=jax.ShapeDtypeStruct((n * ROW_TILES, LANES), F32),
        compiler_params=_cparams(("parallel",)),
        name="merge_ln1",
    )(dn, at, proj, proj, x, wa, wb, wo, g, b)


def _router_kernel(xt_ref, rw_ref, rb_ref, dest_ref, wt_ref, seg_ref):
    gt, ne = GROUP, N_EXPERTS
    logits = lax.dot_general(rw_ref[...], _load_row_tiles(xt_ref), (((1,), (1,)), ((), ())),
                             preferred_element_type=F32, precision=lax.Precision.HIGHEST)
    scores = _sigmoid(logits)
    sel = scores + rb_ref[:, 0:1]
    eidx = lax.broadcasted_iota(I32, (ne, gt), 0)
    picked = jnp.zeros((ne, gt), jnp.bool_)
    hits, vals = [], []
    for _ in range(TOP_K):
        m = jnp.max(sel, axis=0, keepdims=True)
        idx = jnp.min(jnp.where(sel == m, eidx, ne), axis=0, keepdims=True)
        hit = eidx == idx
        hits.append(hit)
        vals.append(jnp.sum(jnp.where(hit, scores, 0.0), axis=0, keepdims=True))
        sel = jnp.where(hit, -jnp.inf, sel)
        picked = picked | hit
    total = vals[0]
    for v in vals[1:]:
        total = total + v

    onehot = picked.astype(BF16)
    tr = lax.broadcasted_iota(I32, (gt, gt), 0)
    tc = lax.broadcasted_iota(I32, (gt, gt), 1)
    rank = jnp.dot(onehot, (tr < tc).astype(BF16), preferred_element_type=F32)
    count = jnp.sum(picked.astype(F32), axis=1, keepdims=True)
    tiles = jnp.floor((count + (SUBLANES - 1)) * (1.0 / SUBLANES))
    er = lax.broadcasted_iota(I32, (ne, ne), 0)
    ec = lax.broadcasted_iota(I32, (ne, ne), 1)
    start = SUBLANES * jnp.dot((ec < er).astype(BF16),
                               jnp.broadcast_to(tiles, (ne, LANES)).astype(BF16),
                               preferred_element_type=F32)
    slot = start[:, 0:1] + rank

    sub = lax.broadcasted_iota(I32, (SUBLANES, gt), 0)
    dest = jnp.zeros((SUBLANES, gt), F32)
    wts = jnp.zeros((SUBLANES, gt), F32)
    for kk in range(TOP_K):
        d = jnp.sum(jnp.where(hits[kk], slot, 0.0), axis=0, keepdims=True)
        dest = jnp.where(sub == kk, d, dest)
        wts = jnp.where(sub == kk, vals[kk] / total * ROUTED_SCALE, wts)
    dest_ref[0] = dest.astype(I32) * ROW_TILES
    wt_ref[0] = wts
    lane = lax.broadcasted_iota(I32, (ne, LANES), 1)
    seg_ref[0] = jnp.where(lane == 0, start, jnp.broadcast_to(count, (ne, LANES))).astype(I32)


def _router(x1t, rw_t, rb):
    n = x1t.shape[0] // ROW_TILES
    ng = n // GROUP
    return pl.pallas_call(
        _router_kernel,
        grid=(ng,),
        in_specs=[pl.BlockSpec((GROUP * ROW_TILES, LANES), lambda g: (g, 0)),
                  pl.BlockSpec((N_EXPERTS, D_MODEL), lambda g: (0, 0)),
                  pl.BlockSpec((N_EXPERTS, LANES), lambda g: (0, 0))],
        out_specs=[pl.BlockSpec((1, SUBLANES, GROUP), lambda g: (g, 0, 0)),
                   pl.BlockSpec((1, SUBLANES, GROUP), lambda g: (g, 0, 0)),
                   pl.BlockSpec((1, N_EXPERTS, LANES), lambda g: (g, 0, 0))],
        out_shape=[jax.ShapeDtypeStruct((ng, SUBLANES, GROUP), I32),
                   jax.ShapeDtypeStruct((ng, SUBLANES, GROUP), F32),
                   jax.ShapeDtypeStruct((ng, N_EXPERTS, LANES), I32)],
        compiler_params=_cparams(("parallel",)),
        name="router",
    )(x1t, rw_t, rb)


def _moe_kernel(start_ref, count_ref, xt_hbm, dest_hbm, wt_hbm, wgu_hbm, wd_hbm, out_ref,
                buf, dest_s, wt_s, sem, wgu_buf, wd_buf, wsem, *, first_expert, n_groups):
    g = pl.program_id(0)
    e = pl.program_id(1)
    ne, gt = N_EXPERTS, GROUP
    n_steps = N_EXPERTS // MOE_EPS
    n_blocks_total = n_groups * n_steps
    blk = g * n_steps + e
    slot = blk % MOE_WEIGHT_SLOTS

    def routing_copies():
        rows = pl.ds(pl.multiple_of(g * (gt * ROW_TILES), gt * ROW_TILES), gt * ROW_TILES)
        return (pltpu.make_async_copy(dest_hbm.at[g], dest_s, sem.at[0]),
                pltpu.make_async_copy(wt_hbm.at[g], wt_s, sem.at[1]),
                pltpu.make_async_copy(xt_hbm.at[rows], out_ref, sem.at[2]))

    def weight_copies(b, s):
        first = first_expert + (b % n_steps) * MOE_EPS
        return (pltpu.make_async_copy(wgu_hbm.at[pl.ds(first, MOE_EPS)], wgu_buf.at[s],
                                      wsem.at[s, 0]),
                pltpu.make_async_copy(wd_hbm.at[pl.ds(first, MOE_EPS)], wd_buf.at[s],
                                      wsem.at[s, 1]))

    @pl.when(blk == 0)
    def _():
        for b in range(min(MOE_WEIGHT_SLOTS, n_blocks_total)):
            for cp in weight_copies(b, b):
                cp.start()

    def tile_of(row8):
        return pl.ds(pl.multiple_of(row8, ROW_TILES), ROW_TILES)

    @pl.when(e == 0)
    def _():
        for cp in routing_copies():
            cp.start()
        zeros = jnp.zeros((SUBLANES * ROW_TILES, LANES), F32)

        def zero_pad(ee, carry):
            s = start_ref[g * ne + ee]
            cnt = count_ref[g * ne + ee]
            z0 = pl.multiple_of((s + (cnt // SUBLANES) * SUBLANES) * ROW_TILES, SUBLANES * ROW_TILES)
            buf[pl.ds(z0, SUBLANES * ROW_TILES), :] = zeros
            return carry

        lax.fori_loop(0, ne, zero_pad, 0)
        last = g * ne + ne - 1
        end = start_ref[last] + (count_ref[last] + SUBLANES - 1) // SUBLANES * SUBLANES
        end = pl.multiple_of(end * ROW_TILES, SUBLANES * ROW_TILES)
        buf[pl.ds(end, FFN_BM * ROW_TILES), :] = jnp.zeros((FFN_BM * ROW_TILES, LANES), F32)
        for cp in routing_copies():
            cp.wait()

        def scatter(t0, carry):
            for u in range(ROW_LOOP_UNROLL):
                tok = t0 * ROW_LOOP_UNROLL + u
                row = out_ref[tile_of(tok * ROW_TILES), :]
                for kk in range(TOP_K):
                    buf[tile_of(dest_s[kk * gt + tok]), :] = row
            return carry

        lax.fori_loop(0, gt // ROW_LOOP_UNROLL, scatter, 0)

    experts = list(range(MOE_EPS))
    seg = [(start_ref[g * ne + e * MOE_EPS + k], count_ref[g * ne + e * MOE_EPS + k])
           for k in experts]
    n_blocks = [(cnt + FFN_BM - 1) // FFN_BM for _, cnt in seg]
    last_block_row = GROUP_ROWS - FFN_BM

    def ffn_block(i, carry):
        r0 = [pl.multiple_of(jnp.minimum(s + i * FFN_BM, last_block_row), SUBLANES) for s, _ in seg]
        base = [pl.multiple_of(r * ROW_TILES, SUBLANES * ROW_TILES) for r in r0]
        xin = [jnp.concatenate([buf[pl.ds(base[k] + j, FFN_BM, stride=ROW_TILES), :]
                                for j in range(ROW_TILES)], axis=1) for k in experts]
        gu = [jnp.dot(xin[k].astype(BF16), wgu_buf[slot, k], preferred_element_type=F32)
              for k in experts]
        hidden = [(_silu(gu[k][:, :EXPERT_DIM]) * gu[k][:, EXPERT_DIM:]).astype(BF16)
                  for k in experts]
        y = [jnp.dot(hidden[k], wd_buf[slot, k], preferred_element_type=F32) for k in experts]
        for k in experts:
            s, cnt = seg[k]
            keep = (r0[k] + lax.broadcasted_iota(I32, (FFN_BM, 1), 0)) < s + cnt
            for j in range(ROW_TILES):
                buf[pl.ds(base[k] + j, FFN_BM, stride=ROW_TILES), :] = jnp.where(
                    keep, y[k][:, j * LANES:(j + 1) * LANES], xin[k][:, j * LANES:(j + 1) * LANES])
        return carry

    total_blocks = n_blocks[0]
    for nb in n_blocks[1:]:
        total_blocks = jnp.maximum(total_blocks, nb)
    for cp in weight_copies(blk, slot):
        cp.wait()
    lax.fori_loop(0, total_blocks, ffn_block, 0)

    @pl.when(blk + MOE_WEIGHT_SLOTS < n_blocks_total)
    def _():
        for cp in weight_copies(blk + MOE_WEIGHT_SLOTS, slot):
            cp.start()

    @pl.when(e == pl.num_programs(1) - 1)
    def _():
        def combine(t0, carry):
            for u in range(ROW_LOOP_UNROLL):
                tok = t0 * ROW_LOOP_UNROLL + u
                acc = buf[tile_of(dest_s[tok]), :] * wt_s[tok]
                for kk in range(1, TOP_K):
                    acc = acc + buf[tile_of(dest_s[kk * gt + tok]), :] * wt_s[kk * gt + tok]
                out_ref[tile_of(tok * ROW_TILES), :] = acc
            return carry

        lax.fori_loop(0, gt // ROW_LOOP_UNROLL, combine, 0)


def _moe(x1t, dest, wts, starts, counts, wgu_all, wd_all, layer):
    n = x1t.shape[0] // ROW_TILES
    ng = n // GROUP
    grid_spec = pltpu.PrefetchScalarGridSpec(
        num_scalar_prefetch=2,
        grid=(ng, N_EXPERTS // MOE_EPS),
        in_specs=[pl.BlockSpec(memory_space=pl.ANY)] * 5,
        out_specs=pl.BlockSpec((GROUP * ROW_TILES, LANES), lambda g, e, *_: (g, 0),
                               pipeline_mode=pl.Buffered(1)),
        scratch_shapes=[pltpu.VMEM((GROUP_ROWS * ROW_TILES, LANES), F32),
                        pltpu.SMEM((SUBLANES * GROUP,), I32),
                        pltpu.SMEM((SUBLANES * GROUP,), F32),
                        pltpu.SemaphoreType.DMA((3,)),
                        pltpu.VMEM((MOE_WEIGHT_SLOTS, MOE_EPS, D_MODEL, 2 * EXPERT_DIM), BF16),
                        pltpu.VMEM((MOE_WEIGHT_SLOTS, MOE_EPS, EXPERT_DIM, D_MODEL), BF16),
                        pltpu.SemaphoreType.DMA((MOE_WEIGHT_SLOTS, 2))],
    )
    return pl.pallas_call(
        functools.partial(_moe_kernel, first_expert=layer * N_EXPERTS, n_groups=ng),
        grid_spec=grid_spec,
        out_shape=jax.ShapeDtypeStruct((n * ROW_TILES, LANES), F32),
        compiler_params=_cparams(("arbitrary", "arbitrary")),
        name="routed_experts",
    )(starts, counts, x1t, dest, wts, wgu_all, wd_all)


def _final_kernel(x1t_ref, rt_ref, wsg_ref, wsd_ref, g_ref, b_ref, o_ref):
    x1 = _load_row_tiles(x1t_ref)
    gus = jnp.dot(x1.astype(BF16), wsg_ref[...], preferred_element_type=F32)
    hidden = _silu(gus[:, :SHARED_DIM]) * gus[:, SHARED_DIM:]
    shared = jnp.dot(hidden.astype(BF16), wsd_ref[...], preferred_element_type=F32)
    y = ALPHA * x1 + (_load_row_tiles(rt_ref) + shared)
    o_ref[...] = _layer_norm(y, g_ref[...], b_ref[...])


def _final(x1t, routed_t, wsg, wsd, g, b, tm=512):
    n = x1t.shape[0] // ROW_TILES
    return pl.pallas_call(
        _final_kernel,
        grid=(n // tm,),
        in_specs=[pl.BlockSpec((tm * ROW_TILES, LANES), lambda i: (i, 0)),
                  pl.BlockSpec((tm * ROW_TILES, LANES), lambda i: (i, 0)),
                  pl.BlockSpec((D_MODEL, 2 * SHARED_DIM), lambda i: (0, 0)),
                  pl.BlockSpec((SHARED_DIM, D_MODEL), lambda i: (0, 0)),
                  pl.BlockSpec((1, D_MODEL), lambda i: (0, 0)),
                  pl.BlockSpec((1, D_MODEL), lambda i: (0, 0))],
        out_specs=pl.BlockSpec((tm, D_MODEL), lambda i: (i, 0)),
        out_shape=jax.ShapeDtypeStruct((n, D_MODEL), F32),
        compiler_params=_cparams(("parallel",)),
        name="shared_ln2",
    )(x1t, routed_t, wsg, wsd, g, b)


C_SMALL = 4 * DN_WIDTH
C_ATT = C_SMALL + 2 * DN_HEADS
C_GATE = C_ATT + 3 * AT_WIDTH


def _layer(l, x, batch, seq, w_big_all, wgu_all, wd_all, w_in, conv_w, a_log, dt_bias,
           dn_norm_w, rel_table, w_branch_a, w_branch_b, w_out, ln1_g, ln1_b, router_w,
           router_bias, ws_gate_up, ws_down, ln2_g, ln2_b):
    w_small = w_in[:, C_SMALL:C_ATT]
    w_col = jnp.pad(w_small, ((0, 0), (0, LANES - 2 * DN_HEADS))).astype(BF16)
    w_row = w_small.T.astype(BF16)
    decay_rate = jnp.exp(a_log.astype(F32))
    zeros8 = jnp.zeros((DN_HEADS,), F32)
    p_col = jnp.pad(jnp.stack([jnp.concatenate([zeros8, decay_rate]),
                               jnp.concatenate([zeros8, dt_bias.astype(F32)])]),
                    ((0, 0), (0, LANES - 2 * DN_HEADS)))
    p_row = jnp.pad(p_col[:, :2 * DN_HEADS].T, ((0, 0), (0, LANES - 2)))

    proj = _matmul(x, w_big_all, l, BF16, tm=1024, tn=1280)
    cg, rg = _gates(x, w_col, w_row, p_col, p_row)
    dn = _deltanet(proj, cg, rg, conv_w.astype(F32), dn_norm_w.astype(F32).reshape(1, DN_HEAD_DIM),
                   batch, seq)
    at = _attention(proj, _attention_bias(rel_table), batch, seq)
    x1t = _merge(dn, at, proj, x, w_branch_a.astype(BF16), w_branch_b.astype(BF16),
                 w_out.astype(BF16), ln1_g.reshape(1, -1), ln1_b.reshape(1, -1))
    dest, wts, seg = _router(x1t, router_w.T.astype(F32),
                             jnp.broadcast_to(router_bias.astype(F32)[:, None], (N_EXPERTS, LANES)))
    ng = dest.shape[0]
    routed_t = _moe(x1t, dest.reshape(ng, SUBLANES * GROUP), wts.reshape(ng, SUBLANES * GROUP),
                    seg[:, :, 0].reshape(-1), seg[:, :, 1].reshape(-1), wgu_all, wd_all, l)
    return _final(x1t, routed_t, ws_gate_up.astype(BF16), ws_down.astype(BF16),
                  ln2_g.reshape(1, -1), ln2_b.reshape(1, -1))


def kernel(x, w_in, conv_w, a_log, dt_bias, dn_norm_w, rel_table, w_branch_a, w_branch_b, w_out,
           ln1_g, ln1_b, router_w, router_bias, expert_w_gate_up, expert_w_down,
           shared_w_gate_up, shared_w_down, ln2_g, ln2_b):
    batch, seq, d = x.shape
    assert d == D_MODEL and seq % DN_TB == 0 and seq % AT_QB == 0 and (batch * seq) % GROUP == 0
    depth = w_in.shape[0]
    xf = x.reshape(batch * seq, d).astype(F32)
    w_big_all = jnp.concatenate([w_in[:, :, :C_SMALL], w_in[:, :, C_GATE:], w_in[:, :, C_ATT:C_GATE]],
                                axis=2).astype(BF16).reshape(depth * D_MODEL, BIG_COLS)
    wgu_all = expert_w_gate_up.astype(BF16).reshape(depth * N_EXPERTS, D_MODEL, 2 * EXPERT_DIM)
    wd_all = expert_w_down.astype(BF16).reshape(depth * N_EXPERTS, EXPERT_DIM, D_MODEL)
    for l in range(depth):
        xf = _layer(l, xf, batch, seq, w_big_all, wgu_all, wd_all, w_in[l], conv_w[l],
                        a_log[l], dt_bias[l], dn_norm_w[l], rel_table[l], w_branch_a[l],
                        w_branch_b[l], w_out[l], ln1_g[l], ln1_b[l], router_w[l], router_bias[l],
                        shared_w_gate_up[l], shared_w_down[l], ln2_g[l], ln2_b[l])
    return xf.reshape(batch, seq, d).astype(x.dtype)
```

```python
import functools

import jax
import jax.numpy as jnp
from jax import lax
from jax.experimental import pallas as pl
from jax.experimental.pallas import tpu as pltpu

F32 = jnp.float32
BF16 = jnp.bfloat16
I32 = jnp.int32

D_MODEL = 1024
DEPTH = 2
CHUNK = 64
DN_HEADS = 8
DN_HEAD_DIM = 128
DN_WIDTH = DN_HEADS * DN_HEAD_DIM
CONV_WIDTH = 4
AT_HEADS = 8
AT_HEAD_DIM = 64
AT_WIDTH = AT_HEADS * AT_HEAD_DIM
LEFT_CHUNKS = 8
REL_CLIP = 256
N_EXPERTS = 64
TOP_K = 6
EXPERT_DIM = 256
SHARED_DIM = 256
ROUTED_SCALE = 2.5
ALPHA = (2 * DEPTH) ** 0.25
LN_EPS = 1e-5
RMS_EPS = 1e-6
L2_EPS = 1e-6
NEG_INF = -1e30

LANES = 128
SUBLANES = 8
ROW_TILES = D_MODEL // LANES

DN_TB = 256
DN_SB = 128
DN_HEADS_PER_PASS = 8
AT_QB = 256
AT_KB = 3 * AT_QB
GROUP = 1024
FFN_BM = 128
MOE_EPS = 4
MOE_WEIGHT_SLOTS = 4
ROW_LOOP_UNROLL = 16
GROUP_ROWS = GROUP * TOP_K + N_EXPERTS * (SUBLANES - 1) + FFN_BM
GROUP_ROWS = -(-GROUP_ROWS // SUBLANES) * SUBLANES
VMEM_LIMIT = 58 * 1024 * 1024

COL_Q, COL_K, COL_V, COL_Z, COL_GA, COL_GB = 0, 1, 2, 3, 4, 5
COL_AQ, COL_AK, COL_AV = 12, 13, 14
BIG_COLS = 4 * DN_WIDTH + 2 * D_MODEL + 3 * AT_WIDTH


def _sigmoid(x):
    return 1.0 / (1.0 + jnp.exp(-x))


def _silu(x):
    return x * _sigmoid(x)


def _softplus(x):
    return jnp.maximum(x, 0.0) + jnp.log1p(jnp.exp(-jnp.abs(x)))


def _bdot(a, b):
    return jnp.dot(a.astype(BF16), b.astype(BF16), preferred_element_type=F32)


def _bdot_nt(a, b):
    return lax.dot_general(a.astype(BF16), b.astype(BF16), (((1,), (1,)), ((), ())),
                           preferred_element_type=F32)


def _cparams(sem):
    return pltpu.CompilerParams(dimension_semantics=sem, vmem_limit_bytes=VMEM_LIMIT)


def _mm_kernel(x_ref, w_ref, o_ref):
    o_ref[...] = jnp.dot(x_ref[...].astype(BF16), w_ref[...],
                         preferred_element_type=F32).astype(o_ref.dtype)


def _matmul(x, w_all, layer, out_dtype, tm, tn):
    n, k = x.shape
    m = w_all.shape[1]
    return pl.pallas_call(
        _mm_kernel,
        grid=(n // tm, m // tn),
        in_specs=[pl.BlockSpec((tm, k), lambda i, j: (i, 0)),
                  pl.BlockSpec((k, tn), lambda i, j: (layer, j))],
        out_specs=pl.BlockSpec((tm, tn), lambda i, j: (i, j)),
        out_shape=jax.ShapeDtypeStruct((n, m), out_dtype),
        compiler_params=_cparams(("parallel", "parallel")),
        name="inproj",
    )(x, w_all)


def _gates_kernel(x_ref, wc_ref, wr_ref, pc_ref, pr_ref, cg_ref, rg_ref, *, tm):
    x = x_ref[...].astype(BF16)
    r = lax.broadcasted_iota(I32, (tm, tm), 0)
    c = lax.broadcasted_iota(I32, (tm, tm), 1)
    same = (r // CHUNK) == (c // CHUNK)

    pc = jnp.dot(x, wc_ref[...], preferred_element_type=F32)
    lane = lax.broadcasted_iota(I32, (tm, LANES), 1)
    col = jnp.where(lane < DN_HEADS, _sigmoid(pc),
                    -pc_ref[0:1, :] * _softplus(pc + pc_ref[1:2, :]))
    lower = (same & (r >= c)).astype(F32)
    gcc = jnp.dot(lower, col, preferred_element_type=F32, precision=lax.Precision.HIGHEST)
    cg_ref[...] = jnp.where(lane < DN_HEADS, col, gcc)

    pr = lax.dot_general(wr_ref[...], x, (((1,), (1,)), ((), ())),
                         preferred_element_type=F32)
    sub = lax.broadcasted_iota(I32, (2 * DN_HEADS, tm), 0)
    row = jnp.where(sub < DN_HEADS, _sigmoid(pr),
                    -pr_ref[:, 0:1] * _softplus(pr + pr_ref[:, 1:2]))
    upper = (same & (r <= c)).astype(F32)
    gcr = jnp.dot(row, upper, preferred_element_type=F32, precision=lax.Precision.HIGHEST)
    rg_ref[...] = jnp.where(sub < DN_HEADS, row, gcr)


def _gates(xb, w_col, w_row, p_col, p_row, tm=256):
    n = xb.shape[0]
    return pl.pallas_call(
        functools.partial(_gates_kernel, tm=tm),
        grid=(n // tm,),
        in_specs=[pl.BlockSpec((tm, D_MODEL), lambda i: (i, 0)),
                  pl.BlockSpec((D_MODEL, LANES), lambda i: (0, 0)),
                  pl.BlockSpec((2 * DN_HEADS, D_MODEL), lambda i: (0, 0)),
                  pl.BlockSpec((2, LANES), lambda i: (0, 0)),
                  pl.BlockSpec((2 * DN_HEADS, LANES), lambda i: (0, 0))],
        out_specs=[pl.BlockSpec((tm, LANES), lambda i: (i, 0)),
                   pl.BlockSpec((2 * DN_HEADS, tm), lambda i: (0, i))],
        out_shape=[jax.ShapeDtypeStruct((n, LANES), F32),
                   jax.ShapeDtypeStruct((2 * DN_HEADS, n), F32)],
        compiler_params=_cparams(("parallel",)),
        name="dn_gates",
    )(xb, w_col, w_row, p_col, p_row)


def _dn_kernel(q_ref, k_ref, v_ref, z_ref, cg_ref, rg_ref, cw_ref, nw_ref, o_ref,
               xpad, s_ref, u_s, wq_s, qkk_s, vn_s, o_s, mask_s):
    tb, nch, hd = DN_TB, DN_TB // CHUNK, DN_HEAD_DIM
    t = pl.program_id(1)

    sb = DN_SB
    halves = tb // sb

    @pl.when(t == 0)
    def _():
        xpad[:, 0:SUBLANES, :] = jnp.zeros((3, SUBLANES, DN_WIDTH), F32)
        s_ref[...] = jnp.zeros_like(s_ref)
        r = lax.broadcasted_iota(I32, (sb, sb), 0)
        c = lax.broadcasted_iota(I32, (sb, sb), 1)
        same = (r // CHUNK) == (c // CHUNK)
        mask_s[0] = jnp.where(same & (r >= c), 0.0, NEG_INF)
        mask_s[1] = (same & (r > c)).astype(F32)
        mask_s[2] = ((r // 16) == (c // 16)).astype(F32)
        mask_s[3] = (r == c).astype(F32)

    @pl.when(t > 0)
    def _():
        xpad[:, 0:SUBLANES, :] = xpad[:, tb:tb + SUBLANES, :]

    xpad[0, SUBLANES:SUBLANES + tb, :] = q_ref[...].astype(F32)
    xpad[1, SUBLANES:SUBLANES + tb, :] = k_ref[...].astype(F32)
    xpad[2, SUBLANES:SUBLANES + tb, :] = v_ref[...].astype(F32)
    vn_s[...] = jnp.zeros_like(vn_s)

    lane_chunk = lax.broadcasted_iota(I32, (hd, sb), 1) // CHUNK
    ones = jnp.ones((hd, hd), BF16)

    def row_sum_sq(x):
        return jnp.dot((x * x).astype(BF16), ones, preferred_element_type=F32)

    def each(fn, *lists):
        return [fn(*args) for args in zip(*lists)]

    for h0 in range(0, DN_HEADS, DN_HEADS_PER_PASS):
        inst = [(h, half) for h in range(h0, h0 + DN_HEADS_PER_PASS) for half in range(halves)]

        def conv(idx, h, half):
            block = xpad[idx, half * sb:half * sb + sb + SUBLANES, h * hd:(h + 1) * hd]
            acc = None
            for i in range(CONV_WIDTH):
                col0 = idx * DN_WIDTH + h * hd
                back = CONV_WIDTH - 1 - i
                shifted = block if back == 0 else pltpu.roll(block, back, 0)
                term = shifted[SUBLANES:SUBLANES + sb, :] * cw_ref[i:i + 1, col0:col0 + hd]
                acc = term if acc is None else acc + term
            return _silu(acc)

        qh = [conv(0, h, half) for h, half in inst]
        kh = [conv(1, h, half) for h, half in inst]
        vh = [conv(2, h, half) for h, half in inst]
        qh = each(lambda x: x * lax.rsqrt(row_sum_sq(x) + L2_EPS) * (hd ** -0.5), qh)
        kh = each(lambda x: x * lax.rsqrt(row_sum_sq(x) + L2_EPS), kh)
        beta = [cg_ref[half * sb:(half + 1) * sb, h:h + 1] for h, half in inst]
        gc = [cg_ref[half * sb:(half + 1) * sb, DN_HEADS + h:DN_HEADS + h + 1] for h, half in inst]
        gcr = [rg_ref[DN_HEADS + h:DN_HEADS + h + 1, half * sb:(half + 1) * sb] for h, half in inst]
        dec = each(lambda a, b: jnp.exp((a - b) + mask_s[0]), gc, gcr)
        kb = each(lambda a, b: a * b, kh, beta)
        khb = each(lambda a: a.astype(BF16), kh)
        lmat = each(lambda a, b, d: _bdot_nt(a, b) * d * mask_s[1], kb, khb, dec)
        qk = each(lambda a, b, d: _bdot_nt(a, b) * d, qh, khb, dec)
        eye = mask_s[3]
        dmat = each(lambda a: a * mask_s[2], lmat)
        omat = each(lambda a, b: a - b, lmat, dmat)
        s1 = each(lambda a: eye - a, dmat)
        d2 = each(lambda a: _bdot(a, a), dmat)
        s2 = each(lambda a, b: a + _bdot(a, b), s1, d2)
        d4 = each(lambda a: _bdot(a, a), d2)
        s3 = each(lambda a, b: a + _bdot(a, b), s2, d4)
        d8 = each(lambda a: _bdot(a, a), d4)
        dinv = each(lambda a, b: a + _bdot(a, b), s3, d8)
        m1 = each(_bdot, dinv, omat)
        m2 = each(lambda a: _bdot(a, a), m1)
        im = each(lambda a: eye - a, m1)
        pm = each(lambda a, b: a + _bdot(a, b), im, m2)
        tinv = each(_bdot, pm, dinv)
        eg = each(jnp.exp, gc)
        rhs = each(lambda v, b, k, e: jnp.concatenate([v * b, k * e], axis=1), vh, beta, kb, eg)
        uw = each(_bdot, tinv, rhs)
        for i, (h, half) in enumerate(inst):
            rs = slice(half * sb, (half + 1) * sb)
            u_s[h, rs, :] = uw[i][:, :hd]
            w_b = uw[i][:, hd:].astype(BF16)
            qd_b = (qh[i] * eg[i]).astype(BF16)
            qk_b = qk[i].astype(BF16)
            gl = jnp.concatenate(
                [jnp.broadcast_to(gc[i][cc * CHUNK + CHUNK - 1:cc * CHUNK + CHUNK, :], (CHUNK, 1))
                 for cc in range(sb // CHUNK)], axis=0)
            kdt = (kh[i] * jnp.exp(gl - gc[i])).T
            for cc in range(sb // CHUNK):
                chunk = half * (sb // CHUNK) + cc
                cr = slice(cc * CHUNK, (cc + 1) * CHUNK)
                wq_s[h, chunk, 0:CHUNK, :] = w_b[cr, :]
                wq_s[h, chunk, CHUNK:2 * CHUNK, :] = qd_b[cr, :]
                qkk_s[h, chunk, 0:CHUNK, :] = qk_b[cr, :]
                qkk_s[h, chunk, CHUNK:CHUNK + hd, :] = jnp.where(lane_chunk == cc, kdt,
                                                                  0.0).astype(BF16)

    heads = list(range(DN_HEADS))

    def chunk_step(cc, carry):
        r0 = pl.multiple_of(cc * CHUNK, CHUNK)
        rows = pl.ds(r0, CHUNK)
        sub_rows = pl.ds(pl.multiple_of((cc // (sb // CHUNK)) * sb, sb), sb)
        s_old = [s_ref[h] for h in heads]
        s_b = each(lambda a: a.astype(BF16), s_old)
        by_state = [jnp.dot(wq_s[h, cc], s_b[h], preferred_element_type=F32) for h in heads]
        for h in heads:
            vn_s[h, rows, :] = (u_s[h, rows, :] - by_state[h][:CHUNK, :]).astype(BF16)
        vn_all = [vn_s[h, sub_rows, :] for h in heads]
        by_vnew = [jnp.dot(qkk_s[h, cc], vn_all[h], preferred_element_type=F32) for h in heads]
        for h in heads:
            g_last = cg_ref[pl.ds(r0 + CHUNK - 1, 1), DN_HEADS + h:DN_HEADS + h + 1]
            s_ref[h] = s_old[h] * jnp.exp(g_last) + by_vnew[h][CHUNK:, :]
            o_s[rows, h * hd:(h + 1) * hd] = by_state[h][CHUNK:, :] + by_vnew[h][:CHUNK, :]
        return carry

    lax.fori_loop(0, nch, chunk_step, 0)

    for h in range(DN_HEADS):
        sl = slice(h * hd, (h + 1) * hd)
        o = o_s[:, sl]
        o = o * lax.rsqrt(row_sum_sq(o) * (1.0 / hd) + RMS_EPS)
        z = z_ref[:, sl].astype(F32)
        o_ref[:, sl] = (o * nw_ref[...] * _silu(z)).astype(o_ref.dtype)


def _deltanet(proj, cg, rg, conv_w, norm_w, batch, seq):
    n = batch * seq
    nt = seq // DN_TB
    tb, hd, nch = DN_TB, DN_HEAD_DIM, DN_TB // CHUNK

    def col(j):
        return pl.BlockSpec((tb, DN_WIDTH), lambda b, t, j=j: (b * nt + t, j))

    return pl.pallas_call(
        _dn_kernel,
        grid=(batch, nt),
        in_specs=[col(COL_Q), col(COL_K), col(COL_V), col(COL_Z),
                  pl.BlockSpec((tb, LANES), lambda b, t: (b * nt + t, 0)),
                  pl.BlockSpec((2 * DN_HEADS, tb), lambda b, t: (0, b * nt + t)),
                  pl.BlockSpec((CONV_WIDTH, 3 * DN_WIDTH), lambda b, t: (0, 0)),
                  pl.BlockSpec((1, hd), lambda b, t: (0, 0))],
        out_specs=pl.BlockSpec((tb, DN_WIDTH), lambda b, t: (b * nt + t, 0)),
        out_shape=jax.ShapeDtypeStruct((n, DN_WIDTH), BF16),
        scratch_shapes=[
            pltpu.VMEM((3, tb + SUBLANES, DN_WIDTH), F32),
            pltpu.VMEM((DN_HEADS, hd, hd), F32),
            pltpu.VMEM((DN_HEADS, tb, hd), F32),
            pltpu.VMEM((DN_HEADS, nch, 2 * CHUNK, hd), BF16),
            pltpu.VMEM((DN_HEADS, nch, CHUNK + hd, DN_SB), BF16),
            pltpu.VMEM((DN_HEADS, tb, hd), BF16),
            pltpu.VMEM((tb, DN_WIDTH), F32),
            pltpu.VMEM((4, DN_SB, DN_SB), F32),
        ],
        compiler_params=_cparams(("parallel", "arbitrary")),
        name="deltanet",
    )(proj, proj, proj, proj, cg, rg, conv_w, norm_w)


def _attn_kernel(q_ref, k0_ref, k1_ref, k2_ref, v0_ref, v1_ref, v2_ref, bias_ref, o_ref):
    t = pl.program_id(1)
    q = q_ref[...]
    k = jnp.concatenate([k0_ref[...], k1_ref[...], k2_ref[...]], axis=0)
    v = jnp.concatenate([v0_ref[...], v1_ref[...], v2_ref[...]], axis=0)
    kidx = lax.broadcasted_iota(I32, (1, AT_KB), 1)
    key_valid = kidx >= (2 - t) * AT_QB
    lane_head = lax.broadcasted_iota(I32, (1, LANES), 1) // AT_HEAD_DIM
    scale = AT_HEAD_DIM ** -0.5
    def scores(head):
        p, mine = head // 2, lane_head == head % 2
        sl = slice(p * LANES, (p + 1) * LANES)
        qm = jnp.where(mine, q[:, sl] * scale, jnp.zeros((AT_QB, LANES), BF16))
        s = lax.dot_general(qm, k[:, sl], (((1,), (1,)), ((), ())), preferred_element_type=F32)
        return jnp.where(key_valid, s + bias_ref[head], NEG_INF)

    def weighted_values(head, s, out):
        p, mine = head // 2, lane_head == head % 2
        vp = v[:, p * LANES:(p + 1) * LANES]
        e = jnp.exp(s - jnp.max(s, axis=-1, keepdims=True))
        v_aug = jnp.where(mine, vp, jnp.ones_like(vp))
        pv = jnp.dot(e.astype(BF16), v_aug, preferred_element_type=F32)
        row_sum = pltpu.roll(pv, AT_HEAD_DIM, 1)
        return jnp.where(mine, pv / row_sum, out)

    outs = [jnp.zeros((AT_QB, LANES), F32)] * (AT_HEADS // 2)
    s_prev = scores(0)
    for head in range(AT_HEADS):
        s_next = scores(head + 1) if head + 1 < AT_HEADS else None
        outs[head // 2] = weighted_values(head, s_prev, outs[head // 2])
        s_prev = s_next
    for p in range(AT_HEADS // 2):
        o_ref[:, p * LANES:(p + 1) * LANES] = outs[p].astype(o_ref.dtype)


def _attention(proj, bias, batch, seq):
    n = batch * seq
    nt = seq // AT_QB

    def kv(col, back):
        return pl.BlockSpec((AT_QB, AT_WIDTH),
                            lambda b, t, col=col, back=back: (b * nt + jnp.maximum(t - back, 0), col))

    return pl.pallas_call(
        _attn_kernel,
        grid=(batch, nt),
        in_specs=[pl.BlockSpec((AT_QB, AT_WIDTH), lambda b, t: (b * nt + t, COL_AQ)),
                  kv(COL_AK, 2), kv(COL_AK, 1), kv(COL_AK, 0),
                  kv(COL_AV, 2), kv(COL_AV, 1), kv(COL_AV, 0),
                  pl.BlockSpec((AT_HEADS, AT_QB, AT_KB), lambda b, t: (0, 0, 0))],
        out_specs=pl.BlockSpec((AT_QB, AT_WIDTH), lambda b, t: (b * nt + t, 0)),
        out_shape=jax.ShapeDtypeStruct((n, AT_WIDTH), BF16),
        compiler_params=_cparams(("parallel", "parallel")),
        name="band_attention",
    )(proj, proj, proj, proj, proj, proj, proj, bias)


def _bias_kernel(f_ref, o_ref):
    width = f_ref.shape[1]
    qi = lax.broadcasted_iota(I32, (AT_QB, AT_KB), 0) // CHUNK
    ki = lax.broadcasted_iota(I32, (AT_QB, AT_KB), 1) // CHUNK
    band = (ki >= qi) & (ki <= qi + LEFT_CHUNKS)
    for h in range(AT_HEADS):
        base = jnp.broadcast_to(f_ref[h:h + 1, :], (AT_QB, width))
        rolled = pltpu.roll(base, width - (AT_QB - 1), 1, stride=1, stride_axis=0)
        o_ref[h] = jnp.where(band, rolled[:, :AT_KB], NEG_INF)


def _attention_bias(rel_table):
    table = rel_table.astype(F32)
    n_clipped = (AT_KB - 1) - REL_CLIP
    f = jnp.concatenate([jnp.broadcast_to(table[:, :1], (AT_HEADS, n_clipped)),
                         table[:, :AT_QB + REL_CLIP + 1]], axis=1)
    return pl.pallas_call(
        _bias_kernel,
        out_shape=jax.ShapeDtypeStruct((AT_HEADS, AT_QB, AT_KB), F32),
        name="attn_bias",
    )(f)


def _layer_norm(y, g, b):
    mu = jnp.mean(y, axis=-1, keepdims=True)
    d = y - mu
    var = jnp.mean(d * d, axis=-1, keepdims=True)
    return d * lax.rsqrt(var + LN_EPS) * g + b


def _load_row_tiles(ref):
    m = ref.shape[0] // ROW_TILES
    return jnp.concatenate([ref[pl.ds(j, m, stride=ROW_TILES), :] for j in range(ROW_TILES)],
                           axis=1)


def _store_row_tiles(ref, x):
    m = x.shape[0]
    for j in range(ROW_TILES):
        ref[pl.ds(j, m, stride=ROW_TILES), :] = x[:, j * LANES:(j + 1) * LANES]


def _merge_kernel(dn_ref, at_ref, ga_ref, gb_ref, x_ref, wa_ref, wb_ref, wo_ref, g_ref, b_ref,
                  x1t_ref):
    ya = jnp.dot(dn_ref[...], wa_ref[...], preferred_element_type=F32)
    yb = jnp.dot(at_ref[...], wb_ref[...], preferred_element_type=F32)
    mix = _sigmoid(ga_ref[...].astype(F32)) * ya + _sigmoid(gb_ref[...].astype(F32)) * yb
    y = ALPHA * x_ref[...] + jnp.dot(mix.astype(BF16), wo_ref[...], preferred_element_type=F32)
    _store_row_tiles(x1t_ref, _layer_norm(y, g_ref[...], b_ref[...]))


def _merge(dn, at, proj, x, wa, wb, wo, g, b, tm=512):
    n = x.shape[0]
    return pl.pallas_call(
        _merge_kernel,
        grid=(n // tm,),
        in_specs=[pl.BlockSpec((tm, DN_WIDTH), lambda i: (i, 0)),
                  pl.BlockSpec((tm, AT_WIDTH), lambda i: (i, 0)),
                  pl.BlockSpec((tm, D_MODEL), lambda i: (i, COL_GA)),
                  pl.BlockSpec((tm, D_MODEL), lambda i: (i, COL_GB)),
                  pl.BlockSpec((tm, D_MODEL), lambda i: (i, 0)),
                  pl.BlockSpec((DN_WIDTH, D_MODEL), lambda i: (0, 0)),
                  pl.BlockSpec((AT_WIDTH, D_MODEL), lambda i: (0, 0)),
                  pl.BlockSpec((D_MODEL, D_MODEL), lambda i: (0, 0)),
                  pl.BlockSpec((1, D_MODEL), lambda i: (0, 0)),
                  pl.BlockSpec((1, D_MODEL), lambda i: (0, 0))],
        out_specs=pl.BlockSpec((tm * ROW_TILES, LANES), lambda i: (i, 0)),
        out_shape=jax.ShapeDtypeStruct((n * ROW_TILES, LANES), F32),
        compiler_params=_cparams(("parallel",)),
        name="merge_ln1",
    )(dn, at, proj, proj, x, wa, wb, wo, g, b)


def _router_kernel(xt_ref, rw_ref, rb_ref, dest_ref, wt_ref, seg_ref):
    gt, ne = GROUP, N_EXPERTS
    logits = lax.dot_general(rw_ref[...], _load_row_tiles(xt_ref), (((1,), (1,)), ((), ())),
                             preferred_element_type=F32, precision=lax.Precision.HIGHEST)
    scores = _sigmoid(logits)
    sel = scores + rb_ref[:, 0:1]
    eidx = lax.broadcasted_iota(I32, (ne, gt), 0)
    picked = jnp.zeros((ne, gt), jnp.bool_)
    hits, vals = [], []
    for _ in range(TOP_K):
        m = jnp.max(sel, axis=0, keepdims=True)
        idx = jnp.min(jnp.where(sel == m, eidx, ne), axis=0, keepdims=True)
        hit = eidx == idx
        hits.append(hit)
        vals.append(jnp.sum(jnp.where(hit, scores, 0.0), axis=0, keepdims=True))
        sel = jnp.where(hit, -jnp.inf, sel)
        picked = picked | hit
    total = vals[0]
    for v in vals[1:]:
        total = total + v

    onehot = picked.astype(BF16)
    tr = lax.broadcasted_iota(I32, (gt, gt), 0)
    tc = lax.broadcasted_iota(I32, (gt, gt), 1)
    rank = jnp.dot(onehot, (tr < tc).astype(BF16), preferred_element_type=F32)
    count = jnp.sum(picked.astype(F32), axis=1, keepdims=True)
    tiles = jnp.floor((count + (SUBLANES - 1)) * (1.0 / SUBLANES))
    er = lax.broadcasted_iota(I32, (ne, ne), 0)
    ec = lax.broadcasted_iota(I32, (ne, ne), 1)
    start = SUBLANES * jnp.dot((ec < er).astype(BF16),
                               jnp.broadcast_to(tiles, (ne, LANES)).astype(BF16),
                               preferred_element_type=F32)
    slot = start[:, 0:1] + rank

    sub = lax.broadcasted_iota(I32, (SUBLANES, gt), 0)
    dest = jnp.zeros((SUBLANES, gt), F32)
    wts = jnp.zeros((SUBLANES, gt), F32)
    for kk in range(TOP_K):
        d = jnp.sum(jnp.where(hits[kk], slot, 0.0), axis=0, keepdims=True)
        dest = jnp.where(sub == kk, d, dest)
        wts = jnp.where(sub == kk, vals[kk] / total * ROUTED_SCALE, wts)
    dest_ref[0] = dest.astype(I32) * ROW_TILES
    wt_ref[0] = wts
    lane = lax.broadcasted_iota(I32, (ne, LANES), 1)
    seg_ref[0] = jnp.where(lane == 0, start, jnp.broadcast_to(count, (ne, LANES))).astype(I32)


def _router(x1t, rw_t, rb):
    n = x1t.shape[0] // ROW_TILES
    ng = n // GROUP
    return pl.pallas_call(
        _router_kernel,
        grid=(ng,),
        in_specs=[pl.BlockSpec((GROUP * ROW_TILES, LANES), lambda g: (g, 0)),
                  pl.BlockSpec((N_EXPERTS, D_MODEL), lambda g: (0, 0)),
                  pl.BlockSpec((N_EXPERTS, LANES), lambda g: (0, 0))],
        out_specs=[pl.BlockSpec((1, SUBLANES, GROUP), lambda g: (g, 0, 0)),
                   pl.BlockSpec((1, SUBLANES, GROUP), lambda g: (g, 0, 0)),
                   pl.BlockSpec((1, N_EXPERTS, LANES), lambda g: (g, 0, 0))],
        out_shape=[jax.ShapeDtypeStruct((ng, SUBLANES, GROUP), I32),
                   jax.ShapeDtypeStruct((ng, SUBLANES, GROUP), F32),
                   jax.ShapeDtypeStruct((ng, N_EXPERTS, LANES), I32)],
        compiler_params=_cparams(("parallel",)),
        name="router",
    )(x1t, rw_t, rb)


def _moe_kernel(start_ref, count_ref, xt_hbm, dest_hbm, wt_hbm, wgu_hbm, wd_hbm, out_ref,
                buf, dest_s, wt_s, sem, wgu_buf, wd_buf, wsem, *, first_expert, n_groups):
    g = pl.program_id(0)
    e = pl.program_id(1)
    ne, gt = N_EXPERTS, GROUP
    n_steps = N_EXPERTS // MOE_EPS
    n_blocks_total = n_groups * n_steps
    blk = g * n_steps + e
    slot = blk % MOE_WEIGHT_SLOTS

    def routing_copies():
        rows = pl.ds(pl.multiple_of(g * (gt * ROW_TILES), gt * ROW_TILES), gt * ROW_TILES)
        return (pltpu.make_async_copy(dest_hbm.at[g], dest_s, sem.at[0]),
                pltpu.make_async_copy(wt_hbm.at[g], wt_s, sem.at[1]),
                pltpu.make_async_copy(xt_hbm.at[rows], out_ref, sem.at[2]))

    def weight_copies(b, s):
        first = first_expert + (b % n_steps) * MOE_EPS
        return (pltpu.make_async_copy(wgu_hbm.at[pl.ds(first, MOE_EPS)], wgu_buf.at[s],
                                      wsem.at[s, 0]),
                pltpu.make_async_copy(wd_hbm.at[pl.ds(first, MOE_EPS)], wd_buf.at[s],
                                      wsem.at[s, 1]))

    @pl.when(blk == 0)
    def _():
        for b in range(min(MOE_WEIGHT_SLOTS, n_blocks_total)):
            for cp in weight_copies(b, b):
                cp.start()

    def tile_of(row8):
        return pl.ds(pl.multiple_of(row8, ROW_TILES), ROW_TILES)

    @pl.when(e == 0)
    def _():
        for cp in routing_copies():
            cp.start()
        zeros = jnp.zeros((SUBLANES * ROW_TILES, LANES), F32)

        def zero_pad(ee, carry):
            s = start_ref[g * ne + ee]
            cnt = count_ref[g * ne + ee]
            z0 = pl.multiple_of((s + (cnt // SUBLANES) * SUBLANES) * ROW_TILES, SUBLANES * ROW_TILES)
            buf[pl.ds(z0, SUBLANES * ROW_TILES), :] = zeros
            return carry

        lax.fori_loop(0, ne, zero_pad, 0)
        last = g * ne + ne - 1
        end = start_ref[last] + (count_ref[last] + SUBLANES - 1) // SUBLANES * SUBLANES
        end = pl.multiple_of(end * ROW_TILES, SUBLANES * ROW_TILES)
        buf[pl.ds(end, FFN_BM * ROW_TILES), :] = jnp.zeros((FFN_BM * ROW_TILES, LANES), F32)
        for cp in routing_copies():
            cp.wait()

        def scatter(t0, carry):
            for u in range(ROW_LOOP_UNROLL):
                tok = t0 * ROW_LOOP_UNROLL + u
                row = out_ref[tile_of(tok * ROW_TILES), :]
                for kk in range(TOP_K):
                    buf[tile_of(dest_s[kk * gt + tok]), :] = row
            return carry

        lax.fori_loop(0, gt // ROW_LOOP_UNROLL, scatter, 0)

    experts = list(range(MOE_EPS))
    seg = [(start_ref[g * ne + e * MOE_EPS + k], count_ref[g * ne + e * MOE_EPS + k])
           for k in experts]
    n_blocks = [(cnt + FFN_BM - 1) // FFN_BM for _, cnt in seg]
    last_block_row = GROUP_ROWS - FFN_BM

    def ffn_block(i, carry):
        r0 = [pl.multiple_of(jnp.minimum(s + i * FFN_BM, last_block_row), SUBLANES) for s, _ in seg]
        base = [pl.multiple_of(r * ROW_TILES, SUBLANES * ROW_TILES) for r in r0]
        xin = [jnp.concatenate([buf[pl.ds(base[k] + j, FFN_BM, stride=ROW_TILES), :]
                                for j in range(ROW_TILES)], axis=1) for k in experts]
        gu = [jnp.dot(xin[k].astype(BF16), wgu_buf[slot, k], preferred_element_type=F32)
              for k in experts]
        hidden = [(_silu(gu[k][:, :EXPERT_DIM]) * gu[k][:, EXPERT_DIM:]).astype(BF16)
                  for k in experts]
        y = [jnp.dot(hidden[k], wd_buf[slot, k], preferred_element_type=F32) for k in experts]
        for k in experts:
            s, cnt = seg[k]
            keep = (r0[k] + lax.broadcasted_iota(I32, (FFN_BM, 1), 0)) < s + cnt
            for j in range(ROW_TILES):
                buf[pl.ds(base[k] + j, FFN_BM, stride=ROW_TILES), :] = jnp.where(
                    keep, y[k][:, j * LANES:(j + 1) * LANES], xin[k][:, j * LANES:(j + 1) * LANES])
        return carry

    total_blocks = n_blocks[0]
    for nb in n_blocks[1:]:
        total_blocks = jnp.maximum(total_blocks, nb)
    for cp in weight_copies(blk, slot):
        cp.wait()
    lax.fori_loop(0, total_blocks, ffn_block, 0)

    @pl.when(blk + MOE_WEIGHT_SLOTS < n_blocks_total)
    def _():
        for cp in weight_copies(blk + MOE_WEIGHT_SLOTS, slot):
            cp.start()

    @pl.when(e == pl.num_programs(1) - 1)
    def _():
        def combine(t0, carry):
            for u in range(ROW_LOOP_UNROLL):
                tok = t0 * ROW_LOOP_UNROLL + u
                acc = buf[tile_of(dest_s[tok]), :] * wt_s[tok]
                for kk in range(1, TOP_K):
                    acc = acc + buf[tile_of(dest_s[kk * gt + tok]), :] * wt_s[kk * gt + tok]
                out_ref[tile_of(tok * ROW_TILES), :] = acc
            return carry

        lax.fori_loop(0, gt // ROW_LOOP_UNROLL, combine, 0)


def _moe(x1t, dest, wts, starts, counts, wgu_all, wd_all, layer):
    n = x1t.shape[0] // ROW_TILES
    ng = n // GROUP
    grid_spec = pltpu.PrefetchScalarGridSpec(
        num_scalar_prefetch=2,
        grid=(ng, N_EXPERTS // MOE_EPS),
        in_specs=[pl.BlockSpec(memory_space=pl.ANY)] * 5,
        out_specs=pl.BlockSpec((GROUP * ROW_TILES, LANES), lambda g, e, *_: (g, 0),
                               pipeline_mode=pl.Buffered(1)),
        scratch_shapes=[pltpu.VMEM((GROUP_ROWS * ROW_TILES, LANES), F32),
                        pltpu.SMEM((SUBLANES * GROUP,), I32),
                        pltpu.SMEM((SUBLANES * GROUP,), F32),
                        pltpu.SemaphoreType.DMA((3,)),
                        pltpu.VMEM((MOE_WEIGHT_SLOTS, MOE_EPS, D_MODEL, 2 * EXPERT_DIM), BF16),
                        pltpu.VMEM((MOE_WEIGHT_SLOTS, MOE_EPS, EXPERT_DIM, D_MODEL), BF16),
                        pltpu.SemaphoreType.DMA((MOE_WEIGHT_SLOTS, 2))],
    )
    return pl.pallas_call(
        functools.partial(_moe_kernel, first_expert=layer * N_EXPERTS, n_groups=ng),
        grid_spec=grid_spec,
        out_shape=jax.ShapeDtypeStruct((n * ROW_TILES, LANES), F32),
        compiler_params=_cparams(("arbitrary", "arbitrary")),
        name="routed_experts",
    )(starts, counts, x1t, dest, wts, wgu_all, wd_all)


def _final_kernel(x1t_ref, rt_ref, wsg_ref, wsd_ref, g_ref, b_ref, o_ref):
    x1 = _load_row_tiles(x1t_ref)
    gus = jnp.dot(x1.astype(BF16), wsg_ref[...], preferred_element_type=F32)
    hidden = _silu(gus[:, :SHARED_DIM]) * gus[:, SHARED_DIM:]
    shared = jnp.dot(hidden.astype(BF16), wsd_ref[...], preferred_element_type=F32)
    y = ALPHA * x1 + (_load_row_tiles(rt_ref) + shared)
    o_ref[...] = _layer_norm(y, g_ref[...], b_ref[...])


def _final(x1t, routed_t, wsg, wsd, g, b, tm=512):
    n = x1t.shape[0] // ROW_TILES
    return pl.pallas_call(
        _final_kernel,
        grid=(n // tm,),
        in_specs=[pl.BlockSpec((tm * ROW_TILES, LANES), lambda i: (i, 0)),
                  pl.BlockSpec((tm * ROW_TILES, LANES), lambda i: (i, 0)),
                  pl.BlockSpec((D_MODEL, 2 * SHARED_DIM), lambda i: (0, 0)),
                  pl.BlockSpec((SHARED_DIM, D_MODEL), lambda i: (0, 0)),
                  pl.BlockSpec((1, D_MODEL), lambda i: (0, 0)),
                  pl.BlockSpec((1, D_MODEL), lambda i: (0, 0))],
        out_specs=pl.BlockSpec((tm, D_MODEL), lambda i: (i, 0)),
        out_shape=jax.ShapeDtypeStruct((n, D_MODEL), F32),
        compiler_params=_cparams(("parallel",)),
        name="shared_ln2",
    )(x1t, routed_t, wsg, wsd, g, b)


C_SMALL = 4 * DN_WIDTH
C_ATT = C_SMALL + 2 * DN_HEADS
C_GATE = C_ATT + 3 * AT_WIDTH


def _layer(l, x, batch, seq, w_big_all, wgu_all, wd_all, w_in, conv_w, a_log, dt_bias,
           dn_norm_w, rel_table, w_branch_a, w_branch_b, w_out, ln1_g, ln1_b, router_w,
           router_bias, ws_gate_up, ws_down, ln2_g, ln2_b):
    w_small = w_in[:, C_SMALL:C_ATT]
    w_col = jnp.pad(w_small, ((0, 0), (0, LANES - 2 * DN_HEADS))).astype(BF16)
    w_row = w_small.T.astype(BF16)
    decay_rate = jnp.exp(a_log.astype(F32))
    zeros8 = jnp.zeros((DN_HEADS,), F32)
    p_col = jnp.pad(jnp.stack([jnp.concatenate([zeros8, decay_rate]),
                               jnp.concatenate([zeros8, dt_bias.astype(F32)])]),
                    ((0, 0), (0, LANES - 2 * DN_HEADS)))
    p_row = jnp.pad(p_col[:, :2 * DN_HEADS].T, ((0, 0), (0, LANES - 2)))

    proj = _matmul(x, w_big_all, l, BF16, tm=1024, tn=1280)
    cg, rg = _gates(x, w_col, w_row, p_col, p_row)
    dn = _deltanet(proj, cg, rg, conv_w.astype(F32), dn_norm_w.astype(F32).reshape(1, DN_HEAD_DIM),
                   batch, seq)
    at = _attention(proj, _attention_bias(rel_table), batch, seq)
    x1t = _merge(dn, at, proj, x, w_branch_a.astype(BF16), w_branch_b.astype(BF16),
                 w_out.astype(BF16), ln1_g.reshape(1, -1), ln1_b.reshape(1, -1))
    dest, wts, seg = _router(x1t, router_w.T.astype(F32),
                             jnp.broadcast_to(router_bias.astype(F32)[:, None], (N_EXPERTS, LANES)))
    ng = dest.shape[0]
    routed_t = _moe(x1t, dest.reshape(ng, SUBLANES * GROUP), wts.reshape(ng, SUBLANES * GROUP),
                    seg[:, :, 0].reshape(-1), seg[:, :, 1].reshape(-1), wgu_all, wd_all, l)
    return _final(x1t, routed_t, ws_gate_up.astype(BF16), ws_down.astype(BF16),
                  ln2_g.reshape(1, -1), ln2_b.reshape(1, -1))


def kernel(x, w_in, conv_w, a_log, dt_bias, dn_norm_w, rel_table, w_branch_a, w_branch_b, w_out,
           ln1_g, ln1_b, router_w, router_bias, expert_w_gate_up, expert_w_down,
           shared_w_gate_up, shared_w_down, ln2_g, ln2_b):
    batch, seq, d = x.shape
    assert d == D_MODEL and seq % DN_TB == 0 and seq % AT_QB == 0 and (batch * seq) % GROUP == 0
    depth = w_in.shape[0]
    xf = x.reshape(batch * seq, d).astype(F32)
    w_big_all = jnp.concatenate([w_in[:, :, :C_SMALL], w_in[:, :, C_GATE:], w_in[:, :, C_ATT:C_GATE]],
                                axis=2).astype(BF16).reshape(depth * D_MODEL, BIG_COLS)
    wgu_all = expert_w_gate_up.astype(BF16).reshape(depth * N_EXPERTS, D_MODEL, 2 * EXPERT_DIM)
    wd_all = expert_w_down.astype(BF16).reshape(depth * N_EXPERTS, EXPERT_DIM, D_MODEL)
    for l in range(depth):
        xf = _layer(l, xf, batch, seq, w_big_all, wgu_all, wd_all, w_in[l], conv_w[l],
                        a_log[l], dt_bias[l], dn_norm_w[l], rel_table[l], w_branch_a[l],
                        w_branch_b[l], w_out[l], ln1_g[l], ln1_b[l], router_w[l], router_bias[l],
                        shared_w_gate_up[l], shared_w_down[l], ln2_g[l], ln2_b[l])
    return xf.reshape(batch, seq, d).astype(x.dtype)
```
